```python
import jax
import jax.numpy as jnp
from jax import lax
import numpy as np

D_MODEL = 1024
BATCH = 16
SEQ = 256
DEPTH = 4
DEC_BATCH = 8
DEC_SEQ = 4096
PAST_LEN = 512

GRID_W = 64
HEAD_DIM = 64
EPS = 1e-6
MASK_VALUE = -1e30
NA_HEADS = 8
NA_WIN_R = 8
NA_WIN_C = 16
SWA_HEADS = 8
SWA_KV_HEADS = 2
SWA_WINDOW = 128
SWA_BLOCK = 128
ROPE_BASE = 10000.0
ATTN_Q_BLOCK = 128
CONV_CH = 512
CONV_WIDTH = 31
HG_HEADS = 4
HG_DK = 128
HG_DV = 128
HG_CHUNK = 64
N_GROUPS = 4
EXPERTS_PER_GROUP = 8
N_EXPERTS = N_GROUPS * EXPERTS_PER_GROUP
TOP_K = 2
D_EXPERT = 512
MOE_BLOCK = 128
NA_W = NA_HEADS * HEAD_DIM
SWA_Q_W = SWA_HEADS * HEAD_DIM
SWA_KV_W = SWA_KV_HEADS * HEAD_DIM
D_IN_EVEN = 3 * NA_W + SWA_Q_W + 2 * SWA_KV_W
HG_W = HG_HEADS * HG_DK
HG_VW = HG_HEADS * HG_DV
D_IN_ODD = 2 * CONV_CH + 3 * HG_W + 2 * HG_VW
D_MIX = NA_W + SWA_Q_W
N_EVEN = (DEPTH + 1) // 2
N_ODD = DEPTH // 2

kernel_name = 'hybrid_dit_na_swa_conformer_hgrn2_hmoe'


def rmsnorm(x, g):
    xf = x.astype(jnp.float32)
    y = xf * lax.rsqrt(jnp.mean(xf * xf, axis=-1, keepdims=True) + EPS)
    return (y * g.astype(jnp.float32)).astype(x.dtype)


def ada_params(cvec, w_mod, b_mod):
    m = (jax.nn.silu(cvec) @ w_mod + b_mod)[..., None, :]
    return jnp.split(m, 6, axis=-1)


def modulate(x, g, shift, scale):
    return rmsnorm(x, g) * (1 + scale) + shift


def to_heads(u, n_heads, head_dim):
    b, t, _ = u.shape
    return u.reshape(b, t, n_heads, head_dim).transpose(0, 2, 1, 3)


def from_heads(u):
    b, h, t, d = u.shape
    return u.transpose(0, 2, 1, 3).reshape(b, t, h * d)


def split_cols(p, widths):
    idx, acc = [], 0
    for w in widths[:-1]:
        acc += w
        idx.append(acc)
    return jnp.split(p, idx, axis=-1)


def rope_2d(t):
    n_tok = t.shape[2]
    pos = jnp.arange(n_tok)
    row = (pos // GRID_W).astype(jnp.float32)
    col = (pos % GRID_W).astype(jnp.float32)
    half = HEAD_DIM // 2
    inv = ROPE_BASE ** (-jnp.arange(0, half, 2, dtype=jnp.float32) / half)

    def rot(u, p):
        ang = p[:, None] * inv[None, :]
        cos, sin = jnp.cos(ang), jnp.sin(ang)
        u1, u2 = u[..., : half // 2], u[..., half // 2:]
        return jnp.concatenate([u1 * cos - u2 * sin, u1 * sin + u2 * cos], axis=-1)

    tf = t.astype(jnp.float32)
    out = jnp.concatenate([rot(tf[..., :half], row), rot(tf[..., half:], col)], axis=-1)
    return out.astype(t.dtype)


def ctx_self_attention(q, k, v, sink):
    b, kvh, g, L, hd = q.shape
    nb = L // ATTN_Q_BLOCK
    qb = q.reshape(b, kvh, g, nb, ATTN_Q_BLOCK, hd).transpose(3, 0, 1, 2, 4, 5)
    scale = hd ** -0.5

    def one(qi):
        s = jnp.einsum('bkgqd,bkld->bkgql', qi, k).astype(jnp.float32) * scale
        if sink is not None:
            s_sink = jnp.broadcast_to(sink.astype(jnp.float32)[None, :, :, None, None], s.shape[:-1] + (1,))
            p = jax.nn.softmax(jnp.concatenate([s, s_sink], axis=-1), axis=-1)[..., :-1]
        else:
            p = jax.nn.softmax(s, axis=-1)
        return jnp.einsum('bkgql,bkld->bkgqd', p.astype(v.dtype), v)

    o = lax.map(one, qb)
    return o.transpose(1, 2, 3, 0, 4, 5).reshape(b, kvh, g, L, hd)


def neighbourhood_attention(q, k, v, k_ctx, v_ctx, rel_bias):
    b, h, t, hd = q.shape
    rows = t // GRID_W
    wr = min(NA_WIN_R, rows)
    wc = NA_WIN_C
    n_loc = wr * wc
    scale = hd ** -0.5
    qg = q.reshape(b, h, rows, GRID_W, hd)
    kg = k.reshape(b, h, rows, GRID_W, hd)
    vg = v.reshape(b, h, rows, GRID_W, hd)
    cols = jnp.arange(GRID_W)
    c0 = jnp.clip(cols - wc // 2, 0, GRID_W - wc)
    col_idx = c0[:, None] + jnp.arange(wc)[None, :]
    dc = col_idx - cols[:, None] + (NA_WIN_C - 1)

    def one_row(r):
        r0 = jnp.clip(r - wr // 2, 0, rows - wr)
        k_band = lax.dynamic_slice_in_dim(kg, r0, wr, axis=2)
        v_band = lax.dynamic_slice_in_dim(vg, r0, wr, axis=2)
        kw = k_band[:, :, :, col_idx]
        vw = v_band[:, :, :, col_idx]
        qr = lax.dynamic_index_in_dim(qg, r, axis=2, keepdims=False)
        dr = r0 + jnp.arange(wr) - r + (NA_WIN_R - 1)
        bias = rel_bias[:, dr][:, :, dc].transpose(0, 2, 1, 3)
        s_loc = jnp.einsum('bhqd,bhrqcd->bhqrc', qr, kw).astype(jnp.float32) * scale
        s_loc = s_loc + bias[None].astype(jnp.float32)
        s_ctx = jnp.einsum('bhqd,bhld->bhql', qr, k_ctx).astype(jnp.float32) * scale
        s = jnp.concatenate([s_loc.reshape(b, h, GRID_W, n_loc), s_ctx], axis=-1)
        p = jax.nn.softmax(s, axis=-1).astype(v.dtype)
        p_loc = p[..., :n_loc].reshape(b, h, GRID_W, wr, wc)
        return (jnp.einsum('bhqrc,bhrqcd->bhqd', p_loc, vw)
                + jnp.einsum('bhql,bhld->bhqd', p[..., n_loc:], v_ctx))

    o = lax.map(one_row, jnp.arange(rows))
    return o.transpose(1, 2, 0, 3, 4).reshape(b, h, t, hd)


def window_attention(q, k, v, k_ctx, v_ctx, sink):
    b, hb, t, hd = q.shape
    kvh = k.shape[1]
    g = hb // kvh
    nb = t // SWA_BLOCK
    band = 3 * SWA_BLOCK
    scale = hd ** -0.5
    qb = q.reshape(b, kvh, g, nb, SWA_BLOCK, hd)
    pad = ((0, 0), (0, 0), (SWA_BLOCK, SWA_BLOCK), (0, 0))
    kp = jnp.pad(k, pad)
    vp = jnp.pad(v, pad)
    sink_l = sink.astype(jnp.float32).reshape(kvh, g)

    def one_block(i):
        qi = lax.dynamic_index_in_dim(qb, i, axis=3, keepdims=False)
        start = i * SWA_BLOCK
        kb = lax.dynamic_slice_in_dim(kp, start, band, axis=2)
        vb = lax.dynamic_slice_in_dim(vp, start, band, axis=2)
        qpos = start + jnp.arange(SWA_BLOCK)
        kpos = start - SWA_BLOCK + jnp.arange(band)
        valid = ((kpos[None, :] >= 0) & (kpos[None, :] < t)
                 & (jnp.abs(qpos[:, None] - kpos[None, :]) <= SWA_WINDOW))
        s_loc = jnp.einsum('bkgqd,bkld->bkgql', qi, kb).astype(jnp.float32) * scale
        s_loc = jnp.where(valid, s_loc, MASK_VALUE)
        s_ctx = jnp.einsum('bkgqd,bkld->bkgql', qi, k_ctx).astype(jnp.float32) * scale
        s_sink = jnp.broadcast_to(sink_l[None, :, :, None, None], s_loc.shape[:-1] + (1,))
        p = jax.nn.softmax(jnp.concatenate([s_loc, s_ctx, s_sink], axis=-1), axis=-1).astype(v.dtype)
        return (jnp.einsum('bkgql,bkld->bkgqd', p[..., :band], vb)
                + jnp.einsum('bkgql,bkld->bkgqd', p[..., band:-1], v_ctx))

    o = lax.map(one_block, jnp.arange(nb))
    return o.transpose(1, 2, 3, 0, 4, 5).reshape(b, hb, t, hd)


def even_project(h, w_in):
    qa, ka, va, qb, kb, vb = split_cols(h @ w_in, [NA_W, NA_W, NA_W, SWA_Q_W, SWA_KV_W, SWA_KV_W])
    return (to_heads(qa, NA_HEADS, HEAD_DIM), to_heads(ka, NA_HEADS, HEAD_DIM), to_heads(va, NA_HEADS, HEAD_DIM),
            to_heads(qb, SWA_HEADS, HEAD_DIM), to_heads(kb, SWA_KV_HEADS, HEAD_DIM), to_heads(vb, SWA_KV_HEADS, HEAD_DIM))


def even_mixer_context(h, w_in, w_out, sink):
    qa, ka, va, qb, kb, vb = even_project(h, w_in)
    b, L = h.shape[:2]
    g = SWA_HEADS // SWA_KV_HEADS
    oa = ctx_self_attention(qa[:, :, None], ka, va, None)[:, :, 0]
    ob = ctx_self_attention(qb.reshape(b, SWA_KV_HEADS, g, L, HEAD_DIM), kb, vb,
                            sink.reshape(SWA_KV_HEADS, g)).reshape(b, SWA_HEADS, L, HEAD_DIM)
    o = from_heads(jnp.concatenate([oa, ob], axis=1)) @ w_out
    return o, ka, va, kb, vb


def even_mixer_latent(h, ka_c, va_c, kb_c, vb_c, w_in, w_out, rel_bias, sink):
    qa, ka, va, qb, kb, vb = even_project(h, w_in)
    oa = neighbourhood_attention(qa, ka, va, ka_c, va_c, rel_bias)
    ob = window_attention(rope_2d(qb), rope_2d(kb), vb, kb_c, vb_c, sink)
    return from_heads(jnp.concatenate([oa, ob], axis=1)) @ w_out


def conformer_conv(u, conv_w, conv_b, ln_g, ln_b):
    a, gt = jnp.split(u, 2, axis=-1)
    x = a * jax.nn.sigmoid(gt)
    y = lax.conv_general_dilated(x, conv_w[:, None, :], window_strides=(1,),
                                 padding=[(CONV_WIDTH // 2, CONV_WIDTH // 2)],
                                 dimension_numbers=('NWC', 'WIO', 'NWC'),
                                 feature_group_count=CONV_CH) + conv_b
    yf = y.astype(jnp.float32)
    mu = jnp.mean(yf, axis=-1, keepdims=True)
    var = jnp.mean(jnp.square(yf - mu), axis=-1, keepdims=True)
    yn = (yf - mu) * lax.rsqrt(var + EPS) * ln_g.astype(jnp.float32) + ln_b.astype(jnp.float32)
    return jax.nn.silu(yn).astype(u.dtype)


def gla_chunk_scan(q, k, v, log_f, s0):
    b, h, t, dk = q.shape
    dv = v.shape[-1]
    nc = t // HG_CHUNK

    def chunks(u):
        return u.astype(jnp.float32).reshape(b, h, nc, HG_CHUNK, u.shape[-1]).transpose(2, 0, 1, 3, 4)

    lower = jnp.tril(jnp.ones((HG_CHUNK, HG_CHUNK), dtype=bool))[:, :, None]

    def step(S, inp):
        qc, kc, vc, gc = inp
        cum = jnp.cumsum(gc, axis=2)
        rel = cum[:, :, :, None, :] - cum[:, :, None, :, :]
        decay = jnp.where(lower, jnp.exp(jnp.where(lower, rel, 0.0)), 0.0)
        scores = jnp.einsum('bhtd,bhsd,bhtsd->bhts', qc, kc, decay)
        o = (jnp.einsum('bhts,bhse->bhte', scores, vc)
             + jnp.einsum('bhtd,bhde->bhte', qc * jnp.exp(cum), S))
        cum_end = cum[:, :, -1:, :]
        S_new = (jnp.exp(cum_end[:, :, 0, :, None]) * S
                 + jnp.einsum('bhsd,bhse->bhde', kc * jnp.exp(cum_end - cum), vc))
        return S_new, o

    S_fin, o = lax.scan(step, s0.astype(jnp.float32), (chunks(q), chunks(k), chunks(v), chunks(log_f)))
    return o.transpose(1, 2, 0, 3, 4).reshape(b, h, t, dv), S_fin


def hgrn2_gates(f, lb):
    ff = f.astype(jnp.float32)
    forget = lb + (1 - lb) * jax.nn.sigmoid(ff)
    log_f = jnp.log(forget)
    key_in = (1 - lb) * jax.nn.sigmoid(-ff)
    return to_heads(key_in, HG_HEADS, HG_DK), to_heads(log_f, HG_HEADS, HG_DK)


def odd_mixer(h, w_in, w_out, conv_w, conv_b, conv_g, conv_bn, lb, norm_g, s0_f, s0_b):
    u, q, i, f_f, f_b, gt = split_cols(h @ w_in, [2 * CONV_CH, HG_W, HG_VW, HG_W, HG_W, HG_VW])
    oc = conformer_conv(u, conv_w, conv_b, conv_g, conv_bn)
    qh = to_heads(jax.nn.silu(q), HG_HEADS, HG_DK)
    ih = to_heads(i, HG_HEADS, HG_DV)
    kf, gf = hgrn2_gates(f_f, lb[0])
    kb, gb = hgrn2_gates(f_b, lb[1])
    o_f, S_f = gla_chunk_scan(qh, kf, ih, gf, s0_f)
    flip = lambda a: jnp.flip(a, axis=2)
    o_b, S_b = gla_chunk_scan(flip(qh), flip(kb), flip(ih), flip(gb), s0_b)
    od = (o_f + flip(o_b)).astype(h.dtype)
    od = from_heads(rmsnorm(od, norm_g)) * jax.nn.silu(gt)
    o = jnp.concatenate([oc, od], axis=-1) @ w_out
    return o, S_f, S_b


def expert_dispatch(xt, expert, gate, w1, w3, w2):
    n, d = xt.shape
    a = expert.shape[0]
    token = jnp.arange(a, dtype=jnp.int32) // TOP_K
    order = jnp.argsort(expert)
    e_sorted = expert[order]
    counts = jnp.zeros((N_EXPERTS,), jnp.int32).at[expert].add(1)
    starts = jnp.cumsum(counts) - counts
    padded = (counts + MOE_BLOCK - 1) // MOE_BLOCK * MOE_BLOCK
    pad_ends = jnp.cumsum(padded)
    pad_starts = pad_ends - padded
    dest = pad_starts[e_sorted] + (jnp.arange(a, dtype=jnp.int32) - starts[e_sorted])
    n_blocks = -(-a // MOE_BLOCK) + N_EXPERTS
    n_slots = n_blocks * MOE_BLOCK
    slot_tok = jnp.zeros((n_slots,), jnp.int32).at[dest].set(token[order])
    slot_w = jnp.zeros((n_slots,), gate.dtype).at[dest].set(gate[order])
    blk_exp = jnp.minimum(jnp.searchsorted(pad_ends, jnp.arange(n_blocks) * MOE_BLOCK, side='right'),
                          N_EXPERTS - 1)

    def run(args):
        e, toks, w = args
        xb = xt[toks]
        hid = jax.nn.silu(xb @ w1[e]) * (xb @ w3[e])
        return (hid @ w2[e]) * w[:, None].astype(xt.dtype)

    yb = lax.map(run, (blk_exp, slot_tok.reshape(n_blocks, MOE_BLOCK), slot_w.reshape(n_blocks, MOE_BLOCK)))
    return jnp.zeros_like(xt).at[slot_tok].add(yb.reshape(n_slots, d))


def hier_moe(h, wg, bg, we, be, w1, w3, w2):
    b, t, d = h.shape
    xt = h.reshape(b * t, d)
    hf = xt.astype(jnp.float32)
    lg = hf @ wg.astype(jnp.float32) + bg.astype(jnp.float32)
    grp = jnp.argmax(lg, axis=-1)
    p_grp = jnp.take_along_axis(jax.nn.softmax(lg, axis=-1), grp[:, None], axis=1)
    le = (hf @ we.astype(jnp.float32) + be.astype(jnp.float32)).reshape(-1, N_GROUPS, EXPERTS_PER_GROUP)
    le_sel = jnp.take_along_axis(le, grp[:, None, None], axis=1)[:, 0]
    top_v, top_i = lax.top_k(le_sel, TOP_K)
    gate = p_grp * jax.nn.softmax(top_v, axis=-1)
    expert = (grp[:, None] * EXPERTS_PER_GROUP + top_i).astype(jnp.int32)
    y = expert_dispatch(xt, expert.reshape(-1), gate.reshape(-1), w1, w3, w2)
    return y.reshape(b, t, d)


def setup_inputs(seed: int = 0) -> dict:
    key = jax.random.key(seed)
    keys = iter(jax.random.split(key, 40))

    def nrm(shape, scale):
        return jax.random.normal(next(keys), shape, jnp.float32) * scale

    D = D_MODEL
    inp = {}
    inp['x_prompt'] = nrm((BATCH, SEQ, D), 1.0)
    inp['x_sample'] = nrm((DEC_BATCH, DEC_SEQ, D), 1.0)
    inp['cache_k_na'] = nrm((DEC_BATCH, N_EVEN, NA_HEADS, PAST_LEN, HEAD_DIM), 1.0)
    inp['cache_v_na'] = nrm((DEC_BATCH, N_EVEN, NA_HEADS, PAST_LEN, HEAD_DIM), 1.0)
    inp['cache_k_swa'] = nrm((DEC_BATCH, N_EVEN, SWA_KV_HEADS, PAST_LEN, HEAD_DIM), 1.0)
    inp['cache_v_swa'] = nrm((DEC_BATCH, N_EVEN, SWA_KV_HEADS, PAST_LEN, HEAD_DIM), 1.0)
    inp['state_hgrn'] = nrm((DEC_BATCH, N_ODD, 2, HG_HEADS, HG_DK, HG_DV), 0.5)
    inp['c'] = nrm((DEC_BATCH, D), 1.0)
    inp['c_ctx'] = nrm((D,), 1.0)
    inp['w_mod'] = nrm((DEPTH, D, 6 * D), 0.5 * D ** -0.5)
    inp['b_mod'] = nrm((DEPTH, 6 * D), 0.02)
    inp['norm_mix_g'] = 1.0 + nrm((DEPTH, D), 0.02)
    inp['norm_ffn_g'] = 1.0 + nrm((DEPTH, D), 0.02)
    inp['w_in_even'] = nrm((N_EVEN, D, D_IN_EVEN), D ** -0.5)
    inp['w_out_even'] = nrm((N_EVEN, D_MIX, D), D_MIX ** -0.5)
    inp['na_rel_bias'] = nrm((N_EVEN, NA_HEADS, 2 * NA_WIN_R - 1, 2 * NA_WIN_C - 1), 0.5)
    inp['swa_sink'] = nrm((N_EVEN, SWA_HEADS), 0.5)
    inp['w_in_odd'] = nrm((N_ODD, D, D_IN_ODD), D ** -0.5)
    inp['w_out_odd'] = nrm((N_ODD, D_MIX, D), D_MIX ** -0.5)
    inp['conv_w'] = nrm((N_ODD, CONV_WIDTH, CONV_CH), CONV_WIDTH ** -0.5)
    inp['conv_b'] = nrm((N_ODD, CONV_CH), 0.02)
    inp['conv_norm_g'] = 1.0 + nrm((N_ODD, CONV_CH), 0.02)
    inp['conv_norm_b'] = nrm((N_ODD, CONV_CH), 0.02)
    inp['hgrn_lb_raw'] = nrm((N_ODD, 2, HG_W), 0.5)
    inp['hgrn_norm_g'] = 1.0 + nrm((N_ODD, HG_DV), 0.02)
    inp['router_group_w'] = nrm((DEPTH, D, N_GROUPS), D ** -0.5)
    inp['router_group_b'] = nrm((DEPTH, N_GROUPS), 0.01)
    inp['router_expert_w'] = nrm((DEPTH, D, N_EXPERTS), D ** -0.5)
    inp['router_expert_b'] = nrm((DEPTH, N_EXPERTS), 0.01)
    inp['moe_w1'] = nrm((DEPTH, N_EXPERTS, D, D_EXPERT), D ** -0.5)
    inp['moe_w3'] = nrm((DEPTH, N_EXPERTS, D, D_EXPERT), D ** -0.5)
    inp['moe_w2'] = nrm((DEPTH, N_EXPERTS, D_EXPERT, D), D_EXPERT ** -0.5)
    inp['final_norm_g'] = 1.0 + nrm((D,), 0.02)
    return inp


def reference(x_prompt, x_sample, cache_k_na, cache_v_na, cache_k_swa, cache_v_swa, state_hgrn,
              c, c_ctx, w_mod, b_mod, norm_mix_g, norm_ffn_g,
              w_in_even, w_out_even, na_rel_bias, swa_sink,
              w_in_odd, w_out_odd, conv_w, conv_b, conv_norm_g, conv_norm_b, hgrn_lb_raw, hgrn_norm_g,
              router_group_w, router_group_b, router_expert_w, router_expert_b,
              moe_w1, moe_w3, moe_w2, final_norm_g):
    lb_p = jax.nn.softmax(hgrn_lb_raw.astype(jnp.float32), axis=0)
    lower_bounds = jnp.cumsum(lb_p, axis=0) - lb_p[0:1]

    def channel_sublayer(x, l, shift, scale, gate):
        h = modulate(x, norm_ffn_g[l], shift, scale)
        y = hier_moe(h, router_group_w[l], router_group_b[l], router_expert_w[l], router_expert_b[l],
                     moe_w1[l], moe_w3[l], moe_w2[l])
        return x + gate * y

    x = x_prompt
    k_na, v_na, k_swa, v_swa, s_hg = [], [], [], [], []
    for l in range(DEPTH):
        sh1, sc1, g1, sh2, sc2, g2 = ada_params(c_ctx, w_mod[l], b_mod[l])
        h = modulate(x, norm_mix_g[l], sh1, sc1)
        if l % 2 == 0:
            e = l // 2
            o, ka, va, kb, vb = even_mixer_context(h, w_in_even[e], w_out_even[e], swa_sink[e])
            k_na.append(ka)
            v_na.append(va)
            k_swa.append(kb)
            v_swa.append(vb)
        else:
            j = l // 2
            zero_state = jnp.zeros((x.shape[0], HG_HEADS, HG_DK, HG_DV), jnp.float32)
            o, s_f, s_b = odd_mixer(h, w_in_odd[j], w_out_odd[j], conv_w[j], conv_b[j], conv_norm_g[j],
                                    conv_norm_b[j], lower_bounds[j], hgrn_norm_g[j], zero_state, zero_state)
            s_hg.append(jnp.stack([s_f, s_b], axis=1).astype(x.dtype))
        x = x + g1 * o
        x = channel_sublayer(x, l, sh2, sc2, g2)
    y_prompt = rmsnorm(x, final_norm_g)
    new_k_na = jnp.stack(k_na, axis=1)
    new_v_na = jnp.stack(v_na, axis=1)
    new_k_swa = jnp.stack(k_swa, axis=1)
    new_v_swa = jnp.stack(v_swa, axis=1)
    new_state_hgrn = jnp.stack(s_hg, axis=1)

    x = x_sample
    for l in range(DEPTH):
        sh1, sc1, g1, sh2, sc2, g2 = ada_params(c, w_mod[l], b_mod[l])
        h = modulate(x, norm_mix_g[l], sh1, sc1)
        if l % 2 == 0:
            e = l // 2
            o = even_mixer_latent(h, cache_k_na[:, e], cache_v_na[:, e], cache_k_swa[:, e], cache_v_swa[:, e],
                                  w_in_even[e], w_out_even[e], na_rel_bias[e], swa_sink[e])
        else:
            j = l // 2
            o, _, _ = odd_mixer(h, w_in_odd[j], w_out_odd[j], conv_w[j], conv_b[j], conv_norm_g[j],
                                conv_norm_b[j], lower_bounds[j], hgrn_norm_g[j],
                                state_hgrn[:, j, 0], state_hgrn[:, j, 1])
        x = x + g1 * o
        x = channel_sublayer(x, l, sh2, sc2, g2)
    y_sample = rmsnorm(x, final_norm_g)

    return (y_prompt, y_sample, new_k_na, new_v_na, new_k_swa, new_v_swa, new_state_hgrn)
```

```python
import functools

import numpy as np
import jax
import jax.numpy as jnp
from jax import lax
from jax.experimental import pallas as pl
from jax.experimental.pallas import tpu as pltpu

F32 = jnp.float32
BF16 = jnp.bfloat16

D = 1024
BATCH = 16
SEQ = 256
DEPTH = 4
DEC_BATCH = 8
DEC_SEQ = 4096
PAST_LEN = 512
GRID_W = 64
HEAD_DIM = 64
EPS = 1e-6
NA_HEADS = 8
NA_WIN_R = 8
NA_WIN_C = 16
SWA_HEADS = 8
SWA_KV_HEADS = 2
SWA_WINDOW = 128
ROPE_BASE = 10000.0
CONV_CH = 512
CONV_WIDTH = 31
HG_HEADS = 4
HG_DK = 128
HG_CHUNK = 64
N_GROUPS = 4
EXPERTS_PER_GROUP = 8
N_EXPERTS = 32
D_EXPERT = 512
N_EVEN = 2
N_ODD = 2
D_IN_EVEN = 2304
D_IN_ODD = 3584

NT_CTX = BATCH * SEQ
NT_LAT = DEC_BATCH * DEC_SEQ
NT = NT_CTX + NT_LAT
SEG = 4096
assert NT_CTX == SEG and DEC_SEQ == SEG
N_MOD = 1 + DEC_BATCH
MASK = -1e30

MOE_MB = 512
MOE_NB = (2 * NT) // MOE_MB + N_EXPERTS
MOE_SLOTS = MOE_NB * MOE_MB

VMEM_LIMIT = 56 * 1024 * 1024


def _cp(sem, vmem=VMEM_LIMIT):
    return pltpu.CompilerParams(dimension_semantics=sem, vmem_limit_bytes=vmem)


def _dot(a, b):
    return jnp.dot(a, b, preferred_element_type=F32)


def _dot_nt(a, b):
    return lax.dot_general(a, b, (((1,), (1,)), ((), ())), preferred_element_type=F32)


def _dot_tn(a, b):
    return lax.dot_general(a, b, (((0,), (0,)), ((), ())), preferred_element_type=F32)


def _split2(a):
    hi = a.astype(BF16)
    lo = (a - hi.astype(F32)).astype(BF16)
    return hi, lo


def _dot3(a, b_hi, b_lo):
    a_hi, a_lo = _split2(a)
    return _dot(a_hi, b_hi) + (_dot(a_hi, b_lo) + _dot(a_lo, b_hi))


def _sigmoid(x):
    return 1.0 / (1.0 + jnp.exp(-x))


def _modulate(x, g, shift, scale):
    ms = jnp.mean(x * x, axis=-1, keepdims=True)
    return (x * lax.rsqrt(ms + EPS) * g) * (1.0 + scale) + shift


def _mod_kernel(c_ref, whi_ref, wlo_ref, b_ref, o_ref):
    cv = c_ref[...]
    s = cv * _sigmoid(cv)
    o_ref[...] = _dot3(s, whi_ref[...], wlo_ref[...]) + b_ref[...]


def _mod_table(cvec, w_mod, b_mod):
    tn = 1536
    w_hi = w_mod.astype(BF16)
    w_lo = (w_mod - w_hi.astype(F32)).astype(BF16)
    out = pl.pallas_call(
        _mod_kernel,
        grid=(DEPTH, 6 * D // tn),
        in_specs=[
            pl.BlockSpec((16, D), lambda l, j: (0, 0)),
            pl.BlockSpec((None, D, tn), lambda l, j: (l, 0, j)),
            pl.BlockSpec((None, D, tn), lambda l, j: (l, 0, j)),
            pl.BlockSpec((None, 1, tn), lambda l, j: (l, 0, j)),
        ],
        out_specs=pl.BlockSpec((None, 16, tn), lambda l, j: (l, 0, j)),
        out_shape=jax.ShapeDtypeStruct((DEPTH, 16, 6 * D), F32),
        compiler_params=_cp(("arbitrary", "arbitrary")),
        name="mod_table",
    )(cvec, w_hi, w_lo, b_mod.reshape(DEPTH, 1, 6 * D))
    return out.reshape(DEPTH, 16, 1, 6 * D)


def _mod_spec(l, which, tm):
    return pl.BlockSpec((None, None, 1, D), lambda i, *_: (l, (i * tm) // SEG, 0, which))


def _modmm_kernel(x_ref, g_ref, sh_ref, sc_ref, w_ref, o_ref, *, nchunk):
    h = _modulate(x_ref[...], g_ref[...], sh_ref[...], sc_ref[...]).astype(BF16)
    n = o_ref.shape[1]
    for n0 in range(0, n, nchunk):
        o_ref[:, n0:n0 + nchunk] = _dot(h, w_ref[:, n0:n0 + nchunk])


def _modmm(x, g, mod, l, w_bf16, tm):
    n = w_bf16.shape[1]
    return pl.pallas_call(
        functools.partial(_modmm_kernel, nchunk=256),
        grid=(NT // tm,),
        in_specs=[
            pl.BlockSpec((tm, D), lambda i: (i, 0)),
            pl.BlockSpec((1, D), lambda i: (0, 0)),
            _mod_spec(l, 0, tm),
            _mod_spec(l, 1, tm),
            pl.BlockSpec((D, n), lambda i: (0, 0)),
        ],
        out_specs=pl.BlockSpec((tm, n), lambda i: (i, 0)),
        out_shape=jax.ShapeDtypeStruct((NT, n), F32),
        compiler_params=_cp(("arbitrary",)),
        name="modmm",
    )(x, g.reshape(1, D), mod, mod, w_bf16)


def _mmres_kernel(o_ref, x_ref, gate_ref, w_ref, out_ref):
    y = _dot(o_ref[...].astype(BF16), w_ref[...])
    out_ref[...] = x_ref[...] + gate_ref[...] * y


def _mmres(o, x, mod, l, w_bf16, tm=512):
    return pl.pallas_call(
        _mmres_kernel,
        grid=(NT // tm,),
        in_specs=[
            pl.BlockSpec((tm, D), lambda i: (i, 0)),
            pl.BlockSpec((tm, D), lambda i: (i, 0)),
            _mod_spec(l, 2, tm),
            pl.BlockSpec((D, D), lambda i: (0, 0)),
        ],
        out_specs=pl.BlockSpec((tm, D), lambda i: (i, 0)),
        out_shape=jax.ShapeDtypeStruct((NT, D), F32),
        compiler_params=_cp(("arbitrary",)),
        name="mmres",
    )(o, x, mod, w_bf16)


def _final_kernel(x_ref, g_ref, o_ref):
    x = x_ref[...]
    ms = jnp.mean(x * x, axis=-1, keepdims=True)
    o_ref[...] = x * lax.rsqrt(ms + EPS) * g_ref[...]


def _final_norm(x, g, tm=512):
    return pl.pallas_call(
        _final_kernel,
        grid=(NT // tm,),
        in_specs=[pl.BlockSpec((tm, D), lambda i: (i, 0)), pl.BlockSpec((1, D), lambda i: (0, 0))],
        out_specs=pl.BlockSpec((tm, D), lambda i: (i, 0)),
        out_shape=jax.ShapeDtypeStruct((NT, D), F32),
        compiler_params=_cp(("arbitrary",)),
        name="final_norm",
    )(x, g.reshape(1, D))


_QA, _KA, _VA, _QB, _KB, _VB = 0, 512, 1024, 1536, 2048, 2176


def _ctx_attn_kernel(sink_ref, p_ref, o_ref):
    scale = HEAD_DIM ** -0.5

    def head(qc, kc, vc, sink):
        q = p_ref[:, qc:qc + 64].astype(BF16)
        k = p_ref[:, kc:kc + 64].astype(BF16)
        v = p_ref[:, vc:vc + 64].astype(BF16)
        s = _dot_nt(q, k) * scale
        m = jnp.max(s, axis=-1, keepdims=True)
        if sink is not None:
            m = jnp.maximum(m, sink)
        p = jnp.exp(s - m)
        den = jnp.sum(p, axis=-1, keepdims=True)
        if sink is not None:
            den = den + jnp.exp(sink - m)
        return _dot(p.astype(BF16), v) / den

    for h in range(NA_HEADS):
        o_ref[:, 64 * h:64 * h + 64] = head(_QA + 64 * h, _KA + 64 * h, _VA + 64 * h, None)
    for j in range(SWA_HEADS):
        kv = j // (SWA_HEADS // SWA_KV_HEADS)
        o_ref[:, 512 + 64 * j:512 + 64 * j + 64] = head(_QB + 64 * j, _KB + 64 * kv, _VB + 64 * kv, sink_ref[j])


def _ctx_attn(p, sink):
    return pl.pallas_call(
        _ctx_attn_kernel,
        grid=(BATCH,),
        in_specs=[
            pl.BlockSpec(memory_space=pltpu.SMEM),
            pl.BlockSpec((SEQ, D_IN_EVEN), lambda b: (b, 0)),
        ],
        out_specs=pl.BlockSpec((SEQ, D), lambda b: (b, 0)),
        out_shape=jax.ShapeDtypeStruct((NT, D), F32),
        compiler_params=_cp(("arbitrary",)),
        name="ctx_attn",
    )(sink, p)


NA_QROWS = 8
NA_BAND = 16
NA_TQ = NA_QROWS * GRID_W
NA_TK = NA_BAND * GRID_W
NA_RB = (DEC_SEQ // GRID_W) // NA_QROWS


def _na_band_start(rb):
    rows = DEC_SEQ // GRID_W
    return int(np.clip(NA_QROWS * rb - NA_WIN_R // 2, 0, rows - NA_BAND))


def _na_bias_index():
    rows = DEC_SEQ // GRID_W
    out = []
    for rb in (0, 1, NA_RB - 1):
        qi = np.arange(NA_TQ)
        kj = np.arange(NA_TK)
        qrow = NA_QROWS * rb + qi // GRID_W
        qcol = qi % GRID_W
        krow = _na_band_start(rb) + kj // GRID_W
        kcol = kj % GRID_W
        r0 = np.clip(qrow - NA_WIN_R // 2, 0, rows - NA_WIN_R)
        c0 = np.clip(qcol - NA_WIN_C // 2, 0, GRID_W - NA_WIN_C)
        vr = (krow[None, :] >= r0[:, None]) & (krow[None, :] < r0[:, None] + NA_WIN_R)
        vc = (kcol[None, :] >= c0[:, None]) & (kcol[None, :] < c0[:, None] + NA_WIN_C)
        dr = krow[None, :] - qrow[:, None] + (NA_WIN_R - 1)
        dc = kcol[None, :] - qcol[:, None] + (NA_WIN_C - 1)
        flat = dr * (2 * NA_WIN_C - 1) + dc
        n_tab = (2 * NA_WIN_R - 1) * (2 * NA_WIN_C - 1)
        out.append(np.where(vr & vc, flat, n_tab))
    return np.stack(out).astype(np.int32)


def _na_bias_table(rel_bias):
    h = rel_bias.shape[0]
    flat = jnp.concatenate([rel_bias.reshape(h, -1), jnp.full((h, 1), MASK, F32)], axis=1)
    tab = jnp.take(flat, jnp.asarray(_na_bias_index()), axis=1)
    return tab.reshape(h // 2, 2, 3, NA_TQ, NA_TK)


def _na_kernel(q_ref, k_ref, v_ref, kc_ref, vc_ref, bias_ref, oin_ref, o_ref):
    del oin_ref
    rb = pl.program_id(2)
    scale = HEAD_DIM ** -0.5
    start = jnp.clip(NA_QROWS * rb - NA_WIN_R // 2, 0, DEC_SEQ // GRID_W - NA_BAND) * GRID_W
    start = pl.multiple_of(start, 256)
    case = jnp.where(rb == 0, 0, jnp.where(rb == NA_RB - 1, 2, 1))
    kb = k_ref[pl.ds(start, NA_TK), :].astype(BF16)
    vb = v_ref[pl.ds(start, NA_TK), :].astype(BF16)
    q = q_ref[...]
    lane = lax.broadcasted_iota(jnp.int32, (1, 128), 1)
    o_loc, o_ctx = [], []
    for hh in range(2):
        qm = jnp.where((lane >> 6) == hh, q, 0.0).astype(BF16)
        s_loc = _dot_nt(qm, kb) * scale + bias_ref[hh, case]
        qh = q[:, 64 * hh:64 * hh + 64].astype(BF16)
        s_ctx = _dot_nt(qh, kc_ref[hh].astype(BF16)) * scale
        m = jnp.maximum(jnp.max(s_loc, axis=-1, keepdims=True), jnp.max(s_ctx, axis=-1, keepdims=True))
        p_loc = jnp.exp(s_loc - m)
        p_ctx = jnp.exp(s_ctx - m)
        inv = 1.0 / (jnp.sum(p_loc, axis=-1, keepdims=True) + jnp.sum(p_ctx, axis=-1, keepdims=True))
        o_loc.append(_dot(p_loc.astype(BF16), vb) * inv)
        o_ctx.append(_dot(p_ctx.astype(BF16), vc_ref[hh].astype(BF16)) * inv)
    o_ref[...] = jnp.where((lane >> 6) == 0, o_loc[0], o_loc[1]) + jnp.concatenate(o_ctx, axis=-1)


def _na_attn(p, cache_k, cache_v, bias_tab, e, o_full):
    n_hp = NA_HEADS // 2
    lat0 = NT_CTX // NA_TQ
    per_b = DEC_SEQ // NA_TQ
    return pl.pallas_call(
        _na_kernel,
        grid=(n_hp, DEC_BATCH, NA_RB),
        in_specs=[
            pl.BlockSpec((NA_TQ, 128), lambda hp, b, rb: (lat0 + b * per_b + rb, _QA // 128 + hp)),
            pl.BlockSpec((DEC_SEQ, 128), lambda hp, b, rb: (1 + b, _KA // 128 + hp)),
            pl.BlockSpec((DEC_SEQ, 128), lambda hp, b, rb: (1 + b, _VA // 128 + hp)),
            pl.BlockSpec((None, None, 2, PAST_LEN, HEAD_DIM), lambda hp, b, rb: (b, e, hp, 0, 0)),
            pl.BlockSpec((None, None, 2, PAST_LEN, HEAD_DIM), lambda hp, b, rb: (b, e, hp, 0, 0)),
            pl.BlockSpec((None, 2, 3, NA_TQ, NA_TK), lambda hp, b, rb: (hp, 0, 0, 0, 0)),
            pl.BlockSpec(memory_space=pl.ANY),
        ],
        out_specs=pl.BlockSpec((NA_TQ, 128), lambda hp, b, rb: (lat0 + b * per_b + rb, hp)),
        out_shape=jax.ShapeDtypeStruct((NT, D), F32),
        input_output_aliases={6: 0},
        compiler_params=_cp(("arbitrary", "arbitrary", "arbitrary")),
        name="na_attn",
    )(p, p, p, cache_k, cache_v, bias_tab, o_full)


SWA_TQ = 256
SWA_TK = SWA_TQ + 2 * SWA_WINDOW


def _rope_tables():
    pos = np.arange(DEC_SEQ)
    row = (pos // GRID_W).astype(np.float32)
    col = (pos % GRID_W).astype(np.float32)
    half = HEAD_DIM // 2
    inv = jnp.asarray(ROPE_BASE, F32) ** (-jnp.arange(0, half, 2, dtype=F32) / half)
    ar = jnp.asarray(row)[:, None] * inv[None, :]
    ac = jnp.asarray(col)[:, None] * inv[None, :]
    cos = jnp.concatenate([jnp.cos(ar), jnp.cos(ar), jnp.cos(ac), jnp.cos(ac)], axis=1)
    sin = jnp.concatenate([-jnp.sin(ar), jnp.sin(ar), -jnp.sin(ac), jnp.sin(ac)], axis=1)
    return jnp.tile(cos, (1, 2)), jnp.tile(sin, (1, 2))


def _rope128(t, cos, sin):
    lane = lax.broadcasted_iota(jnp.int32, (1, 128), 1)
    up = pltpu.roll(t, 112, 1)
    dn = pltpu.roll(t, 16, 1)
    sw = jnp.where((lane & 31) < 16, up, dn)
    return t * cos + sw * sin


def _swa_kernel(sink_ref, q_ref, k_ref, v_ref, kc_ref, vc_ref, cos_ref, sin_ref, oin_ref, o_ref, kr_ref):
    del oin_ref
    qb = pl.program_id(1)
    scale = HEAD_DIM ** -0.5
    g = SWA_HEADS // SWA_KV_HEADS

    @pl.when(qb == 0)
    def _():
        kr_ref[...] = _rope128(k_ref[...], cos_ref[...], sin_ref[...]).astype(BF16)

    q0 = pl.multiple_of(qb * SWA_TQ, SWA_TQ)
    bs = pl.multiple_of(jnp.clip(q0 - SWA_WINDOW, 0, DEC_SEQ - SWA_TK), 128)
    cq = cos_ref[pl.ds(q0, SWA_TQ), :]
    sq = sin_ref[pl.ds(q0, SWA_TQ), :]
    kb = kr_ref[pl.ds(bs, SWA_TK), :]
    vb = v_ref[pl.ds(bs, SWA_TK), :].astype(BF16)
    qr = [_rope128(q_ref[:, 128 * j:128 * j + 128], cq, sq) for j in range(SWA_HEADS // 2)]

    rowi = lax.broadcasted_iota(jnp.int32, (g * SWA_TQ, SWA_TK), 0)
    coli = lax.broadcasted_iota(jnp.int32, (g * SWA_TQ, SWA_TK), 1)
    dpos = (q0 + (rowi & (SWA_TQ - 1))) - (bs + coli)
    valid = (dpos <= SWA_WINDOW) & (dpos >= -SWA_WINDOW)
    rowc = lax.broadcasted_iota(jnp.int32, (g * SWA_TQ, 1), 0) >> 8

    for kvh in range(SWA_KV_HEADS):
        heads = []
        for gi in range(g):
            hq = kvh * g + gi
            heads.append(qr[hq // 2][:, 64 * (hq % 2):64 * (hq % 2) + 64])
        q4 = jnp.concatenate(heads, axis=0).astype(BF16)
        sink = jnp.zeros((g * SWA_TQ, 1), F32)
        for gi in range(g):
            sink = jnp.where(rowc == gi, sink_ref[kvh * g + gi], sink)
        kh = kb[:, 64 * kvh:64 * kvh + 64]
        vh = vb[:, 64 * kvh:64 * kvh + 64]
        s_loc = jnp.where(valid, _dot_nt(q4, kh) * scale, MASK)
        s_ctx = _dot_nt(q4, kc_ref[kvh].astype(BF16)) * scale
        m = jnp.maximum(jnp.maximum(jnp.max(s_loc, axis=-1, keepdims=True),
                                    jnp.max(s_ctx, axis=-1, keepdims=True)), sink)
        p_loc = jnp.exp(s_loc - m)
        p_ctx = jnp.exp(s_ctx - m)
        den = (jnp.sum(p_loc, axis=-1, keepdims=True) + jnp.sum(p_ctx, axis=-1, keepdims=True)
               + jnp.exp(sink - m))
        o = (_dot(p_loc.astype(BF16), vh) + _dot(p_ctx.astype(BF16), vc_ref[kvh].astype(BF16))) / den
        for gi in range(g):
            hq = kvh * g + gi
            o_ref[:, 64 * hq:64 * hq + 64] = o[SWA_TQ * gi:SWA_TQ * (gi + 1), :]


def _swa_attn(p, cache_k, cache_v, sink, cos, sin, e, o_full):
    assert SWA_TQ == 256
    lat0 = NT_CTX // SWA_TQ
    per_b = DEC_SEQ // SWA_TQ
    return pl.pallas_call(
        _swa_kernel,
        grid=(DEC_BATCH, per_b),
        in_specs=[
            pl.BlockSpec(memory_space=pltpu.SMEM),
            pl.BlockSpec((SWA_TQ, 512), lambda b, qb: (lat0 + b * per_b + qb, _QB // 512)),
            pl.BlockSpec((DEC_SEQ, 128), lambda b, qb: (1 + b, _KB // 128)),
            pl.BlockSpec((DEC_SEQ, 128), lambda b, qb: (1 + b, _VB // 128)),
            pl.BlockSpec((None, None, 2, PAST_LEN, HEAD_DIM), lambda b, qb: (b, e, 0, 0, 0)),
            pl.BlockSpec((None, None, 2, PAST_LEN, HEAD_DIM), lambda b, qb: (b, e, 0, 0, 0)),
            pl.BlockSpec((DEC_SEQ, 128), lambda b, qb: (0, 0)),
            pl.BlockSpec((DEC_SEQ, 128), lambda b, qb: (0, 0)),
            pl.BlockSpec(memory_space=pl.ANY),
        ],
        out_specs=pl.BlockSpec((SWA_TQ, 512), lambda b, qb: (lat0 + b * per_b + qb, 1)),
        out_shape=jax.ShapeDtypeStruct((NT, D), F32),
        scratch_shapes=[pltpu.VMEM((DEC_SEQ, 128), BF16)],
        input_output_aliases={8: 0},
        compiler_params=_cp(("arbitrary", "arbitrary")),
        name="swa_attn",
    )(sink, p, p, p, cache_k, cache_v, cos, sin, o_full)


CONV_TT = 256
CONV_HALO = 16
CONV_RC = 32


def _conv_kernel(u_ref, ul_ref, ur_ref, w_ref, cb_ref, lg_ref, lb_ref, o_ref, xs_ref):
    i = pl.program_id(0)
    n_ctx_tiles = NT_CTX // CONV_TT
    per_seq = DEC_SEQ // CONV_TT
    j = i - n_ctx_tiles
    is_ctx = i < n_ctx_tiles
    is_start = is_ctx | ((j % per_seq) == 0)
    is_end = is_ctx | ((j % per_seq) == per_seq - 1)

    def glu(u):
        return u[:, :CONV_CH] * _sigmoid(u[:, CONV_CH:])

    xs_ref[CONV_HALO:CONV_HALO + CONV_TT, :] = glu(u_ref[...])
    xs_ref[0:CONV_HALO, :] = jnp.where(is_start, 0.0, glu(ul_ref[...]))
    xs_ref[CONV_HALO + CONV_TT:, :] = jnp.where(is_end, 0.0, glu(ur_ref[...]))

    pad = CONV_WIDTH // 2
    for c in range(CONV_TT // CONV_RC):
        base = CONV_HALO + c * CONV_RC - pad
        acc = jnp.zeros((CONV_RC, CONV_CH), F32)
        for t in range(CONV_WIDTH):
            acc = acc + xs_ref[base + t:base + t + CONV_RC, :] * w_ref[t:t + 1, :]
        y = acc + cb_ref[...]
        mu = jnp.mean(y, axis=-1, keepdims=True)
        yc = y - mu
        var = jnp.mean(yc * yc, axis=-1, keepdims=True)
        yn = yc * lax.rsqrt(var + EPS) * lg_ref[...] + lb_ref[...]
        o_ref[c * CONV_RC:(c + 1) * CONV_RC, :] = yn * _sigmoid(yn)


def _conv_module(p, conv_w, conv_b, ln_g, ln_b):
    nh = CONV_TT // CONV_HALO
    last = NT // CONV_HALO - 1
    return pl.pallas_call(
        _conv_kernel,
        grid=(NT // CONV_TT,),
        in_specs=[
            pl.BlockSpec((CONV_TT, 2 * CONV_CH), lambda i: (i, 0)),
            pl.BlockSpec((CONV_HALO, 2 * CONV_CH), lambda i: (jnp.maximum(i * nh - 1, 0), 0)),
            pl.BlockSpec((CONV_HALO, 2 * CONV_CH), lambda i: (jnp.minimum((i + 1) * nh, last), 0)),
            pl.BlockSpec((CONV_WIDTH, CONV_CH), lambda i: (0, 0)),
            pl.BlockSpec((1, CONV_CH), lambda i: (0, 0)),
            pl.BlockSpec((1, CONV_CH), lambda i: (0, 0)),
            pl.BlockSpec((1, CONV_CH), lambda i: (0, 0)),
        ],
        out_specs=pl.BlockSpec((CONV_TT, CONV_CH), lambda i: (i, 0)),
        out_shape=jax.ShapeDtypeStruct((NT, D), F32),
        scratch_shapes=[pltpu.VMEM((CONV_TT + 2 * CONV_HALO, CONV_CH), F32)],
        compiler_params=_cp(("arbitrary",)),
        name="conv_module",
    )(p, p, p, conv_w, conv_b.reshape(1, -1), ln_g.reshape(1, -1), ln_b.reshape(1, -1))


_U, _HQ, _HI, _HFF, _HFB, _HGT = 0, 1024, 1536, 2048, 2560, 3072
HG_LEVELS = 6
HG_ROWS = (HG_LEVELS + 2) * HG_CHUNK


def _hgrn_constants():
    c = HG_CHUNK
    dm = np.zeros((HG_ROWS, c), np.float32)
    mm = np.zeros((HG_LEVELS + 1, c, c), np.float32)
    for lv in range(HG_LEVELS):
        n = c >> lv
        half = n // 2
        for t in range(c):
            r = (t // n) * n + half - 1
            if t % n >= half:
                dm[lv * c + t, r + 1:t + 1] = 1.0
            else:
                dm[lv * c + t, t + 1:r + 1] = 1.0
            for s in range(c):
                if t // n == s // n and t % n >= half and s % n < half:
                    mm[lv, t, s] = 1.0
    for t in range(c):
        dm[HG_LEVELS * c + t, :t + 1] = 1.0
        dm[(HG_LEVELS + 1) * c + t, t + 1:] = 1.0
    mm[HG_LEVELS] = np.eye(c, dtype=np.float32)
    dm_b = dm.reshape(HG_LEVELS + 2, c, c)[:, ::-1, ::-1].reshape(HG_ROWS, c)
    mm_b = mm[:, ::-1, ::-1]
    return np.stack([dm, dm_b]), np.stack([mm, mm_b])


def _hgrn_chunk(qs, f, v, lb, dmat, masks, s_t, total_row):
    c = HG_CHUNK
    sg = _sigmoid(f)
    g = jnp.log(lb + (1.0 - lb) * sg)
    k = (1.0 - lb) * _sigmoid(-f)
    g_hi = g.astype(BF16)
    r1 = g - g_hi.astype(F32)
    g_mid = r1.astype(BF16)
    g_lo = (r1 - g_mid.astype(F32)).astype(BF16)
    e = jnp.exp(_dot(dmat, g_hi) + (_dot(dmat, g_mid) + _dot(dmat, g_lo)))
    a = masks[HG_LEVELS] * _dot_nt(qs.astype(BF16), k.astype(BF16))
    for lv in range(HG_LEVELS):
        el = e[lv * c:(lv + 1) * c]
        a = a + masks[lv] * _dot_nt((qs * el).astype(BF16), (k * el).astype(BF16))
    cum = e[HG_LEVELS * c:(HG_LEVELS + 1) * c]
    o = _dot(a.astype(BF16), v.astype(BF16)) + _dot_nt((qs * cum).astype(BF16), s_t.astype(BF16))
    kend = (k * e[(HG_LEVELS + 1) * c:]).astype(BF16)
    dec = e[total_row:total_row + 1, :]
    s_new = s_t * dec + _dot_tn(v.astype(BF16), kend)
    return o, s_new


def _hgrn_kernel(q_ref, i_ref, ff_ref, fb_ref, gt_ref, lb_ref, ng_ref, s0_ref, d_ref, m_ref, oin_ref,
                 o_ref, sfin_ref, ob_ref, st_ref, *, t_len):
    del oin_ref
    c = HG_CHUNK
    nc = t_len // c
    st_ref[...] = s0_ref[...]
    lbf = lb_ref[0:1, :]
    lbb = lb_ref[1:2, :]
    tot_f = HG_LEVELS * c + c - 1
    tot_b = HG_LEVELS * c

    def body(ci, carry):
        rf = pl.multiple_of(ci * c, c)
        rbw = pl.multiple_of((nc - 1 - ci) * c, c)
        qf = q_ref[pl.ds(rf, c), :]
        o_f, s_f = _hgrn_chunk(qf * _sigmoid(qf), ff_ref[pl.ds(rf, c), :], i_ref[pl.ds(rf, c), :], lbf,
                               d_ref[0], [m_ref[0, x] for x in range(HG_LEVELS + 1)], st_ref[0], tot_f)
        qb = q_ref[pl.ds(rbw, c), :]
        o_b, s_b = _hgrn_chunk(qb * _sigmoid(qb), fb_ref[pl.ds(rbw, c), :], i_ref[pl.ds(rbw, c), :], lbb,
                               d_ref[1], [m_ref[1, x] for x in range(HG_LEVELS + 1)], st_ref[1], tot_b)
        o_ref[pl.ds(rf, c), :] = o_f
        ob_ref[pl.ds(rbw, c), :] = o_b
        st_ref[0] = s_f
        st_ref[1] = s_b
        return carry

    lax.fori_loop(0, nc, body, 0)
    sfin_ref[...] = st_ref[...]

    rows = 256

    def fin(ri, carry):
        r = pl.multiple_of(ri * rows, rows)
        od = o_ref[pl.ds(r, rows), :] + ob_ref[pl.ds(r, rows), :]
        ms = jnp.mean(od * od, axis=-1, keepdims=True)
        gt = gt_ref[pl.ds(r, rows), :]
        o_ref[pl.ds(r, rows), :] = od * lax.rsqrt(ms + EPS) * ng_ref[...] * (gt * _sigmoid(gt))
        return carry

    lax.fori_loop(0, t_len // rows, fin, 0)


def _hgrn(p, lb, norm_g, s0_t, o_full, n_batch, t_len, row0):
    dm, mm = _hgrn_constants()
    dm = jnp.asarray(dm, BF16)
    mm = jnp.asarray(mm, F32)
    b0 = row0 // t_len

    def col(c0):
        return lambda b, h: (b0 + b, c0 // 128 + h)

    blk = lambda c0: pl.BlockSpec((t_len, 128), col(c0))
    return pl.pallas_call(
        functools.partial(_hgrn_kernel, t_len=t_len),
        grid=(n_batch, HG_HEADS),
        in_specs=[
            blk(_HQ), blk(_HI), blk(_HFF), blk(_HFB), blk(_HGT),
            pl.BlockSpec((2, 128), lambda b, h: (0, h)),
            pl.BlockSpec((1, 128), lambda b, h: (0, 0)),
            pl.BlockSpec((None, 2, None, 128, 128), lambda b, h: (b, 0, h, 0, 0)),
            pl.BlockSpec((2, HG_ROWS, HG_CHUNK), lambda b, h: (0, 0, 0)),
            pl.BlockSpec((2, HG_LEVELS + 1, HG_CHUNK, HG_CHUNK), lambda b, h: (0, 0, 0, 0)),
            pl.BlockSpec(memory_space=pl.ANY),
        ],
        out_specs=[
            pl.BlockSpec((t_len, 128), col(CONV_CH)),
            pl.BlockSpec((None, 2, None, 128, 128), lambda b, h: (b, 0, h, 0, 0)),
        ],
        out_shape=[jax.ShapeDtypeStruct((NT, D), F32),
                   jax.ShapeDtypeStruct((n_batch, 2, HG_HEADS, 128, 128), F32)],
        scratch_shapes=[pltpu.VMEM((t_len, 128), F32), pltpu.VMEM((2, 128, 128), F32)],
        input_output_aliases={10: 0},
        compiler_params=_cp(("arbitrary", "arbitrary")),
        name="hgrn",
    )(p, p, p, p, p, lb, norm_g.reshape(1, 128), s0_t, dm, mm, o_full)


ROUTE_TM = 512


def _router_kernel(x_ref, g_ref, sh_ref, sc_ref, whi_ref, wlo_ref, br_ref, tri_ref,
                   h_ref, mi_ref, mf_ref, cnt_ref, carry_ref):
    i = pl.program_id(0)

    @pl.when(i == 0)
    def _():
        carry_ref[...] = jnp.zeros_like(carry_ref)

    h = _modulate(x_ref[...], g_ref[...], sh_ref[...], sc_ref[...])
    h_ref[...] = h
    logits = _dot3(h, whi_ref[...], wlo_ref[...]) + br_ref[...]
    lane = lax.broadcasted_iota(jnp.int32, logits.shape, 1)
    neg = jnp.float32(-3e38)
    is_g = lane < N_GROUPS
    lg = jnp.where(is_g, logits, neg)
    mg = jnp.max(lg, axis=-1, keepdims=True)
    grp = jnp.min(jnp.where(lg == mg, lane, 128), axis=-1, keepdims=True)
    pg = 1.0 / jnp.sum(jnp.where(is_g, jnp.exp(lg - mg), 0.0), axis=-1, keepdims=True)
    ex = lane - N_GROUPS
    in_grp = (ex >= 0) & (ex < N_EXPERTS) & ((ex >> 3) == grp)
    le = jnp.where(in_grp, logits, neg)
    v1 = jnp.max(le, axis=-1, keepdims=True)
    i1 = jnp.min(jnp.where(le == v1, lane, 128), axis=-1, keepdims=True)
    le2 = jnp.where(lane == i1, neg, le)
    v2 = jnp.max(le2, axis=-1, keepdims=True)
    i2 = jnp.min(jnp.where(le2 == v2, lane, 128), axis=-1, keepdims=True)
    t = jnp.exp(v2 - v1)
    g1 = pg / (1.0 + t)
    g2 = pg * t / (1.0 + t)
    oh = jnp.where((lane == i1) | (lane == i2), 1.0, 0.0)
    prefix = _dot(tri_ref[...], oh.astype(BF16)) + carry_ref[...]
    r1 = jnp.sum(jnp.where(lane == i1, prefix, 0.0), axis=-1, keepdims=True).astype(jnp.int32)
    r2 = jnp.sum(jnp.where(lane == i2, prefix, 0.0), axis=-1, keepdims=True).astype(jnp.int32)
    carry_ref[...] = carry_ref[...] + jnp.sum(oh, axis=0, keepdims=True)
    cnt_ref[...] = carry_ref[...]
    mi_ref[...] = jnp.where(lane == 0, i1 - N_GROUPS, jnp.where(lane == 1, i2 - N_GROUPS,
                            jnp.where(lane == 2, r1, jnp.where(lane == 3, r2, 0))))
    mf_ref[...] = jnp.where(lane == 0, g1, jnp.where(lane == 1, g2, 0.0))


def _router(x, g, mod, l, wg, bg, we, be):
    tm = ROUTE_TM
    wr = jnp.zeros((D, 128), F32).at[:, :N_GROUPS].set(wg).at[:, N_GROUPS:N_GROUPS + N_EXPERTS].set(we)
    br = jnp.zeros((1, 128), F32).at[0, :N_GROUPS].set(bg).at[0, N_GROUPS:N_GROUPS + N_EXPERTS].set(be)
    w_hi = wr.astype(BF16)
    w_lo = (wr - w_hi.astype(F32)).astype(BF16)
    tri = jnp.asarray(np.tril(np.ones((tm, tm), np.float32), -1), BF16)
    const = lambda i: (0, 0)
    return pl.pallas_call(
        _router_kernel,
        grid=(NT // tm,),
        in_specs=[
            pl.BlockSpec((tm, D), lambda i: (i, 0)),
            pl.BlockSpec((1, D), const),
            _mod_spec(l, 3, tm),
            _mod_spec(l, 4, tm),
            pl.BlockSpec((D, 128), const),
            pl.BlockSpec((D, 128), const),
            pl.BlockSpec((1, 128), const),
            pl.BlockSpec((tm, tm), const),
        ],
        out_specs=[
            pl.BlockSpec((tm, D), lambda i: (i, 0)),
            pl.BlockSpec((tm, 128), lambda i: (i, 0)),
            pl.BlockSpec((tm, 128), lambda i: (i, 0)),
            pl.BlockSpec((1, 128), const),
        ],
        out_shape=[
            jax.ShapeDtypeStruct((NT, D), F32),
            jax.ShapeDtypeStruct((NT, 128), jnp.int32),
            jax.ShapeDtypeStruct((NT, 128), F32),
            jax.ShapeDtypeStruct((1, 128), F32),
        ],
        scratch_shapes=[pltpu.VMEM((1, 128), F32)],
        compiler_params=_cp(("arbitrary",)),
        name="router",
    )(x, g.reshape(1, D), mod, mod, w_hi, w_lo, br, tri)


DISP_TM = 512


def _row_copy(src, s_row, dst, d_row, sem):
    return pltpu.make_async_copy(src.at[pl.ds(s_row, 1)], dst.at[pl.ds(d_row, 1)], sem)


def _dispatch_kernel(dest_ref, h_hbm, xs_hbm, sem):
    base = pl.program_id(0) * DISP_TM

    def issue(t, carry):
        for k in range(2):
            _row_copy(h_hbm, base + t, xs_hbm, dest_ref[k, t], sem).start()
        return carry

    lax.fori_loop(0, DISP_TM, issue, 0)

    def drain(t, carry):
        for k in range(2):
            _row_copy(h_hbm, base + t, xs_hbm, dest_ref[k, t], sem).wait()
        return carry

    lax.fori_loop(0, DISP_TM, drain, 0)


def _dispatch(dest_t, h):
    return pl.pallas_call(
        _dispatch_kernel,
        grid=(NT // DISP_TM,),
        in_specs=[
            pl.BlockSpec((2, DISP_TM), lambda i: (0, i), memory_space=pltpu.SMEM),
            pl.BlockSpec(memory_space=pl.ANY),
        ],
        out_specs=pl.BlockSpec(memory_space=pl.ANY),
        out_shape=jax.ShapeDtypeStruct((MOE_SLOTS, D), F32),
        scratch_shapes=[pltpu.SemaphoreType.DMA(())],
        compiler_params=_cp(("arbitrary",)),
        name="moe_dispatch",
    )(dest_t, h)


def _expert_kernel(bexp_ref, bval_ref, xs_ref, w1_ref, w3_ref, w2_ref, y_ref):
    del bexp_ref
    nv = bval_ref[pl.program_id(0)]

    @pl.when(nv > 0)
    def _():
        row = lax.broadcasted_iota(jnp.int32, (MOE_MB, 1), 0)
        xb = jnp.where(row < nv, xs_ref[...], 0.0).astype(BF16)
        a = _dot(xb, w1_ref[...])
        b = _dot(xb, w3_ref[...])
        hid = (a * _sigmoid(a) * b).astype(BF16)
        y_ref[...] = _dot(hid, w2_ref[...])

    @pl.when(nv <= 0)
    def _():
        y_ref[...] = jnp.zeros_like(y_ref)


def _experts(blk_exp, blk_valid, xs, w1, w3, w2):
    grid_spec = pltpu.PrefetchScalarGridSpec(
        num_scalar_prefetch=2,
        grid=(MOE_NB,),
        in_specs=[
            pl.BlockSpec((MOE_MB, D), lambda i, be, bv: (i, 0)),
            pl.BlockSpec((None, D, D_EXPERT), lambda i, be, bv: (be[i], 0, 0)),
            pl.BlockSpec((None, D, D_EXPERT), lambda i, be, bv: (be[i], 0, 0)),
            pl.BlockSpec((None, D_EXPERT, D), lambda i, be, bv: (be[i], 0, 0)),
        ],
        out_specs=pl.BlockSpec((MOE_MB, D), lambda i, be, bv: (i, 0)),
    )
    return pl.pallas_call(
        _expert_kernel,
        grid_spec=grid_spec,
        out_shape=jax.ShapeDtypeStruct((MOE_SLOTS, D), F32),
        compiler_params=_cp(("arbitrary",)),
        name="moe_experts",
    )(blk_exp, blk_valid, xs, w1, w3, w2)


COMB_TM = 256


def _combine_kernel(dest_ref, mf_ref, x_ref, gate_ref, y_hbm, o_ref, buf_ref, sem):
    def issue(t, carry):
        for k in range(2):
            _row_copy(y_hbm, dest_ref[k, t], buf_ref.at[k], t, sem).start()
        return carry

    lax.fori_loop(0, COMB_TM, issue, 0)

    def drain(t, carry):
        for k in range(2):
            _row_copy(y_hbm, dest_ref[k, t], buf_ref.at[k], t, sem).wait()
        return carry

    lax.fori_loop(0, COMB_TM, drain, 0)
    mf = mf_ref[...]
    y = mf[:, 0:1] * buf_ref[0] + mf[:, 1:2] * buf_ref[1]
    o_ref[...] = x_ref[...] + gate_ref[...] * y


def _combine(dest_t, mf, x, mod, l, y):
    tm = COMB_TM
    return pl.pallas_call(
        _combine_kernel,
        grid=(NT // tm,),
        in_specs=[
            pl.BlockSpec((2, tm), lambda i: (0, i), memory_space=pltpu.SMEM),
            pl.BlockSpec((tm, 128), lambda i: (i, 0)),
            pl.BlockSpec((tm, D), lambda i: (i, 0)),
            _mod_spec(l, 5, tm),
            pl.BlockSpec(memory_space=pl.ANY),
        ],
        out_specs=pl.BlockSpec((tm, D), lambda i: (i, 0)),
        out_shape=jax.ShapeDtypeStruct((NT, D), F32),
        scratch_shapes=[pltpu.VMEM((2, tm, D), F32), pltpu.SemaphoreType.DMA(())],
        compiler_params=_cp(("arbitrary",)),
        name="moe_combine",
    )(dest_t, mf, x, mod, y)


def _moe_plan(mi, counts):
    cnt = counts[0, N_GROUPS:N_GROUPS + N_EXPERTS].astype(jnp.int32)
    padded = (cnt + MOE_MB - 1) // MOE_MB * MOE_MB
    pad_ends = jnp.cumsum(padded)
    pad_starts = pad_ends - padded
    dest = pad_starts[mi[:, 0:2]] + mi[:, 2:4]
    blk0 = jnp.arange(MOE_NB, dtype=jnp.int32) * MOE_MB
    blk_exp = jnp.minimum(jnp.searchsorted(pad_ends, blk0, side="right"), N_EXPERTS - 1).astype(jnp.int32)
    blk_valid = jnp.clip(cnt[blk_exp] - (blk0 - pad_starts[blk_exp]), 0, MOE_MB)
    blk_valid = jnp.where(blk0 < pad_ends[-1], blk_valid, 0).astype(jnp.int32)
    return dest.T.astype(jnp.int32), blk_exp, blk_valid


def _moe_layer(x, g, mod, l, wg, bg, we, be, w1, w3, w2):
    h, mi, mf, counts = _router(x, g, mod, l, wg, bg, we, be)
    dest_t, blk_exp, blk_valid = _moe_plan(mi, counts)
    xs = _dispatch(dest_t, h)
    y = _experts(blk_exp, blk_valid, xs, w1, w3, w2)
    return _combine(dest_t, mf, x, mod, l, y)


def kernel(x_prompt, x_sample, cache_k_na, cache_v_na, cache_k_swa, cache_v_swa, state_hgrn, c, c_ctx, w_mod, b_mod, norm_mix_g, norm_ffn_g, w_in_even, w_out_even, na_rel_bias, swa_sink, w_in_odd, w_out_odd, conv_w, conv_b, conv_norm_g, conv_norm_b, hgrn_lb_raw, hgrn_norm_g, router_group_w, router_group_b, router_expert_w, router_expert_b, moe_w1, moe_w3, moe_w2, final_norm_g):
    x = jnp.concatenate([x_prompt.reshape(NT_CTX, D), x_sample.reshape(NT_LAT, D)], axis=0)
    cvec = jnp.zeros((16, D), F32).at[0].set(c_ctx).at[1:1 + DEC_BATCH].set(c)
    mod = _mod_table(cvec, w_mod, b_mod)

    lb_p = jax.nn.softmax(hgrn_lb_raw.astype(F32), axis=0)
    lower_bounds = jnp.cumsum(lb_p, axis=0) - lb_p[0:1]
    cos, sin = _rope_tables()

    k_na, v_na, k_swa, v_swa, s_hg = [], [], [], [], []
    for l in range(DEPTH):
        if l % 2 == 0:
            e = l // 2
            p = _modmm(x, norm_mix_g[l], mod, l, w_in_even[e].astype(BF16), tm=512)
            o = _ctx_attn(p, swa_sink[e])
            o = _na_attn(p, cache_k_na, cache_v_na, _na_bias_table(na_rel_bias[e]), e, o)
            o = _swa_attn(p, cache_k_swa, cache_v_swa, swa_sink[e], cos, sin, e, o)
            pc = p[:NT_CTX]
            to_heads = lambda u, nh: u.reshape(BATCH, SEQ, nh, HEAD_DIM).transpose(0, 2, 1, 3)
            k_na.append(to_heads(pc[:, _KA:_KA + 512], NA_HEADS))
            v_na.append(to_heads(pc[:, _VA:_VA + 512], NA_HEADS))
            k_swa.append(to_heads(pc[:, _KB:_KB + 128], SWA_KV_HEADS))
            v_swa.append(to_heads(pc[:, _VB:_VB + 128], SWA_KV_HEADS))
            w_out = w_out_even[e]
        else:
            j = l // 2
            p = _modmm(x, norm_mix_g[l], mod, l, w_in_odd[j].astype(BF16), tm=256)
            o = _conv_module(p, conv_w[j], conv_b[j], conv_norm_g[j], conv_norm_b[j])
            zero_state = jnp.zeros((BATCH, 2, HG_HEADS, 128, 128), F32)
            o, s_ctx = _hgrn(p, lower_bounds[j], hgrn_norm_g[j], zero_state, o, BATCH, SEQ, 0)
            s0_lat = jnp.swapaxes(state_hgrn[:, j], -1, -2)
            o, _ = _hgrn(p, lower_bounds[j], hgrn_norm_g[j], s0_lat, o, DEC_BATCH, DEC_SEQ, NT_CTX)
            s_hg.append(jnp.swapaxes(s_ctx, -1, -2))
            w_out = w_out_odd[j]
        x = _mmres(o, x, mod, l, w_out.astype(BF16))
        x = _moe_layer(x, norm_ffn_g[l], mod, l, router_group_w[l], router_group_b[l],
                       router_expert_w[l], router_expert_b[l],
                       moe_w1[l].astype(BF16), moe_w3[l].astype(BF16), moe_w2[l].astype(BF16))

    y = _final_norm(x, final_norm_g)
    y_prompt = y[:NT_CTX].reshape(BATCH, SEQ, D)
    y_sample = y[NT_CTX:].reshape(DEC_BATCH, DEC_SEQ, D)
    return (y_prompt, y_sample, jnp.stack(k_na, axis=1), jnp.stack(v_na, axis=1),
            jnp.stack(k_swa, axis=1), jnp.stack(v_swa, axis=1), jnp.stack(s_hg, axis=1))
```

```python
import functools

import numpy as np
import jax
import jax.numpy as jnp
from jax import lax
from jax.experimental import pallas as pl
from jax.experimental.pallas import tpu as pltpu

F32 = jnp.float32
BF16 = jnp.bfloat16

D = 1024
BATCH = 16
SEQ = 256
DEPTH = 4
DEC_BATCH = 8
DEC_SEQ = 4096
PAST_LEN = 512
GRID_W = 64
HEAD_DIM = 64
EPS = 1e-6
NA_HEADS = 8
NA_WIN_R = 8
NA_WIN_C = 16
SWA_HEADS = 8
SWA_KV_HEADS = 2
SWA_WINDOW = 128
ROPE_BASE = 10000.0
CONV_CH = 512
CONV_WIDTH = 31
HG_HEADS = 4
HG_DK = 128
HG_CHUNK = 64
N_GROUPS = 4
EXPERTS_PER_GROUP = 8
N_EXPERTS = 32
D_EXPERT = 512
N_EVEN = 2
N_ODD = 2
D_IN_EVEN = 2304
D_IN_ODD = 3584

NT_CTX = BATCH * SEQ
NT_LAT = DEC_BATCH * DEC_SEQ
NT = NT_CTX + NT_LAT
SEG = 4096
assert NT_CTX == SEG and DEC_SEQ == SEG
N_MOD = 1 + DEC_BATCH
MASK = -1e30

MOE_MB = 512
MOE_NB = (2 * NT) // MOE_MB + N_EXPERTS
MOE_SLOTS = MOE_NB * MOE_MB

VMEM_LIMIT = 56 * 1024 * 1024


def _cp(sem, vmem=VMEM_LIMIT):
    return pltpu.CompilerParams(dimension_semantics=sem, vmem_limit_bytes=vmem)


def _dot(a, b):
    return jnp.dot(a, b, preferred_element_type=F32)


def _dot_nt(a, b):
    return lax.dot_general(a, b, (((1,), (1,)), ((), ())), preferred_element_type=F32)


def _dot_tn(a, b):
    return lax.dot_general(a, b, (((0,), (0,)), ((), ())), preferred_element_type=F32)


def _split2(a):
    hi = a.astype(BF16)
    lo = (a - hi.astype(F32)).astype(BF16)
    return hi, lo


def _dot3(a, b_hi, b_lo):
    a_hi, a_lo = _split2(a)
    return _dot(a_hi, b_hi) + (_dot(a_hi, b_lo) + _dot(a_lo, b_hi))


def _sigmoid(x):
    return 1.0 / (1.0 + jnp.exp(-x))


def _modulate(x, g, shift, scale):
    ms = jnp.mean(x * x, axis=-1, keepdims=True)
    return (x * lax.rsqrt(ms + EPS) * g) * (1.0 + scale) + shift


def _mod_kernel(c_ref, whi_ref, wlo_ref, b_ref, o_ref):
    cv = c_ref[...]
    s = cv * _sigmoid(cv)
    o_ref[...] = _dot3(s, whi_ref[...], wlo_ref[...]) + b_ref[...]


def _mod_table(cvec, w_mod, b_mod):
    tn = 1536
    w_hi = w_mod.astype(BF16)
    w_lo = (w_mod - w_hi.astype(F32)).astype(BF16)
    out = pl.pallas_call(
        _mod_kernel,
        grid=(DEPTH, 6 * D // tn),
        in_specs=[
            pl.BlockSpec((16, D), lambda l, j: (0, 0)),
            pl.BlockSpec((None, D, tn), lambda l, j: (l, 0, j)),
            pl.BlockSpec((None, D, tn), lambda l, j: (l, 0, j)),
            pl.BlockSpec((None, 1, tn), lambda l, j: (l, 0, j)),
        ],
        out_specs=pl.BlockSpec((None, 16, tn), lambda l, j: (l, 0, j)),
        out_shape=jax.ShapeDtypeStruct((DEPTH, 16, 6 * D), F32),
        compiler_params=_cp(("arbitrary", "arbitrary")),
        name="mod_table",
    )(cvec, w_hi, w_lo, b_mod.reshape(DEPTH, 1, 6 * D))
    return out.reshape(DEPTH, 16, 1, 6 * D)


def _mod_spec(l, which, tm):
    return pl.BlockSpec((None, None, 1, D), lambda i, *_: (l, (i * tm) // SEG, 0, which))


def _modmm_kernel(x_ref, g_ref, sh_ref, sc_ref, w_ref, o_ref, *, nchunk):
    h = _modulate(x_ref[...], g_ref[...], sh_ref[...], sc_ref[...]).astype(BF16)
    n = o_ref.shape[1]
    for n0 in range(0, n, nchunk):
        o_ref[:, n0:n0 + nchunk] = _dot(h, w_ref[:, n0:n0 + nchunk])


def _modmm(x, g, mod, l, w_bf16, tm):
    n = w_bf16.shape[1]
    return pl.pallas_call(
        functools.partial(_modmm_kernel, nchunk=256),
        grid=(NT // tm,),
        in_specs=[
            pl.BlockSpec((tm, D), lambda i: (i, 0)),
            pl.BlockSpec((1, D), lambda i: (0, 0)),
            _mod_spec(l, 0, tm),
            _mod_spec(l, 1, tm),
            pl.BlockSpec((D, n), lambda i: (0, 0)),
        ],
        out_specs=pl.BlockSpec((tm, n), lambda i: (i, 0)),
        out_shape=jax.ShapeDtypeStruct((NT, n), F32),
        compiler_params=_cp(("arbitrary",)),
        name="modmm",
    )(x, g.reshape(1, D), mod, mod, w_bf16)


def _mmres_kernel(o_ref, x_ref, gate_ref, w_ref, out_ref):
    y = _dot(o_ref[...].astype(BF16), w_ref[...])
    out_ref[...] = x_ref[...] + gate_ref[...] * y


def _mmres(o, x, mod, l, w_bf16, tm=512):
    return pl.pallas_call(
        _mmres_kernel,
        grid=(NT // tm,),
        in_specs=[
            pl.BlockSpec((tm, D), lambda i: (i, 0)),
            pl.BlockSpec((tm, D), lambda i: (i, 0)),
            _mod_spec(l, 2, tm),
            pl.BlockSpec((D, D), lambda i: (0, 0)),
        ],
        out_specs=pl.BlockSpec((tm, D), lambda i: (i, 0)),
        out_shape=jax.ShapeDtypeStruct((NT, D), F32),
        compiler_params=_cp(("arbitrary",)),
        name="mmres",
    )(o, x, mod, w_bf16)


def _final_kernel(x_ref, g_ref, o_ref):
    x = x_ref[...]
    ms = jnp.mean(x * x, axis=-1, keepdims=True)
    o_ref[...] = x * lax.rsqrt(ms + EPS) * g_ref[...]


def _final_norm(x, g, tm=512):
    return pl.pallas_call(
        _final_kernel,
        grid=(NT // tm,),
        in_specs=[pl.BlockSpec((tm, D), lambda i: (i, 0)), pl.BlockSpec((1, D), lambda i: (0, 0))],
        out_specs=pl.BlockSpec((tm, D), lambda i: (i, 0)),
        out_shape=jax.ShapeDtypeStruct((NT, D), F32),
        compiler_params=_cp(("arbitrary",)),
        name="final_norm",
    )(x, g.reshape(1, D))


_QA, _KA, _VA, _QB, _KB, _VB = 0, 512, 1024, 1536, 2048, 2176


def _ctx_attn_kernel(sink_ref, p_ref, o_ref):
    scale = HEAD_DIM ** -0.5

    def head(qc, kc, vc, sink):
        q = p_ref[:, qc:qc + 64].astype(BF16)
        k = p_ref[:, kc:kc + 64].astype(BF16)
        v = p_ref[:, vc:vc + 64].astype(BF16)
        s = _dot_nt(q, k) * scale
        m = jnp.max(s, axis=-1, keepdims=True)
        if sink is not None:
            m = jnp.maximum(m, sink)
        p = jnp.exp(s - m)
        den = jnp.sum(p, axis=-1, keepdims=True)
        if sink is not None:
            den = den + jnp.exp(sink - m)
        return _dot(p.astype(BF16), v) / den

    for h in range(NA_HEADS):
        o_ref[:, 64 * h:64 * h + 64] = head(_QA + 64 * h, _KA + 64 * h, _VA + 64 * h, None)
    for j in range(SWA_HEADS):
        kv = j // (SWA_HEADS // SWA_KV_HEADS)
        o_ref[:, 512 + 64 * j:512 + 64 * j + 64] = head(_QB + 64 * j, _KB + 64 * kv, _VB + 64 * kv, sink_ref[j])


def _ctx_attn(p, sink):
    return pl.pallas_call(
        _ctx_attn_kernel,
        grid=(BATCH,),
        in_specs=[
            pl.BlockSpec(memory_space=pltpu.SMEM),
            pl.BlockSpec((SEQ, D_IN_EVEN), lambda b: (b, 0)),
        ],
        out_specs=pl.BlockSpec((SEQ, D), lambda b: (b, 0)),
        out_shape=jax.ShapeDtypeStruct((NT, D), F32),
        compiler_params=_cp(("arbitrary",)),
        name="ctx_attn",
    )(sink, p)


NA_QROWS = 8
NA_BAND = 16
NA_TQ = NA_QROWS * GRID_W
NA_TK = NA_BAND * GRID_W
NA_RB = (DEC_SEQ // GRID_W) // NA_QROWS


def _na_band_start(rb):
    rows = DEC_SEQ // GRID_W
    return int(np.clip(NA_QROWS * rb - NA_WIN_R // 2, 0, rows - NA_BAND))


def _na_row_index():
    rows = DEC_SEQ // GRID_W
    n_dr = 2 * NA_WIN_R - 1
    out = np.full((3, NA_QROWS, NA_BAND), n_dr, np.int32)
    for ci, rb in enumerate((0, 1, NA_RB - 1)):
        for ql in range(NA_QROWS):
            qrow = NA_QROWS * rb + ql
            r0 = int(np.clip(qrow - NA_WIN_R // 2, 0, rows - NA_WIN_R))
            for kl in range(NA_BAND):
                krow = _na_band_start(rb) + kl
                if r0 <= krow < r0 + NA_WIN_R:
                    out[ci, ql, kl] = krow - qrow + (NA_WIN_R - 1)
    return out


def _na_bias_table(rel_bias):
    h = rel_bias.shape[0]
    n_dr, n_dc = 2 * NA_WIN_R - 1, 2 * NA_WIN_C - 1
    cols = np.arange(GRID_W)
    c0 = np.clip(cols - NA_WIN_C // 2, 0, GRID_W - NA_WIN_C)
    vc = (cols[None, :] >= c0[:, None]) & (cols[None, :] < c0[:, None] + NA_WIN_C)
    dc = cols[None, :] - cols[:, None] + (NA_WIN_C - 1)
    sel = ((np.arange(n_dc)[:, None, None] == dc[None]) & vc[None]).astype(np.float32)
    bc = jnp.dot(rel_bias.reshape(h * n_dr, n_dc), jnp.asarray(sel.reshape(n_dc, -1)),
                 precision=lax.Precision.HIGHEST).reshape(h, n_dr, GRID_W, GRID_W)
    bc = jnp.where(jnp.asarray(vc), bc, MASK)
    bc = jnp.concatenate([bc, jnp.full((h, 1, GRID_W, GRID_W), MASK, F32)], axis=1)
    tab = jnp.take(bc, jnp.asarray(_na_row_index()), axis=1)
    return tab.transpose(0, 1, 2, 4, 3, 5).reshape(h // 2, 2, 3, NA_TQ, NA_TK)


def _na_kernel(q_ref, k_ref, v_ref, kc_ref, vc_ref, bias_ref, oin_ref, o_ref):
    del oin_ref
    rb = pl.program_id(2)
    scale = HEAD_DIM ** -0.5
    start = jnp.clip(NA_QROWS * rb - NA_WIN_R // 2, 0, DEC_SEQ // GRID_W - NA_BAND) * GRID_W
    start = pl.multiple_of(start, 256)
    case = jnp.where(rb == 0, 0, jnp.where(rb == NA_RB - 1, 2, 1))
    kb = k_ref[pl.ds(start, NA_TK), :].astype(BF16)
    vb = v_ref[pl.ds(start, NA_TK), :].astype(BF16)
    q = q_ref[...]
    lane = lax.broadcasted_iota(jnp.int32, (1, 128), 1)
    o_loc, o_ctx = [], []
    for hh in range(2):
        qm = jnp.where((lane >> 6) == hh, q, 0.0).astype(BF16)
        s_loc = _dot_nt(qm, kb) * scale + bias_ref[hh, case]
        qh = q[:, 64 * hh:64 * hh + 64].astype(BF16)
        s_ctx = _dot_nt(qh, kc_ref[hh].astype(BF16)) * scale
        m = jnp.maximum(jnp.max(s_loc, axis=-1, keepdims=True), jnp.max(s_ctx, axis=-1, keepdims=True))
        p_loc = jnp.exp(s_loc - m)
        p_ctx = jnp.exp(s_ctx - m)
        inv = 1.0 / (jnp.sum(p_loc, axis=-1, keepdims=True) + jnp.sum(p_ctx, axis=-1, keepdims=True))
        o_loc.append(_dot(p_loc.astype(BF16), vb) * inv)
        o_ctx.append(_dot(p_ctx.astype(BF16), vc_ref[hh].astype(BF16)) * inv)
    o_ref[...] = jnp.where((lane >> 6) == 0, o_loc[0], o_loc[1]) + jnp.concatenate(o_ctx, axis=-1)


def _na_attn(p, cache_k, cache_v, bias_tab, e, o_full):
    n_hp = NA_HEADS // 2
    lat0 = NT_CTX // NA_TQ
    per_b = DEC_SEQ // NA_TQ
    return pl.pallas_call(
        _na_kernel,
        grid=(n_hp, DEC_BATCH, NA_RB),
        in_specs=[
            pl.BlockSpec((NA_TQ, 128), lambda hp, b, rb: (lat0 + b * per_b + rb, _QA // 128 + hp)),
            pl.BlockSpec((DEC_SEQ, 128), lambda hp, b, rb: (1 + b, _KA // 128 + hp)),
            pl.BlockSpec((DEC_SEQ, 128), lambda hp, b, rb: (1 + b, _VA // 128 + hp)),
            pl.BlockSpec((None, None, 2, PAST_LEN, HEAD_DIM), lambda hp, b, rb: (b, e, hp, 0, 0)),
            pl.BlockSpec((None, None, 2, PAST_LEN, HEAD_DIM), lambda hp, b, rb: (b, e, hp, 0, 0)),
            pl.BlockSpec((None, 2, 3, NA_TQ, NA_TK), lambda hp, b, rb: (hp, 0, 0, 0, 0)),
            pl.BlockSpec(memory_space=pl.ANY),
        ],
        out_specs=pl.BlockSpec((NA_TQ, 128), lambda hp, b, rb: (lat0 + b * per_b + rb, hp)),
        out_shape=jax.ShapeDtypeStruct((NT, D), F32),
        input_output_aliases={6: 0},
        compiler_params=_cp(("arbitrary", "arbitrary", "arbitrary")),
        name="na_attn",
    )(p, p, p, cache_k, cache_v, bias_tab, o_full)


SWA_TQ = 256
SWA_TK = SWA_TQ + 2 * SWA_WINDOW


def _rope_tables():
    pos = np.arange(DEC_SEQ)
    row = (pos // GRID_W).astype(np.float32)
    col = (pos % GRID_W).astype(np.float32)
    half = HEAD_DIM // 2
    inv = jnp.asarray(ROPE_BASE, F32) ** (-jnp.arange(0, half, 2, dtype=F32) / half)
    ar = jnp.asarray(row)[:, None] * inv[None, :]
    ac = jnp.asarray(col)[:, None] * inv[None, :]
    cos = jnp.concatenate([jnp.cos(ar), jnp.cos(ar), jnp.cos(ac), jnp.cos(ac)], axis=1)
    sin = jnp.concatenate([-jnp.sin(ar), jnp.sin(ar), -jnp.sin(ac), jnp.sin(ac)], axis=1)
    return jnp.tile(cos, (1, 2)), jnp.tile(sin, (1, 2))


def _rope128(t, cos, sin):
    lane = lax.broadcasted_iota(jnp.int32, (1, 128), 1)
    up = pltpu.roll(t, 112, 1)
    dn = pltpu.roll(t, 16, 1)
    sw = jnp.where((lane & 31) < 16, up, dn)
    return t * cos + sw * sin


def _swa_kernel(sink_ref, q_ref, k_ref, v_ref, kc_ref, vc_ref, cos_ref, sin_ref, oin_ref, o_ref, kr_ref):
    del oin_ref
    qb = pl.program_id(1)
    scale = HEAD_DIM ** -0.5
    g = SWA_HEADS // SWA_KV_HEADS

    @pl.when(qb == 0)
    def _():
        kr_ref[...] = _rope128(k_ref[...], cos_ref[...], sin_ref[...]).astype(BF16)

    q0 = pl.multiple_of(qb * SWA_TQ, SWA_TQ)
    bs = pl.multiple_of(jnp.clip(q0 - SWA_WINDOW, 0, DEC_SEQ - SWA_TK), 128)
    cq = cos_ref[pl.ds(q0, SWA_TQ), :]
    sq = sin_ref[pl.ds(q0, SWA_TQ), :]
    kb = kr_ref[pl.ds(bs, SWA_TK), :]
    vb = v_ref[pl.ds(bs, SWA_TK), :].astype(BF16)
    qr = [_rope128(q_ref[:, 128 * j:128 * j + 128], cq, sq) for j in range(SWA_HEADS // 2)]

    rowi = lax.broadcasted_iota(jnp.int32, (g * SWA_TQ, SWA_TK), 0)
    coli = lax.broadcasted_iota(jnp.int32, (g * SWA_TQ, SWA_TK), 1)
    dpos = (q0 + (rowi & (SWA_TQ - 1))) - (bs + coli)
    valid = (dpos <= SWA_WINDOW) & (dpos >= -SWA_WINDOW)
    rowc = lax.broadcasted_iota(jnp.int32, (g * SWA_TQ, 1), 0) >> 8

    for kvh in range(SWA_KV_HEADS):
        heads = []
        for gi in range(g):
            hq = kvh * g + gi
            heads.append(qr[hq // 2][:, 64 * (hq % 2):64 * (hq % 2) + 64])
        q4 = jnp.concatenate(heads, axis=0).astype(BF16)
        sink = jnp.zeros((g * SWA_TQ, 1), F32)
        for gi in range(g):
            sink = jnp.where(rowc == gi, sink_ref[kvh * g + gi], sink)
        kh = kb[:, 64 * kvh:64 * kvh + 64]
        vh = vb[:, 64 * kvh:64 * kvh + 64]
        s_loc = jnp.where(valid, _dot_nt(q4, kh) * scale, MASK)
        s_ctx = _dot_nt(q4, kc_ref[kvh].astype(BF16)) * scale
        m = jnp.maximum(jnp.maximum(jnp.max(s_loc, axis=-1, keepdims=True),
                                    jnp.max(s_ctx, axis=-1, keepdims=True)), sink)
        p_loc = jnp.exp(s_loc - m)
        p_ctx = jnp.exp(s_ctx - m)
        den = (jnp.sum(p_loc, axis=-1, keepdims=True) + jnp.sum(p_ctx, axis=-1, keepdims=True)
               + jnp.exp(sink - m))
        o = (_dot(p_loc.astype(BF16), vh) + _dot(p_ctx.astype(BF16), vc_ref[kvh].astype(BF16))) / den
        for gi in range(g):
            hq = kvh * g + gi
            o_ref[:, 64 * hq:64 * hq + 64] = o[SWA_TQ * gi:SWA_TQ * (gi + 1), :]


def _swa_attn(p, cache_k, cache_v, sink, cos, sin, e, o_full):
    assert SWA_TQ == 256
    lat0 = NT_CTX // SWA_TQ
    per_b = DEC_SEQ // SWA_TQ
    return pl.pallas_call(
        _swa_kernel,
        grid=(DEC_BATCH, per_b),
        in_specs=[
            pl.BlockSpec(memory_space=pltpu.SMEM),
            pl.BlockSpec((SWA_TQ, 512), lambda b, qb: (lat0 + b * per_b + qb, _QB // 512)),
            pl.BlockSpec((DEC_SEQ, 128), lambda b, qb: (1 + b, _KB // 128)),
            pl.BlockSpec((DEC_SEQ, 128), lambda b, qb: (1 + b, _VB // 128)),
            pl.BlockSpec((None, None, 2, PAST_LEN, HEAD_DIM), lambda b, qb: (b, e, 0, 0, 0)),
            pl.BlockSpec((None, None, 2, PAST_LEN, HEAD_DIM), lambda b, qb: (b, e, 0, 0, 0)),
            pl.BlockSpec((DEC_SEQ, 128), lambda b, qb: (0, 0)),
            pl.BlockSpec((DEC_SEQ, 128), lambda b, qb: (0, 0)),
            pl.BlockSpec(memory_space=pl.ANY),
        ],
        out_specs=pl.BlockSpec((SWA_TQ, 512), lambda b, qb: (lat0 + b * per_b + qb, 1)),
        out_shape=jax.ShapeDtypeStruct((NT, D), F32),
        scratch_shapes=[pltpu.VMEM((DEC_SEQ, 128), BF16)],
        input_output_aliases={8: 0},
        compiler_params=_cp(("arbitrary", "arbitrary")),
        name="swa_attn",
    )(sink, p, p, p, cache_k, cache_v, cos, sin, o_full)


CONV_TT = 256
CONV_HALO = 16
CONV_RC = 32


def _conv_kernel(u_ref, ul_ref, ur_ref, w_ref, cb_ref, lg_ref, lb_ref, o_ref, xs_ref):
    i = pl.program_id(0)
    n_ctx_tiles = NT_CTX // CONV_TT
    per_seq = DEC_SEQ // CONV_TT
    j = i - n_ctx_tiles
    is_ctx = i < n_ctx_tiles
    is_start = is_ctx | ((j % per_seq) == 0)
    is_end = is_ctx | ((j % per_seq) == per_seq - 1)

    def glu(u):
        return u[:, :CONV_CH] * _sigmoid(u[:, CONV_CH:])

    xs_ref[CONV_HALO:CONV_HALO + CONV_TT, :] = glu(u_ref[...])
    xs_ref[0:CONV_HALO, :] = jnp.where(is_start, 0.0, glu(ul_ref[...]))
    xs_ref[CONV_HALO + CONV_TT:, :] = jnp.where(is_end, 0.0, glu(ur_ref[...]))

    pad = CONV_WIDTH // 2
    for c in range(CONV_TT // CONV_RC):
        base = CONV_HALO + c * CONV_RC - pad
        acc = jnp.zeros((CONV_RC, CONV_CH), F32)
        for t in range(CONV_WIDTH):
            acc = acc + xs_ref[base + t:base + t + CONV_RC, :] * w_ref[t:t + 1, :]
        y = acc + cb_ref[...]
        mu = jnp.mean(y, axis=-1, keepdims=True)
        yc = y - mu
        var = jnp.mean(yc * yc, axis=-1, keepdims=True)
        yn = yc * lax.rsqrt(var + EPS) * lg_ref[...] + lb_ref[...]
        o_ref[c * CONV_RC:(c + 1) * CONV_RC, :] = yn * _sigmoid(yn)


def _conv_module(p, conv_w, conv_b, ln_g, ln_b):
    nh = CONV_TT // CONV_HALO
    last = NT // CONV_HALO - 1
    return pl.pallas_call(
        _conv_kernel,
        grid=(NT // CONV_TT,),
        in_specs=[
            pl.BlockSpec((CONV_TT, 2 * CONV_CH), lambda i: (i, 0)),
            pl.BlockSpec((CONV_HALO, 2 * CONV_CH), lambda i: (jnp.maximum(i * nh - 1, 0), 0)),
            pl.BlockSpec((CONV_HALO, 2 * CONV_CH), lambda i: (jnp.minimum((i + 1) * nh, last), 0)),
            pl.BlockSpec((CONV_WIDTH, CONV_CH), lambda i: (0, 0)),
            pl.BlockSpec((1, CONV_CH), lambda i: (0, 0)),
            pl.BlockSpec((1, CONV_CH), lambda i: (0, 0)),
            pl.BlockSpec((1, CONV_CH), lambda i: (0, 0)),
        ],
        out_specs=pl.BlockSpec((CONV_TT, CONV_CH), lambda i: (i, 0)),
        out_shape=jax.ShapeDtypeStruct((NT, D), F32),
        scratch_shapes=[pltpu.VMEM((CONV_TT + 2 * CONV_HALO, CONV_CH), F32)],
        compiler_params=_cp(("arbitrary",)),
        name="conv_module",
    )(p, p, p, conv_w, conv_b.reshape(1, -1), ln_g.reshape(1, -1), ln_b.reshape(1, -1))


_U, _HQ, _HI, _HFF, _HFB, _HGT = 0, 1024, 1536, 2048, 2560, 3072
HG_LEVELS = 6
HG_W = HG_HEADS * HG_DK


def _hgrn_constants():
    c = HG_CHUNK
    mm = np.zeros((HG_LEVELS + 1, c, c), np.float32)
    for lv in range(HG_LEVELS):
        n = c >> lv
        half = n // 2
        for t in range(c):
            for s in range(c):
                if t // n == s // n and t % n >= half and s % n < half:
                    mm[lv, t, s] = 1.0
    mm[HG_LEVELS] = np.eye(c, dtype=np.float32)
    tri = np.tril(np.ones((c, c), np.float32))
    return np.stack([tri, tri.T]), np.stack([mm, mm[:, ::-1, ::-1]])


def _hgrn_level_refs(cum, cum_ref, fwd):
    c = HG_CHUNK
    cum_ref[...] = cum
    refs = []
    for lv in range(HG_LEVELS - 2):
        n = c >> lv
        rows = [b * n + n // 2 - (1 if fwd else 0) for b in range(c // n)]
        refs.append(jnp.concatenate([jnp.broadcast_to(cum_ref[r:r + 1, :], (n, HG_W)) for r in rows], axis=0))
    row = lax.broadcasted_iota(jnp.int32, (c, HG_W), 0)
    for n in (4, 2):
        p = n // 2 - (1 if fwd else 0)
        res = row & (n - 1)
        ref = None
        for j in range(n):
            shift = (j - p) % c
            cand = cum if shift == 0 else pltpu.roll(cum, shift, 0)
            ref = cand if ref is None else jnp.where(res == j, cand, ref)
        refs.append(ref)
    return refs


def _hgrn_chunk(qs, f, v, lb, tri, masks, st_ref, cum_ref, fwd):
    c = HG_CHUNK
    g = jnp.log(lb + (1.0 - lb) * _sigmoid(f))
    k = (1.0 - lb) * _sigmoid(-f)
    g_hi = g.astype(BF16)
    r1 = g - g_hi.astype(F32)
    g_mid = r1.astype(BF16)
    g_lo = (r1 - g_mid.astype(F32)).astype(BF16)
    cum = _dot(tri, g_hi) + (_dot(tri, g_mid) + _dot(tri, g_lo))
    refs = _hgrn_level_refs(cum, cum_ref, fwd)
    end_row = c - 1 if fwd else 0
    ecum = jnp.exp(cum)
    eend = jnp.exp(-jnp.abs(cum - jnp.broadcast_to(cum_ref[end_row:end_row + 1, :], (c, HG_W))))
    dec = ecum[end_row:end_row + 1, :]
    qk = [((qs * el).astype(BF16), (k * el).astype(BF16))
          for el in (jnp.exp(-jnp.abs(cum - ref)) for ref in refs)]
    qb, kb, vb = qs.astype(BF16), k.astype(BF16), v.astype(BF16)
    qc = (qs * ecum).astype(BF16)
    kend = (k * eend).astype(BF16)
    outs = []
    for h in range(HG_HEADS):
        sl = slice(h * HG_DK, (h + 1) * HG_DK)
        a = masks[HG_LEVELS] * _dot_nt(qb[:, sl], kb[:, sl])
        for lv in range(HG_LEVELS):
            a = a + masks[lv] * _dot_nt(qk[lv][0][:, sl], qk[lv][1][:, sl])
        s_t = st_ref[h]
        outs.append(_dot(a.astype(BF16), vb[:, sl]) + _dot_nt(qc[:, sl], s_t.astype(BF16)))
        st_ref[h] = s_t * dec[:, sl] + _dot_tn(vb[:, sl], kend[:, sl])
    return jnp.concatenate(outs, axis=1)


def _hgrn_kernel(qf_ref, if_ref, ff_ref, qb_ref, ib_ref, fb_ref, lb_ref, s0_ref, tri_ref, m_ref,
                 of_ref, ob_ref, sfin_ref, stf_ref, stb_ref, cumf_ref, cumb_ref, *, tile):
    c = HG_CHUNK
    nct = tile // c
    tt = pl.program_id(1)

    @pl.when(tt == 0)
    def _():
        stf_ref[...] = s0_ref[0]
        stb_ref[...] = s0_ref[1]

    lbf = lb_ref[0:1, :]
    lbb = lb_ref[1:2, :]

    def body(ci, carry):
        rf = pl.multiple_of(ci * c, c)
        rb = pl.multiple_of((nct - 1 - ci) * c, c)
        q = qf_ref[pl.ds(rf, c), :]
        of_ref[pl.ds(rf, c), :] = _hgrn_chunk(
            q * _sigmoid(q), ff_ref[pl.ds(rf, c), :], if_ref[pl.ds(rf, c), :], lbf, tri_ref[0],
            [m_ref[0, x] for x in range(HG_LEVELS + 1)], stf_ref, cumf_ref, True)
        q = qb_ref[pl.ds(rb, c), :]
        ob_ref[pl.ds(rb, c), :] = _hgrn_chunk(
            q * _sigmoid(q), fb_ref[pl.ds(rb, c), :], ib_ref[pl.ds(rb, c), :], lbb, tri_ref[1],
            [m_ref[1, x] for x in range(HG_LEVELS + 1)], stb_ref, cumb_ref, False)
        return carry

    lax.fori_loop(0, nct, body, 0)

    @pl.when(tt == pl.num_programs(1) - 1)
    def _():
        sfin_ref[0] = stf_ref[...]
        sfin_ref[1] = stb_ref[...]


def _hgrn_scan(p, lb, s0_t, n_batch, t_len, row0):
    tri, mm = _hgrn_constants()
    tri = jnp.asarray(tri, BF16)
    mm = jnp.asarray(mm, F32)
    tile = min(t_len, 512)
    ntt = t_len // tile
    r0 = row0 // tile

    def fwd(c0):
        return pl.BlockSpec((tile, HG_W), lambda b, t: (r0 + b * ntt + t, c0 // HG_W))

    def bwd(c0):
        return pl.BlockSpec((tile, HG_W), lambda b, t: (r0 + b * ntt + ntt - 1 - t, c0 // HG_W))

    st_spec = pl.BlockSpec((None, 2, HG_HEADS, 128, 128), lambda b, t: (b, 0, 0, 0, 0))
    n_rows = n_batch * t_len
    return pl.pallas_call(
        functools.partial(_hgrn_kernel, tile=tile),
        grid=(n_batch, ntt),
        in_specs=[
            fwd(_HQ), fwd(_HI), fwd(_HFF), bwd(_HQ), bwd(_HI), bwd(_HFB),
            pl.BlockSpec((2, HG_W), lambda b, t: (0, 0)),
            st_spec,
            pl.BlockSpec((2, HG_CHUNK, HG_CHUNK), lambda b, t: (0, 0, 0)),
            pl.BlockSpec((2, HG_LEVELS + 1, HG_CHUNK, HG_CHUNK), lambda b, t: (0, 0, 0, 0)),
        ],
        out_specs=[
            pl.BlockSpec((tile, HG_W), lambda b, t: (b * ntt + t, 0)),
            pl.BlockSpec((tile, HG_W), lambda b, t: (b * ntt + ntt - 1 - t, 0)),
            st_spec,
        ],
        out_shape=[jax.ShapeDtypeStruct((n_rows, HG_W), F32),
                   jax.ShapeDtypeStruct((n_rows, HG_W), F32),
                   jax.ShapeDtypeStruct((n_batch, 2, HG_HEADS, 128, 128), F32)],
        scratch_shapes=[pltpu.VMEM((HG_HEADS, 128, 128), F32), pltpu.VMEM((HG_HEADS, 128, 128), F32),
                        pltpu.VMEM((HG_CHUNK, HG_W), F32), pltpu.VMEM((HG_CHUNK, HG_W), F32)],
        compiler_params=_cp(("arbitrary", "arbitrary")),
        name="hgrn_scan",
    )(p, p, p, p, p, p, lb, s0_t, tri, mm)


def _hgrn_fin_kernel(of_ref, ob_ref, gt_ref, ng_ref, oin_ref, o_ref):
    del oin_ref
    od = of_ref[...] + ob_ref[...]
    gt = gt_ref[...]
    gate = gt * _sigmoid(gt)
    for h in range(HG_HEADS):
        sl = slice(h * HG_DK, (h + 1) * HG_DK)
        x = od[:, sl]
        ms = jnp.mean(x * x, axis=-1, keepdims=True)
        o_ref[:, sl] = x * lax.rsqrt(ms + EPS) * ng_ref[...] * gate[:, sl]


def _hgrn_finish(o_f, o_b, p, norm_g, o_full, row0, tm=512):
    n_rows = o_f.shape[0]
    r0 = row0 // tm
    return pl.pallas_call(
        _hgrn_fin_kernel,
        grid=(n_rows // tm,),
        in_specs=[
            pl.BlockSpec((tm, HG_W), lambda i: (i, 0)),
            pl.BlockSpec((tm, HG_W), lambda i: (i, 0)),
            pl.BlockSpec((tm, HG_W), lambda i: (r0 + i, _HGT // HG_W)),
            pl.BlockSpec((1, 128), lambda i: (0, 0)),
            pl.BlockSpec(memory_space=pl.ANY),
        ],
        out_specs=pl.BlockSpec((tm, HG_W), lambda i: (r0 + i, 1)),
        out_shape=jax.ShapeDtypeStruct((NT, D), F32),
        input_output_aliases={4: 0},
        compiler_params=_cp(("arbitrary",)),
        name="hgrn_finish",
    )(o_f, o_b, p, norm_g.reshape(1, 128), o_full)


def _hgrn(p, lb, norm_g, s0_t, o_full, n_batch, t_len, row0):
    o_f, o_b, s_fin = _hgrn_scan(p, lb, s0_t, n_batch, t_len, row0)
    return _hgrn_finish(o_f, o_b, p, norm_g, o_full, row0), s_fin


ROUTE_TM = 512


def _router_kernel(x_ref, g_ref, sh_ref, sc_ref, whi_ref, wlo_ref, br_ref, tri_ref,
                   h_ref, mi_ref, mf_ref, cnt_ref, carry_ref):
    i = pl.program_id(0)

    @pl.when(i == 0)
    def _():
        carry_ref[...] = jnp.zeros_like(carry_ref)

    h = _modulate(x_ref[...], g_ref[...], sh_ref[...], sc_ref[...])
    h_ref[...] = h
    logits = _dot3(h, whi_ref[...], wlo_ref[...]) + br_ref[...]
    lane = lax.broadcasted_iota(jnp.int32, logits.shape, 1)
    neg = jnp.float32(-3e38)
    is_g = lane < N_GROUPS
    lg = jnp.where(is_g, logits, neg)
    mg = jnp.max(lg, axis=-1, keepdims=True)
    grp = jnp.min(jnp.where(lg == mg, lane, 128), axis=-1, keepdims=True)
    pg = 1.0 / jnp.sum(jnp.where(is_g, jnp.exp(lg - mg), 0.0), axis=-1, keepdims=True)
    ex = lane - N_GROUPS
    in_grp = (ex >= 0) & (ex < N_EXPERTS) & ((ex >> 3) == grp)
    le = jnp.where(in_grp, logits, neg)
    v1 = jnp.max(le, axis=-1, keepdims=True)
    i1 = jnp.min(jnp.where(le == v1, lane, 128), axis=-1, keepdims=True)
    le2 = jnp.where(lane == i1, neg, le)
    v2 = jnp.max(le2, axis=-1, keepdims=True)
    i2 = jnp.min(jnp.where(le2 == v2, lane, 128), axis=-1, keepdims=True)
    t = jnp.exp(v2 - v1)
    g1 = pg / (1.0 + t)
    g2 = pg * t / (1.0 + t)
    oh = jnp.where((lane == i1) | (lane == i2), 1.0, 0.0)
    prefix = _dot(tri_ref[...], oh.astype(BF16)) + carry_ref[...]
    r1 = jnp.sum(jnp.where(lane == i1, prefix, 0.0), axis=-1, keepdims=True).astype(jnp.int32)
    r2 = jnp.sum(jnp.where(lane == i2, prefix, 0.0), axis=-1, keepdims=True).astype(jnp.int32)
    carry_ref[...] = carry_ref[...] + jnp.sum(oh, axis=0, keepdims=True)
    cnt_ref[...] = carry_ref[...]
    mi_ref[...] = jnp.where(lane == 0, i1 - N_GROUPS, jnp.where(lane == 1, i2 - N_GROUPS,
                            jnp.where(lane == 2, r1, jnp.where(lane == 3, r2, 0))))
    mf_ref[...] = jnp.where(lane == 0, g1, jnp.where(lane == 1, g2, 0.0))


def _router(x, g, mod, l, wg, bg, we, be):
    tm = ROUTE_TM
    wr = jnp.zeros((D, 128), F32).at[:, :N_GROUPS].set(wg).at[:, N_GROUPS:N_GROUPS + N_EXPERTS].set(we)
    br = jnp.zeros((1, 128), F32).at[0, :N_GROUPS].set(bg).at[0, N_GROUPS:N_GROUPS + N_EXPERTS].set(be)
    w_hi = wr.astype(BF16)
    w_lo = (wr - w_hi.astype(F32)).astype(BF16)
    tri = jnp.asarray(np.tril(np.ones((tm, tm), np.float32), -1), BF16)
    const = lambda i: (0, 0)
    return pl.pallas_call(
        _router_kernel,
        grid=(NT // tm,),
        in_specs=[
            pl.BlockSpec((tm, D), lambda i: (i, 0)),
            pl.BlockSpec((1, D), const),
            _mod_spec(l, 3, tm),
            _mod_spec(l, 4, tm),
            pl.BlockSpec((D, 128), const),
            pl.BlockSpec((D, 128), const),
            pl.BlockSpec((1, 128), const),
            pl.BlockSpec((tm, tm), const),
        ],
        out_specs=[
            pl.BlockSpec((tm, D), lambda i: (i, 0)),
            pl.BlockSpec((tm, 128), lambda i: (i, 0)),
            pl.BlockSpec((tm, 128), lambda i: (i, 0)),
            pl.BlockSpec((1, 128), const),
        ],
        out_shape=[
            jax.ShapeDtypeStruct((NT, D), F32),
            jax.ShapeDtypeStruct((NT, 128), jnp.int32),
            jax.ShapeDtypeStruct((NT, 128), F32),
            jax.ShapeDtypeStruct((1, 128), F32),
        ],
        scratch_shapes=[pltpu.VMEM((1, 128), F32)],
        compiler_params=_cp(("arbitrary",)),
        name="router",
    )(x, g.reshape(1, D), mod, mod, w_hi, w_lo, br, tri)


DISP_TM = 512


def _row_copy(src, s_row, dst, d_row, sem):
    return pltpu.make_async_copy(src.at[pl.ds(s_row, 1)], dst.at[pl.ds(d_row, 1)], sem)


def _dispatch_kernel(dest_ref, h_ref, xs_hbm, sem):
    def issue(t, carry):
        for k in range(2):
            _row_copy(h_ref, t, xs_hbm, dest_ref[k, t], sem).start()
        return carry

    lax.fori_loop(0, DISP_TM, issue, 0)

    def drain(t, carry):
        for k in range(2):
            _row_copy(h_ref, t, xs_hbm, dest_ref[k, t], sem).wait()
        return carry

    lax.fori_loop(0, DISP_TM, drain, 0)


def _dispatch(dest_t, h):
    return pl.pallas_call(
        _dispatch_kernel,
        grid=(NT // DISP_TM,),
        in_specs=[
            pl.BlockSpec((2, DISP_TM), lambda i: (0, i), memory_space=pltpu.SMEM),
            pl.BlockSpec((DISP_TM, D), lambda i: (i, 0)),
        ],
        out_specs=pl.BlockSpec(memory_space=pl.ANY),
        out_shape=jax.ShapeDtypeStruct((MOE_SLOTS, D), F32),
        scratch_shapes=[pltpu.SemaphoreType.DMA(())],
        compiler_params=_cp(("arbitrary",)),
        name="moe_dispatch",
    )(dest_t, h)


def _expert_kernel(bexp_ref, bval_ref, xs_ref, w1_ref, w3_ref, w2_ref, y_ref):
    del bexp_ref
    nv = bval_ref[pl.program_id(0)]

    @pl.when(nv > 0)
    def _():
        row = lax.broadcasted_iota(jnp.int32, (MOE_MB, 1), 0)
        xb = jnp.where(row < nv, xs_ref[...], 0.0).astype(BF16)
        a = _dot(xb, w1_ref[...])
        b = _dot(xb, w3_ref[...])
        hid = (a * _sigmoid(a) * b).astype(BF16)
        y_ref[...] = _dot(hid, w2_ref[...])

    @pl.when(nv <= 0)
    def _():
        y_ref[...] = jnp.zeros_like(y_ref)


def _experts(blk_exp, blk_valid, xs, w1, w3, w2):
    grid_spec = pltpu.PrefetchScalarGridSpec(
        num_scalar_prefetch=2,
        grid=(MOE_NB,),
        in_specs=[
            pl.BlockSpec((MOE_MB, D), lambda i, be, bv: (i, 0)),
            pl.BlockSpec((None, D, D_EXPERT), lambda i, be, bv: (be[i], 0, 0)),
            pl.BlockSpec((None, D, D_EXPERT), lambda i, be, bv: (be[i], 0, 0)),
            pl.BlockSpec((None, D_EXPERT, D), lambda i, be, bv: (be[i], 0, 0)),
        ],
        out_specs=pl.BlockSpec((MOE_MB, D), lambda i, be, bv: (i, 0)),
    )
    return pl.pallas_call(
        _expert_kernel,
        grid_spec=grid_spec,
        out_shape=jax.ShapeDtypeStruct((MOE_SLOTS, D), F32),
        compiler_params=_cp(("arbitrary",)),
        name="moe_experts",
    )(blk_exp, blk_valid, xs, w1, w3, w2)


COMB_TM = 256


def _combine_kernel(dest_ref, mf_ref, x_ref, gate_ref, y_hbm, o_ref, buf_ref, sem):
    def issue(t, carry):
        for k in range(2):
            _row_copy(y_hbm, dest_ref[k, t], buf_ref.at[k], t, sem).start()
        return carry

    lax.fori_loop(0, COMB_TM, issue, 0)

    def drain(t, carry):
        for k in range(2):
            _row_copy(y_hbm, dest_ref[k, t], buf_ref.at[k], t, sem).wait()
        return carry

    lax.fori_loop(0, COMB_TM, drain, 0)
    mf = mf_ref[...]
    y = mf[:, 0:1] * buf_ref[0] + mf[:, 1:2] * buf_ref[1]
    o_ref[...] = x_ref[...] + gate_ref[...] * y


def _combine(dest_t, mf, x, mod, l, y):
    tm = COMB_TM
    return pl.pallas_call(
        _combine_kernel,
        grid=(NT // tm,),
        in_specs=[
            pl.BlockSpec((2, tm), lambda i: (0, i), memory_space=pltpu.SMEM),
            pl.BlockSpec((tm, 128), lambda i: (i, 0)),
            pl.BlockSpec((tm, D), lambda i: (i, 0)),
            _mod_spec(l, 5, tm),
            pl.BlockSpec(memory_space=pl.ANY),
        ],
        out_specs=pl.BlockSpec((tm, D), lambda i: (i, 0)),
        out_shape=jax.ShapeDtypeStruct((NT, D), F32),
        scratch_shapes=[pltpu.VMEM((2, tm, D), F32), pltpu.SemaphoreType.DMA(())],
        compiler_params=_cp(("arbitrary",)),
        name="moe_combine",
    )(dest_t, mf, x, mod, y)


def _moe_plan(mi, counts):
    cnt = counts[0, N_GROUPS:N_GROUPS + N_EXPERTS].astype(jnp.int32)
    padded = (cnt + MOE_MB - 1) // MOE_MB * MOE_MB
    pad_ends = jnp.cumsum(padded)
    pad_starts = pad_ends - padded
    dest = pad_starts[mi[:, 0:2]] + mi[:, 2:4]
    blk0 = jnp.arange(MOE_NB, dtype=jnp.int32) * MOE_MB
    blk_exp = jnp.sum((pad_ends[None, :] <= blk0[:, None]).astype(jnp.int32), axis=1)
    blk_exp = jnp.minimum(blk_exp, N_EXPERTS - 1)
    blk_valid = jnp.clip(cnt[blk_exp] - (blk0 - pad_starts[blk_exp]), 0, MOE_MB)
    blk_valid = jnp.where(blk0 < pad_ends[-1], blk_valid, 0).astype(jnp.int32)
    return dest.T.astype(jnp.int32), blk_exp, blk_valid


def _moe_layer(x, g, mod, l, wg, bg, we, be, w1, w3, w2):
    h, mi, mf, counts = _router(x, g, mod, l, wg, bg, we, be)
    dest_t, blk_exp, blk_valid = _moe_plan(mi, counts)
    xs = _dispatch(dest_t, h)
    y = _experts(blk_exp, blk_valid, xs, w1, w3, w2)
    return _combine(dest_t, mf, x, mod, l, y)


def kernel(x_prompt, x_sample, cache_k_na, cache_v_na, cache_k_swa, cache_v_swa, state_hgrn, c, c_ctx, w_mod, b_mod, norm_mix_g, norm_ffn_g, w_in_even, w_out_even, na_rel_bias, swa_sink, w_in_odd, w_out_odd, conv_w, conv_b, conv_norm_g, conv_norm_b, hgrn_lb_raw, hgrn_norm_g, router_group_w, router_group_b, router_expert_w, router_expert_b, moe_w1, moe_w3, moe_w2, final_norm_g):
    x = jnp.concatenate([x_prompt.reshape(NT_CTX, D), x_sample.reshape(NT_LAT, D)], axis=0)
    cvec = jnp.zeros((16, D), F32).at[0].set(c_ctx).at[1:1 + DEC_BATCH].set(c)
    mod = _mod_table(cvec, w_mod, b_mod)

    lb_p = jax.nn.softmax(hgrn_lb_raw.astype(F32), axis=0)
    lower_bounds = jnp.cumsum(lb_p, axis=0) - lb_p[0:1]
    cos, sin = _rope_tables()

    k_na, v_na, k_swa, v_swa, s_hg = [], [], [], [], []
    for l in range(DEPTH):
        if l % 2 == 0:
            e = l // 2
            p = _modmm(x, norm_mix_g[l], mod, l, w_in_even[e].astype(BF16), tm=512)
            o = _ctx_attn(p, swa_sink[e])
            o = _na_attn(p, cache_k_na, cache_v_na, _na_bias_table(na_rel_bias[e]), e, o)
            o = _swa_attn(p, cache_k_swa, cache_v_swa, swa_sink[e], cos, sin, e, o)
            pc = p[:NT_CTX]
            to_heads = lambda u, nh: u.reshape(BATCH, SEQ, nh, HEAD_DIM).transpose(0, 2, 1, 3)
            k_na.append(to_heads(pc[:, _KA:_KA + 512], NA_HEADS))
            v_na.append(to_heads(pc[:, _VA:_VA + 512], NA_HEADS))
            k_swa.append(to_heads(pc[:, _KB:_KB + 128], SWA_KV_HEADS))
            v_swa.append(to_heads(pc[:, _VB:_VB + 128], SWA_KV_HEADS))
            w_out = w_out_even[e]
        else:
            j = l // 2
            p = _modmm(x, norm_mix_g[l], mod, l, w_in_odd[j].astype(BF16), tm=256)
            o = _conv_module(p, conv_w[j], conv_b[j], conv_norm_g[j], conv_norm_b[j])
            zero_state = jnp.zeros((BATCH, 2, HG_HEADS, 128, 128), F32)
            o, s_ctx = _hgrn(p, lower_bounds[j], hgrn_norm_g[j], zero_state, o, BATCH, SEQ, 0)
            s0_lat = jnp.swapaxes(state_hgrn[:, j], -1, -2)
            o, _ = _hgrn(p, lower_bounds[j], hgrn_norm_g[j], s0_lat, o, DEC_BATCH, DEC_SEQ, NT_CTX)
            s_hg.append(jnp.swapaxes(s_ctx, -1, -2))
            w_out = w_out_odd[j]
        x = _mmres(o, x, mod, l, w_out.astype(BF16))
        x = _moe_layer(x, norm_ffn_g[l], mod, l, router_group_w[l], router_group_b[l],
                       router_expert_w[l], router_expert_b[l],
                       moe_w1[l].astype(BF16), moe_w3[l].astype(BF16), moe_w2[l].astype(BF16))

    y = _final_norm(x, final_norm_g)
    y_prompt = y[:NT_CTX].reshape(BATCH, SEQ, D)
    y_sample = y[NT_CTX:].reshape(DEC_BATCH, DEC_SEQ, D)
    return (y_prompt, y_sample, jnp.stack(k_na, axis=1), jnp.stack(v_na, axis=1),
            jnp.stack(k_swa, axis=1), jnp.stack(v_swa, axis=1), jnp.stack(s_hg, axis=1))
```

```python
import functools

import numpy as np
import jax
import jax.numpy as jnp
from jax import lax
from jax.experimental import pallas as pl
from jax.experimental.pallas import tpu as pltpu

F32 = jnp.float32
BF16 = jnp.bfloat16

D = 1024
BATCH = 16
SEQ = 256
DEPTH = 4
DEC_BATCH = 8
DEC_SEQ = 4096
PAST_LEN = 512
GRID_W = 64
HEAD_DIM = 64
EPS = 1e-6
NA_HEADS = 8
NA_WIN_R = 8
NA_WIN_C = 16
SWA_HEADS = 8
SWA_KV_HEADS = 2
SWA_WINDOW = 128
ROPE_BASE = 10000.0
CONV_CH = 512
CONV_WIDTH = 31
HG_HEADS = 4
HG_DK = 128
HG_CHUNK = 64
N_GROUPS = 4
EXPERTS_PER_GROUP = 8
N_EXPERTS = 32
D_EXPERT = 512
N_EVEN = 2
N_ODD = 2
D_IN_EVEN = 2304
D_IN_ODD = 3584

NT_CTX = BATCH * SEQ
NT_LAT = DEC_BATCH * DEC_SEQ
NT = NT_CTX + NT_LAT
SEG = 4096
assert NT_CTX == SEG and DEC_SEQ == SEG
N_MOD = 1 + DEC_BATCH
MASK = -1e30

MOE_MB = 512
MOE_NB = (2 * NT) // MOE_MB + N_EXPERTS
MOE_SLOTS = MOE_NB * MOE_MB

VMEM_LIMIT = 56 * 1024 * 1024


def _cp(sem, vmem=VMEM_LIMIT):
    return pltpu.CompilerParams(dimension_semantics=sem, vmem_limit_bytes=vmem)


def _dot(a, b):
    return jnp.dot(a, b, preferred_element_type=F32)


def _dot_nt(a, b):
    return lax.dot_general(a, b, (((1,), (1,)), ((), ())), preferred_element_type=F32)


def _dot_tn(a, b):
    return lax.dot_general(a, b, (((0,), (0,)), ((), ())), preferred_element_type=F32)


def _split2(a):
    hi = a.astype(BF16)
    lo = (a - hi.astype(F32)).astype(BF16)
    return hi, lo


def _dot3(a, b_hi, b_lo):
    a_hi, a_lo = _split2(a)
    return _dot(a_hi, b_hi) + (_dot(a_hi, b_lo) + _dot(a_lo, b_hi))


def _sigmoid(x):
    return 1.0 / (1.0 + jnp.exp(-x))


def _modulate(x, g, shift, scale):
    ms = jnp.mean(x * x, axis=-1, keepdims=True)
    return (x * lax.rsqrt(ms + EPS) * g) * (1.0 + scale) + shift


def _mod_kernel(c_ref, whi_ref, wlo_ref, b_ref, o_ref):
    cv = c_ref[...]
    s = cv * _sigmoid(cv)
    o_ref[...] = _dot3(s, whi_ref[...], wlo_ref[...]) + b_ref[...]


def _mod_table(cvec, w_mod, b_mod):
    tn = 1536
    w_hi = w_mod.astype(BF16)
    w_lo = (w_mod - w_hi.astype(F32)).astype(BF16)
    out = pl.pallas_call(
        _mod_kernel,
        grid=(DEPTH, 6 * D // tn),
        in_specs=[
            pl.BlockSpec((16, D), lambda l, j: (0, 0)),
            pl.BlockSpec((None, D, tn), lambda l, j: (l, 0, j)),
            pl.BlockSpec((None, D, tn), lambda l, j: (l, 0, j)),
            pl.BlockSpec((None, 1, tn), lambda l, j: (l, 0, j)),
        ],
        out_specs=pl.BlockSpec((None, 16, tn), lambda l, j: (l, 0, j)),
        out_shape=jax.ShapeDtypeStruct((DEPTH, 16, 6 * D), F32),
        compiler_params=_cp(("arbitrary", "arbitrary")),
        name="mod_table",
    )(cvec, w_hi, w_lo, b_mod.reshape(DEPTH, 1, 6 * D))
    return out.reshape(DEPTH, 16, 1, 6 * D)


def _mod_spec(l, which, tm):
    return pl.BlockSpec((None, None, 1, D), lambda i, *_: (l, (i * tm) // SEG, 0, which))


def _modmm_kernel(x_ref, g_ref, sh_ref, sc_ref, w_ref, o_ref, *, nchunk):
    h = _modulate(x_ref[...], g_ref[...], sh_ref[...], sc_ref[...]).astype(BF16)
    n = o_ref.shape[1]
    for n0 in range(0, n, nchunk):
        o_ref[:, n0:n0 + nchunk] = _dot(h, w_ref[:, n0:n0 + nchunk])


def _modmm(x, g, mod, l, w_bf16, tm):
    n = w_bf16.shape[1]
    return pl.pallas_call(
        functools.partial(_modmm_kernel, nchunk=256),
        grid=(NT // tm,),
        in_specs=[
            pl.BlockSpec((tm, D), lambda i: (i, 0)),
            pl.BlockSpec((1, D), lambda i: (0, 0)),
            _mod_spec(l, 0, tm),
            _mod_spec(l, 1, tm),
            pl.BlockSpec((D, n), lambda i: (0, 0)),
        ],
        out_specs=pl.BlockSpec((tm, n), lambda i: (i, 0)),
        out_shape=jax.ShapeDtypeStruct((NT, n), F32),
        compiler_params=_cp(("arbitrary",)),
        name="modmm",
    )(x, g.reshape(1, D), mod, mod, w_bf16)


def _mmres_kernel(o_ref, x_ref, gate_ref, w_ref, out_ref):
    y = _dot(o_ref[...].astype(BF16), w_ref[...])
    out_ref[...] = x_ref[...] + gate_ref[...] * y


def _mmres(o, x, mod, l, w_bf16, tm=512):
    return pl.pallas_call(
        _mmres_kernel,
        grid=(NT // tm,),
        in_specs=[
            pl.BlockSpec((tm, D), lambda i: (i, 0)),
            pl.BlockSpec((tm, D), lambda i: (i, 0)),
            _mod_spec(l, 2, tm),
            pl.BlockSpec((D, D), lambda i: (0, 0)),
        ],
        out_specs=pl.BlockSpec((tm, D), lambda i: (i, 0)),
        out_shape=jax.ShapeDtypeStruct((NT, D), F32),
        compiler_params=_cp(("arbitrary",)),
        name="mmres",
    )(o, x, mod, w_bf16)


def _final_kernel(x_ref, g_ref, o_ref):
    x = x_ref[...]
    ms = jnp.mean(x * x, axis=-1, keepdims=True)
    o_ref[...] = x * lax.rsqrt(ms + EPS) * g_ref[...]


def _final_norm(x, g, row0, n_rows, tm=512):
    r0 = row0 // tm
    return pl.pallas_call(
        _final_kernel,
        grid=(n_rows // tm,),
        in_specs=[pl.BlockSpec((tm, D), lambda i: (r0 + i, 0)), pl.BlockSpec((1, D), lambda i: (0, 0))],
        out_specs=pl.BlockSpec((tm, D), lambda i: (i, 0)),
        out_shape=jax.ShapeDtypeStruct((n_rows, D), F32),
        compiler_params=_cp(("arbitrary",)),
        name="final_norm",
    )(x, g.reshape(1, D))


_QA, _KA, _VA, _QB, _KB, _VB = 0, 512, 1024, 1536, 2048, 2176


def _ctx_attn_kernel(sink_ref, p_ref, o_ref):
    scale = HEAD_DIM ** -0.5

    def head(qc, kc, vc, sink):
        q = p_ref[:, qc:qc + 64].astype(BF16)
        k = p_ref[:, kc:kc + 64].astype(BF16)
        v = p_ref[:, vc:vc + 64].astype(BF16)
        s = _dot_nt(q, k) * scale
        m = jnp.max(s, axis=-1, keepdims=True)
        if sink is not None:
            m = jnp.maximum(m, sink)
        p = jnp.exp(s - m)
        den = jnp.sum(p, axis=-1, keepdims=True)
        if sink is not None:
            den = den + jnp.exp(sink - m)
        return _dot(p.astype(BF16), v) / den

    for h in range(NA_HEADS):
        o_ref[:, 64 * h:64 * h + 64] = head(_QA + 64 * h, _KA + 64 * h, _VA + 64 * h, None)
    for j in range(SWA_HEADS):
        kv = j // (SWA_HEADS // SWA_KV_HEADS)
        o_ref[:, 512 + 64 * j:512 + 64 * j + 64] = head(_QB + 64 * j, _KB + 64 * kv, _VB + 64 * kv, sink_ref[j])


def _ctx_attn(p, sink):
    return pl.pallas_call(
        _ctx_attn_kernel,
        grid=(BATCH,),
        in_specs=[
            pl.BlockSpec(memory_space=pltpu.SMEM),
            pl.BlockSpec((SEQ, D_IN_EVEN), lambda b: (b, 0)),
        ],
        out_specs=pl.BlockSpec((SEQ, D), lambda b: (b, 0)),
        out_shape=jax.ShapeDtypeStruct((NT, D), F32),
        compiler_params=_cp(("arbitrary",)),
        name="ctx_attn",
    )(sink, p)


NA_QROWS = 8
NA_BAND = 16
NA_TQ = NA_QROWS * GRID_W
NA_TK = NA_BAND * GRID_W
NA_RB = (DEC_SEQ // GRID_W) // NA_QROWS


def _na_band_start(rb):
    rows = DEC_SEQ // GRID_W
    return int(np.clip(NA_QROWS * rb - NA_WIN_R // 2, 0, rows - NA_BAND))


def _na_row_index():
    rows = DEC_SEQ // GRID_W
    n_dr = 2 * NA_WIN_R - 1
    out = np.full((3, NA_QROWS, NA_BAND), n_dr, np.int32)
    for ci, rb in enumerate((0, 1, NA_RB - 1)):
        for ql in range(NA_QROWS):
            qrow = NA_QROWS * rb + ql
            r0 = int(np.clip(qrow - NA_WIN_R // 2, 0, rows - NA_WIN_R))
            for kl in range(NA_BAND):
                krow = _na_band_start(rb) + kl
                if r0 <= krow < r0 + NA_WIN_R:
                    out[ci, ql, kl] = krow - qrow + (NA_WIN_R - 1)
    return out


def _na_bias_table(rel_bias):
    h = rel_bias.shape[0]
    n_dr, n_dc = 2 * NA_WIN_R - 1, 2 * NA_WIN_C - 1
    cols = np.arange(GRID_W)
    c0 = np.clip(cols - NA_WIN_C // 2, 0, GRID_W - NA_WIN_C)
    vc = (cols[None, :] >= c0[:, None]) & (cols[None, :] < c0[:, None] + NA_WIN_C)
    dc = cols[None, :] - cols[:, None] + (NA_WIN_C - 1)
    sel = ((np.arange(n_dc)[:, None, None] == dc[None]) & vc[None]).astype(np.float32)
    bc = jnp.dot(rel_bias.reshape(h * n_dr, n_dc), jnp.asarray(sel.reshape(n_dc, -1)),
                 precision=lax.Precision.HIGHEST).reshape(h, n_dr, GRID_W, GRID_W)
    bc = jnp.where(jnp.asarray(vc), bc, MASK)
    bc = jnp.concatenate([bc, jnp.full((h, 1, GRID_W, GRID_W), MASK, F32)], axis=1)
    tab = jnp.take(bc, jnp.asarray(_na_row_index()), axis=1)
    return tab.transpose(0, 1, 2, 4, 3, 5).reshape(h // 2, 2, 3, NA_TQ, NA_TK)


def _na_kernel(q_ref, k_ref, v_ref, kc_ref, vc_ref, bias_ref, oin_ref, o_ref):
    del oin_ref
    rb = pl.program_id(2)
    scale = HEAD_DIM ** -0.5
    start = jnp.clip(NA_QROWS * rb - NA_WIN_R // 2, 0, DEC_SEQ // GRID_W - NA_BAND) * GRID_W
    start = pl.multiple_of(start, 256)
    case = jnp.where(rb == 0, 0, jnp.where(rb == NA_RB - 1, 2, 1))
    kb = k_ref[pl.ds(start, NA_TK), :].astype(BF16)
    vb = v_ref[pl.ds(start, NA_TK), :].astype(BF16)
    q = q_ref[...]
    lane = lax.broadcasted_iota(jnp.int32, (1, 128), 1)
    o_loc, o_ctx = [], []
    for hh in range(2):
        qm = jnp.where((lane >> 6) == hh, q, 0.0).astype(BF16)
        s_loc = _dot_nt(qm, kb) * scale + bias_ref[hh, case]
        qh = q[:, 64 * hh:64 * hh + 64].astype(BF16)
        s_ctx = _dot_nt(qh, kc_ref[hh].astype(BF16)) * scale
        m = jnp.maximum(jnp.max(s_loc, axis=-1, keepdims=True), jnp.max(s_ctx, axis=-1, keepdims=True))
        p_loc = jnp.exp(s_loc - m)
        p_ctx = jnp.exp(s_ctx - m)
        inv = 1.0 / (jnp.sum(p_loc, axis=-1, keepdims=True) + jnp.sum(p_ctx, axis=-1, keepdims=True))
        o_loc.append(_dot(p_loc.astype(BF16), vb) * inv)
        o_ctx.append(_dot(p_ctx.astype(BF16), vc_ref[hh].astype(BF16)) * inv)
    o_ref[...] = jnp.where((lane >> 6) == 0, o_loc[0], o_loc[1]) + jnp.concatenate(o_ctx, axis=-1)


def _na_attn(p, cache_k, cache_v, bias_tab, e, o_full):
    n_hp = NA_HEADS // 2
    lat0 = NT_CTX // NA_TQ
    per_b = DEC_SEQ // NA_TQ
    return pl.pallas_call(
        _na_kernel,
        grid=(n_hp, DEC_BATCH, NA_RB),
        in_specs=[
            pl.BlockSpec((NA_TQ, 128), lambda hp, b, rb: (lat0 + b * per_b + rb, _QA // 128 + hp)),
            pl.BlockSpec((DEC_SEQ, 128), lambda hp, b, rb: (1 + b, _KA // 128 + hp)),
            pl.BlockSpec((DEC_SEQ, 128), lambda hp, b, rb: (1 + b, _VA // 128 + hp)),
            pl.BlockSpec((None, None, 2, PAST_LEN, HEAD_DIM), lambda hp, b, rb: (b, e, hp, 0, 0)),
            pl.BlockSpec((None, None, 2, PAST_LEN, HEAD_DIM), lambda hp, b, rb: (b, e, hp, 0, 0)),
            pl.BlockSpec((None, 2, 3, NA_TQ, NA_TK), lambda hp, b, rb: (hp, 0, 0, 0, 0)),
            pl.BlockSpec(memory_space=pl.ANY),
        ],
        out_specs=pl.BlockSpec((NA_TQ, 128), lambda hp, b, rb: (lat0 + b * per_b + rb, hp)),
        out_shape=jax.ShapeDtypeStruct((NT, D), F32),
        input_output_aliases={6: 0},
        compiler_params=_cp(("arbitrary", "arbitrary", "arbitrary")),
        name="na_attn",
    )(p, p, p, cache_k, cache_v, bias_tab, o_full)


SWA_TQ = 256
SWA_TK = SWA_TQ + 2 * SWA_WINDOW


def _rope_tables():
    pos = np.arange(DEC_SEQ)
    row = (pos // GRID_W).astype(np.float32)
    col = (pos % GRID_W).astype(np.float32)
    half = HEAD_DIM // 2
    inv = jnp.asarray(ROPE_BASE, F32) ** (-jnp.arange(0, half, 2, dtype=F32) / half)
    ar = jnp.asarray(row)[:, None] * inv[None, :]
    ac = jnp.asarray(col)[:, None] * inv[None, :]
    cos = jnp.concatenate([jnp.cos(ar), jnp.cos(ar), jnp.cos(ac), jnp.cos(ac)], axis=1)
    sin = jnp.concatenate([-jnp.sin(ar), jnp.sin(ar), -jnp.sin(ac), jnp.sin(ac)], axis=1)
    return jnp.tile(cos, (1, 2)), jnp.tile(sin, (1, 2))


def _rope128(t, cos, sin):
    lane = lax.broadcasted_iota(jnp.int32, (1, 128), 1)
    up = pltpu.roll(t, 112, 1)
    dn = pltpu.roll(t, 16, 1)
    sw = jnp.where((lane & 31) < 16, up, dn)
    return t * cos + sw * sin


def _swa_kernel(sink_ref, q_ref, k_ref, v_ref, kc_ref, vc_ref, cos_ref, sin_ref, oin_ref, o_ref, kr_ref):
    del oin_ref
    qb = pl.program_id(1)
    scale = HEAD_DIM ** -0.5
    g = SWA_HEADS // SWA_KV_HEADS

    @pl.when(qb == 0)
    def _():
        kr_ref[...] = _rope128(k_ref[...], cos_ref[...], sin_ref[...]).astype(BF16)

    q0 = pl.multiple_of(qb * SWA_TQ, SWA_TQ)
    bs = pl.multiple_of(jnp.clip(q0 - SWA_WINDOW, 0, DEC_SEQ - SWA_TK), 128)
    cq = cos_ref[pl.ds(q0, SWA_TQ), :]
    sq = sin_ref[pl.ds(q0, SWA_TQ), :]
    kb = kr_ref[pl.ds(bs, SWA_TK), :]
    vb = v_ref[pl.ds(bs, SWA_TK), :].astype(BF16)
    qr = [_rope128(q_ref[:, 128 * j:128 * j + 128], cq, sq) for j in range(SWA_HEADS // 2)]

    rowi = lax.broadcasted_iota(jnp.int32, (g * SWA_TQ, SWA_TK), 0)
    coli = lax.broadcasted_iota(jnp.int32, (g * SWA_TQ, SWA_TK), 1)
    dpos = (q0 + (rowi & (SWA_TQ - 1))) - (bs + coli)
    valid = (dpos <= SWA_WINDOW) & (dpos >= -SWA_WINDOW)
    rowc = lax.broadcasted_iota(jnp.int32, (g * SWA_TQ, 1), 0) >> 8

    for kvh in range(SWA_KV_HEADS):
        heads = []
        for gi in range(g):
            hq = kvh * g + gi
            heads.append(qr[hq // 2][:, 64 * (hq % 2):64 * (hq % 2) + 64])
        q4 = jnp.concatenate(heads, axis=0).astype(BF16)
        sink = jnp.zeros((g * SWA_TQ, 1), F32)
        for gi in range(g):
            sink = jnp.where(rowc == gi, sink_ref[kvh * g + gi], sink)
        kh = kb[:, 64 * kvh:64 * kvh + 64]
        vh = vb[:, 64 * kvh:64 * kvh + 64]
        s_loc = jnp.where(valid, _dot_nt(q4, kh) * scale, MASK)
        s_ctx = _dot_nt(q4, kc_ref[kvh].astype(BF16)) * scale
        m = jnp.maximum(jnp.maximum(jnp.max(s_loc, axis=-1, keepdims=True),
                                    jnp.max(s_ctx, axis=-1, keepdims=True)), sink)
        p_loc = jnp.exp(s_loc - m)
        p_ctx = jnp.exp(s_ctx - m)
        den = (jnp.sum(p_loc, axis=-1, keepdims=True) + jnp.sum(p_ctx, axis=-1, keepdims=True)
               + jnp.exp(sink - m))
        o = (_dot(p_loc.astype(BF16), vh) + _dot(p_ctx.astype(BF16), vc_ref[kvh].astype(BF16))) / den
        for gi in range(g):
            hq = kvh * g + gi
            o_ref[:, 64 * hq:64 * hq + 64] = o[SWA_TQ * gi:SWA_TQ * (gi + 1), :]


def _swa_attn(p, cache_k, cache_v, sink, cos, sin, e, o_full):
    assert SWA_TQ == 256
    lat0 = NT_CTX // SWA_TQ
    per_b = DEC_SEQ // SWA_TQ
    return pl.pallas_call(
        _swa_kernel,
        grid=(DEC_BATCH, per_b),
        in_specs=[
            pl.BlockSpec(memory_space=pltpu.SMEM),
            pl.BlockSpec((SWA_TQ, 512), lambda b, qb: (lat0 + b * per_b + qb, _QB // 512)),
            pl.BlockSpec((DEC_SEQ, 128), lambda b, qb: (1 + b, _KB // 128)),
            pl.BlockSpec((DEC_SEQ, 128), lambda b, qb: (1 + b, _VB // 128)),
            pl.BlockSpec((None, None, 2, PAST_LEN, HEAD_DIM), lambda b, qb: (b, e, 0, 0, 0)),
            pl.BlockSpec((None, None, 2, PAST_LEN, HEAD_DIM), lambda b, qb: (b, e, 0, 0, 0)),
            pl.BlockSpec((DEC_SEQ, 128), lambda b, qb: (0, 0)),
            pl.BlockSpec((DEC_SEQ, 128), lambda b, qb: (0, 0)),
            pl.BlockSpec(memory_space=pl.ANY),
        ],
        out_specs=pl.BlockSpec((SWA_TQ, 512), lambda b, qb: (lat0 + b * per_b + qb, 1)),
        out_shape=jax.ShapeDtypeStruct((NT, D), F32),
        scratch_shapes=[pltpu.VMEM((DEC_SEQ, 128), BF16)],
        input_output_aliases={8: 0},
        compiler_params=_cp(("arbitrary", "arbitrary")),
        name="swa_attn",
    )(sink, p, p, p, cache_k, cache_v, cos, sin, o_full)


CONV_TT = 256
CONV_HALO = 16
CONV_RC = 32


def _conv_kernel(u_ref, ul_ref, ur_ref, w_ref, cb_ref, lg_ref, lb_ref, o_ref, xs_ref):
    i = pl.program_id(0)
    n_ctx_tiles = NT_CTX // CONV_TT
    per_seq = DEC_SEQ // CONV_TT
    j = i - n_ctx_tiles
    is_ctx = i < n_ctx_tiles
    is_start = is_ctx | ((j % per_seq) == 0)
    is_end = is_ctx | ((j % per_seq) == per_seq - 1)

    def glu(u):
        return u[:, :CONV_CH] * _sigmoid(u[:, CONV_CH:])

    xs_ref[CONV_HALO:CONV_HALO + CONV_TT, :] = glu(u_ref[...])
    xs_ref[0:CONV_HALO, :] = jnp.where(is_start, 0.0, glu(ul_ref[...]))
    xs_ref[CONV_HALO + CONV_TT:, :] = jnp.where(is_end, 0.0, glu(ur_ref[...]))

    pad = CONV_WIDTH // 2
    for c in range(CONV_TT // CONV_RC):
        base = CONV_HALO + c * CONV_RC - pad
        acc = jnp.zeros((CONV_RC, CONV_CH), F32)
        for t in range(CONV_WIDTH):
            acc = acc + xs_ref[base + t:base + t + CONV_RC, :] * w_ref[t:t + 1, :]
        y = acc + cb_ref[...]
        mu = jnp.mean(y, axis=-1, keepdims=True)
        yc = y - mu
        var = jnp.mean(yc * yc, axis=-1, keepdims=True)
        yn = yc * lax.rsqrt(var + EPS) * lg_ref[...] + lb_ref[...]
        o_ref[c * CONV_RC:(c + 1) * CONV_RC, :] = yn * _sigmoid(yn)


def _conv_module(p, conv_w, conv_b, ln_g, ln_b):
    nh = CONV_TT // CONV_HALO
    last = NT // CONV_HALO - 1
    return pl.pallas_call(
        _conv_kernel,
        grid=(NT // CONV_TT,),
        in_specs=[
            pl.BlockSpec((CONV_TT, 2 * CONV_CH), lambda i: (i, 0)),
            pl.BlockSpec((CONV_HALO, 2 * CONV_CH), lambda i: (jnp.maximum(i * nh - 1, 0), 0)),
            pl.BlockSpec((CONV_HALO, 2 * CONV_CH), lambda i: (jnp.minimum((i + 1) * nh, last), 0)),
            pl.BlockSpec((CONV_WIDTH, CONV_CH), lambda i: (0, 0)),
            pl.BlockSpec((1, CONV_CH), lambda i: (0, 0)),
            pl.BlockSpec((1, CONV_CH), lambda i: (0, 0)),
            pl.BlockSpec((1, CONV_CH), lambda i: (0, 0)),
        ],
        out_specs=pl.BlockSpec((CONV_TT, CONV_CH), lambda i: (i, 0)),
        out_shape=jax.ShapeDtypeStruct((NT, D), F32),
        scratch_shapes=[pltpu.VMEM((CONV_TT + 2 * CONV_HALO, CONV_CH), F32)],
        compiler_params=_cp(("arbitrary",)),
        name="conv_module",
    )(p, p, p, conv_w, conv_b.reshape(1, -1), ln_g.reshape(1, -1), ln_b.reshape(1, -1))


_U, _HQ, _HI, _HFF, _HFB, _HGT = 0, 1024, 1536, 2048, 2560, 3072
HG_LEVELS = 6
HG_W = HG_HEADS * HG_DK


def _hgrn_constants():
    c = HG_CHUNK
    mm = np.zeros((HG_LEVELS + 1, c, c), np.float32)
    for lv in range(HG_LEVELS):
        n = c >> lv
        half = n // 2
        for t in range(c):
            for s in range(c):
                if t // n == s // n and t % n >= half and s % n < half:
                    mm[lv, t, s] = 1.0
    mm[HG_LEVELS] = np.eye(c, dtype=np.float32)
    tri = np.tril(np.ones((c, c), np.float32))
    return np.stack([tri, tri.T]), np.stack([mm, mm[:, ::-1, ::-1]])


def _hgrn_level_refs(cum, cum_ref, fwd):
    c = HG_CHUNK
    cum_ref[...] = cum
    refs = []
    for lv in range(HG_LEVELS - 2):
        n = c >> lv
        rows = [b * n + n // 2 - (1 if fwd else 0) for b in range(c // n)]
        refs.append(jnp.concatenate([jnp.broadcast_to(cum_ref[r:r + 1, :], (n, HG_W)) for r in rows], axis=0))
    row = lax.broadcasted_iota(jnp.int32, (c, HG_W), 0)
    for n in (4, 2):
        p = n // 2 - (1 if fwd else 0)
        res = row & (n - 1)
        ref = None
        for j in range(n):
            shift = (j - p) % c
            cand = cum if shift == 0 else pltpu.roll(cum, shift, 0)
            ref = cand if ref is None else jnp.where(res == j, cand, ref)
        refs.append(ref)
    return refs


def _hgrn_chunk(qs, f, v, lb, tri, masks, st_ref, cum_ref, fwd):
    c = HG_CHUNK
    g = jnp.log(lb + (1.0 - lb) * _sigmoid(f))
    k = (1.0 - lb) * _sigmoid(-f)
    g_hi = g.astype(BF16)
    r1 = g - g_hi.astype(F32)
    g_mid = r1.astype(BF16)
    g_lo = (r1 - g_mid.astype(F32)).astype(BF16)
    cum = _dot(tri, g_hi) + (_dot(tri, g_mid) + _dot(tri, g_lo))
    refs = _hgrn_level_refs(cum, cum_ref, fwd)
    end_row = c - 1 if fwd else 0
    ecum = jnp.exp(cum)
    eend = jnp.exp(-jnp.abs(cum - jnp.broadcast_to(cum_ref[end_row:end_row + 1, :], (c, HG_W))))
    dec = ecum[end_row:end_row + 1, :]
    qk = [((qs * el).astype(BF16), (k * el).astype(BF16))
          for el in (jnp.exp(-jnp.abs(cum - ref)) for ref in refs)]
    qb, kb, vb = qs.astype(BF16), k.astype(BF16), v.astype(BF16)
    qc = (qs * ecum).astype(BF16)
    kend = (k * eend).astype(BF16)
    outs = []
    for h in range(HG_HEADS):
        sl = slice(h * HG_DK, (h + 1) * HG_DK)
        a = masks[HG_LEVELS] * _dot_nt(qb[:, sl], kb[:, sl])
        for lv in range(HG_LEVELS):
            a = a + masks[lv] * _dot_nt(qk[lv][0][:, sl], qk[lv][1][:, sl])
        s_t = st_ref[h]
        outs.append(_dot(a.astype(BF16), vb[:, sl]) + _dot_nt(qc[:, sl], s_t.astype(BF16)))
        st_ref[h] = s_t * dec[:, sl] + _dot_tn(vb[:, sl], kend[:, sl])
    return jnp.concatenate(outs, axis=1)


def _hgrn_kernel(qf_ref, if_ref, ff_ref, qb_ref, ib_ref, fb_ref, lb_ref, s0_ref, tri_ref, m_ref,
                 of_ref, ob_ref, sfin_ref, stf_ref, stb_ref, cumf_ref, cumb_ref, *, tile):
    c = HG_CHUNK
    nct = tile // c
    tt = pl.program_id(1)

    @pl.when(tt == 0)
    def _():
        stf_ref[...] = s0_ref[0]
        stb_ref[...] = s0_ref[1]

    lbf = lb_ref[0:1, :]
    lbb = lb_ref[1:2, :]

    def body(ci, carry):
        rf = pl.multiple_of(ci * c, c)
        rb = pl.multiple_of((nct - 1 - ci) * c, c)
        q = qf_ref[pl.ds(rf, c), :]
        of_ref[pl.ds(rf, c), :] = _hgrn_chunk(
            q * _sigmoid(q), ff_ref[pl.ds(rf, c), :], if_ref[pl.ds(rf, c), :], lbf, tri_ref[0],
            [m_ref[0, x] for x in range(HG_LEVELS + 1)], stf_ref, cumf_ref, True)
        q = qb_ref[pl.ds(rb, c), :]
        ob_ref[pl.ds(rb, c), :] = _hgrn_chunk(
            q * _sigmoid(q), fb_ref[pl.ds(rb, c), :], ib_ref[pl.ds(rb, c), :], lbb, tri_ref[1],
            [m_ref[1, x] for x in range(HG_LEVELS + 1)], stb_ref, cumb_ref, False)
        return carry

    lax.fori_loop(0, nct, body, 0)

    @pl.when(tt == pl.num_programs(1) - 1)
    def _():
        sfin_ref[0] = stf_ref[...]
        sfin_ref[1] = stb_ref[...]


def _hgrn_scan(p, lb, s0_t, n_batch, t_len, row0):
    tri, mm = _hgrn_constants()
    tri = jnp.asarray(tri, BF16)
    mm = jnp.asarray(mm, F32)
    tile = min(t_len, 512)
    ntt = t_len // tile
    r0 = row0 // tile

    def fwd(c0):
        return pl.BlockSpec((tile, HG_W), lambda b, t: (r0 + b * ntt + t, c0 // HG_W))

    def bwd(c0):
        return pl.BlockSpec((tile, HG_W), lambda b, t: (r0 + b * ntt + ntt - 1 - t, c0 // HG_W))

    st_spec = pl.BlockSpec((None, 2, HG_HEADS, 128, 128), lambda b, t: (b, 0, 0, 0, 0))
    n_rows = n_batch * t_len
    return pl.pallas_call(
        functools.partial(_hgrn_kernel, tile=tile),
        grid=(n_batch, ntt),
        in_specs=[
            fwd(_HQ), fwd(_HI), fwd(_HFF), bwd(_HQ), bwd(_HI), bwd(_HFB),
            pl.BlockSpec((2, HG_W), lambda b, t: (0, 0)),
            st_spec,
            pl.BlockSpec((2, HG_CHUNK, HG_CHUNK), lambda b, t: (0, 0, 0)),
            pl.BlockSpec((2, HG_LEVELS + 1, HG_CHUNK, HG_CHUNK), lambda b, t: (0, 0, 0, 0)),
        ],
        out_specs=[
            pl.BlockSpec((tile, HG_W), lambda b, t: (b * ntt + t, 0)),
            pl.BlockSpec((tile, HG_W), lambda b, t: (b * ntt + ntt - 1 - t, 0)),
            st_spec,
        ],
        out_shape=[jax.ShapeDtypeStruct((n_rows, HG_W), F32),
                   jax.ShapeDtypeStruct((n_rows, HG_W), F32),
                   jax.ShapeDtypeStruct((n_batch, 2, HG_HEADS, 128, 128), F32)],
        scratch_shapes=[pltpu.VMEM((HG_HEADS, 128, 128), F32), pltpu.VMEM((HG_HEADS, 128, 128), F32),
                        pltpu.VMEM((HG_CHUNK, HG_W), F32), pltpu.VMEM((HG_CHUNK, HG_W), F32)],
        compiler_params=_cp(("arbitrary", "arbitrary")),
        name="hgrn_scan",
    )(p, p, p, p, p, p, lb, s0_t, tri, mm)


def _hgrn_fin_kernel(of_ref, ob_ref, gt_ref, ng_ref, oin_ref, o_ref):
    del oin_ref
    od = of_ref[...] + ob_ref[...]
    gt = gt_ref[...]
    gate = gt * _sigmoid(gt)
    for h in range(HG_HEADS):
        sl = slice(h * HG_DK, (h + 1) * HG_DK)
        x = od[:, sl]
        ms = jnp.mean(x * x, axis=-1, keepdims=True)
        o_ref[:, sl] = x * lax.rsqrt(ms + EPS) * ng_ref[...] * gate[:, sl]


def _hgrn_finish(o_f, o_b, p, norm_g, o_full, row0, tm=512):
    n_rows = o_f.shape[0]
    r0 = row0 // tm
    return pl.pallas_call(
        _hgrn_fin_kernel,
        grid=(n_rows // tm,),
        in_specs=[
            pl.BlockSpec((tm, HG_W), lambda i: (i, 0)),
            pl.BlockSpec((tm, HG_W), lambda i: (i, 0)),
            pl.BlockSpec((tm, HG_W), lambda i: (r0 + i, _HGT // HG_W)),
            pl.BlockSpec((1, 128), lambda i: (0, 0)),
            pl.BlockSpec(memory_space=pl.ANY),
        ],
        out_specs=pl.BlockSpec((tm, HG_W), lambda i: (r0 + i, 1)),
        out_shape=jax.ShapeDtypeStruct((NT, D), F32),
        input_output_aliases={4: 0},
        compiler_params=_cp(("arbitrary",)),
        name="hgrn_finish",
    )(o_f, o_b, p, norm_g.reshape(1, 128), o_full)


def _hgrn(p, lb, norm_g, s0_t, o_full, n_batch, t_len, row0):
    o_f, o_b, s_fin = _hgrn_scan(p, lb, s0_t, n_batch, t_len, row0)
    return _hgrn_finish(o_f, o_b, p, norm_g, o_full, row0), s_fin


ROUTE_TM = 512


def _router_kernel(x_ref, g_ref, sh_ref, sc_ref, whi_ref, wlo_ref, br_ref, tri_ref,
                   h_ref, mi_ref, mf_ref, cnt_ref, carry_ref):
    i = pl.program_id(0)

    @pl.when(i == 0)
    def _():
        carry_ref[...] = jnp.zeros_like(carry_ref)

    h = _modulate(x_ref[...], g_ref[...], sh_ref[...], sc_ref[...])
    h_ref[...] = h
    logits = _dot3(h, whi_ref[...], wlo_ref[...]) + br_ref[...]
    lane = lax.broadcasted_iota(jnp.int32, logits.shape, 1)
    neg = jnp.float32(-3e38)
    is_g = lane < N_GROUPS
    lg = jnp.where(is_g, logits, neg)
    mg = jnp.max(lg, axis=-1, keepdims=True)
    grp = jnp.min(jnp.where(lg == mg, lane, 128), axis=-1, keepdims=True)
    pg = 1.0 / jnp.sum(jnp.where(is_g, jnp.exp(lg - mg), 0.0), axis=-1, keepdims=True)
    ex = lane - N_GROUPS
    in_grp = (ex >= 0) & (ex < N_EXPERTS) & ((ex >> 3) == grp)
    le = jnp.where(in_grp, logits, neg)
    v1 = jnp.max(le, axis=-1, keepdims=True)
    i1 = jnp.min(jnp.where(le == v1, lane, 128), axis=-1, keepdims=True)
    le2 = jnp.where(lane == i1, neg, le)
    v2 = jnp.max(le2, axis=-1, keepdims=True)
    i2 = jnp.min(jnp.where(le2 == v2, lane, 128), axis=-1, keepdims=True)
    t = jnp.exp(v2 - v1)
    g1 = pg / (1.0 + t)
    g2 = pg * t / (1.0 + t)
    oh = jnp.where((lane == i1) | (lane == i2), 1.0, 0.0)
    prefix = _dot(tri_ref[...], oh.astype(BF16)) + carry_ref[...]
    r1 = jnp.sum(jnp.where(lane == i1, prefix, 0.0), axis=-1, keepdims=True).astype(jnp.int32)
    r2 = jnp.sum(jnp.where(lane == i2, prefix, 0.0), axis=-1, keepdims=True).astype(jnp.int32)
    carry_ref[...] = carry_ref[...] + jnp.sum(oh, axis=0, keepdims=True)
    cnt_ref[...] = carry_ref[...]
    mi_ref[...] = jnp.where(lane == 0, i1 - N_GROUPS, jnp.where(lane == 1, i2 - N_GROUPS,
                            jnp.where(lane == 2, r1, jnp.where(lane == 3, r2, 0))))
    mf_ref[...] = jnp.where(lane == 0, g1, jnp.where(lane == 1, g2, 0.0))


def _router(x, g, mod, l, wg, bg, we, be):
    tm = ROUTE_TM
    wr = jnp.zeros((D, 128), F32).at[:, :N_GROUPS].set(wg).at[:, N_GROUPS:N_GROUPS + N_EXPERTS].set(we)
    br = jnp.zeros((1, 128), F32).at[0, :N_GROUPS].set(bg).at[0, N_GROUPS:N_GROUPS + N_EXPERTS].set(be)
    w_hi = wr.astype(BF16)
    w_lo = (wr - w_hi.astype(F32)).astype(BF16)
    tri = jnp.asarray(np.tril(np.ones((tm, tm), np.float32), -1), BF16)
    const = lambda i: (0, 0)
    return pl.pallas_call(
        _router_kernel,
        grid=(NT // tm,),
        in_specs=[
            pl.BlockSpec((tm, D), lambda i: (i, 0)),
            pl.BlockSpec((1, D), const),
            _mod_spec(l, 3, tm),
            _mod_spec(l, 4, tm),
            pl.BlockSpec((D, 128), const),
            pl.BlockSpec((D, 128), const),
            pl.BlockSpec((1, 128), const),
            pl.BlockSpec((tm, tm), const),
        ],
        out_specs=[
            pl.BlockSpec((tm, D), lambda i: (i, 0)),
            pl.BlockSpec((tm, 128), lambda i: (i, 0)),
            pl.BlockSpec((tm, 128), lambda i: (i, 0)),
            pl.BlockSpec((1, 128), const),
        ],
        out_shape=[
            jax.ShapeDtypeStruct((NT, D), F32),
            jax.ShapeDtypeStruct((NT, 128), jnp.int32),
            jax.ShapeDtypeStruct((NT, 128), F32),
            jax.ShapeDtypeStruct((1, 128), F32),
        ],
        scratch_shapes=[pltpu.VMEM((1, 128), F32)],
        compiler_params=_cp(("arbitrary",)),
        name="router",
    )(x, g.reshape(1, D), mod, mod, w_hi, w_lo, br, tri)


DISP_TM = 512


def _row_copy(src, s_row, dst, d_row, sem):
    return pltpu.make_async_copy(src.at[pl.ds(s_row, 1)], dst.at[pl.ds(d_row, 1)], sem)


def _dispatch_kernel(dest_ref, h_ref, xs_hbm, sem):
    def group(t8, start):
        base = pl.multiple_of(t8 * 8, 8)
        for s in range(8):
            for k in range(2):
                cp = _row_copy(h_ref, base + s, xs_hbm, dest_ref[k, base + s], sem)
                if start:
                    cp.start(priority=k)
                else:
                    cp.wait()

    lax.fori_loop(0, DISP_TM // 8, lambda t8, c: (group(t8, True), c)[1], 0)
    lax.fori_loop(0, DISP_TM // 8, lambda t8, c: (group(t8, False), c)[1], 0)


def _dispatch(dest_t, h):
    return pl.pallas_call(
        _dispatch_kernel,
        grid=(NT // DISP_TM,),
        in_specs=[
            pl.BlockSpec((2, DISP_TM), lambda i: (0, i), memory_space=pltpu.SMEM),
            pl.BlockSpec((DISP_TM, D), lambda i: (i, 0)),
        ],
        out_specs=pl.BlockSpec(memory_space=pl.ANY),
        out_shape=jax.ShapeDtypeStruct((MOE_SLOTS, D), F32),
        scratch_shapes=[pltpu.SemaphoreType.DMA(())],
        compiler_params=_cp(("arbitrary",)),
        name="moe_dispatch",
    )(dest_t, h)


def _expert_kernel(bexp_ref, bval_ref, bfirst_ref, xs_ref, w1_ref, w3_ref, w2_ref, y_ref,
                   w1b_ref, w3b_ref, w2b_ref):
    del bexp_ref
    i = pl.program_id(0)
    nv = bval_ref[i]

    @pl.when(bfirst_ref[i] == 1)
    def _():
        w1b_ref[...] = w1_ref[...].astype(BF16)
        w3b_ref[...] = w3_ref[...].astype(BF16)
        w2b_ref[...] = w2_ref[...].astype(BF16)

    @pl.when(nv > 0)
    def _():
        row = lax.broadcasted_iota(jnp.int32, (MOE_MB, 1), 0)
        xb = jnp.where(row < nv, xs_ref[...], 0.0).astype(BF16)
        a = _dot(xb, w1b_ref[...])
        b = _dot(xb, w3b_ref[...])
        hid = (a * _sigmoid(a) * b).astype(BF16)
        y_ref[...] = _dot(hid, w2b_ref[...])

    @pl.when(nv <= 0)
    def _():
        y_ref[...] = jnp.zeros_like(y_ref)


def _experts(blk_exp, blk_valid, blk_first, xs, w1, w3, w2, l):
    grid_spec = pltpu.PrefetchScalarGridSpec(
        num_scalar_prefetch=3,
        grid=(MOE_NB,),
        in_specs=[
            pl.BlockSpec((MOE_MB, D), lambda i, be, bv, bf: (i, 0)),
            pl.BlockSpec((None, None, D, D_EXPERT), lambda i, be, bv, bf: (l, be[i], 0, 0)),
            pl.BlockSpec((None, None, D, D_EXPERT), lambda i, be, bv, bf: (l, be[i], 0, 0)),
            pl.BlockSpec((None, None, D_EXPERT, D), lambda i, be, bv, bf: (l, be[i], 0, 0)),
        ],
        out_specs=pl.BlockSpec((MOE_MB, D), lambda i, be, bv, bf: (i, 0)),
        scratch_shapes=[pltpu.VMEM((D, D_EXPERT), BF16), pltpu.VMEM((D, D_EXPERT), BF16),
                        pltpu.VMEM((D_EXPERT, D), BF16)],
    )
    return pl.pallas_call(
        _expert_kernel,
        grid_spec=grid_spec,
        out_shape=jax.ShapeDtypeStruct((MOE_SLOTS, D), F32),
        compiler_params=_cp(("arbitrary",)),
        name="moe_experts",
    )(blk_exp, blk_valid, blk_first, xs, w1, w3, w2)


COMB_TM = 256


def _combine_kernel(dest_ref, mf_ref, x_ref, gate_ref, y_hbm, o_ref, buf_ref, sem):
    def group(t8, start):
        base = pl.multiple_of(t8 * 8, 8)
        for s in range(8):
            for k in range(2):
                cp = _row_copy(y_hbm, dest_ref[k, base + s], buf_ref.at[k], base + s, sem)
                if start:
                    cp.start(priority=k)
                else:
                    cp.wait()

    lax.fori_loop(0, COMB_TM // 8, lambda t8, c: (group(t8, True), c)[1], 0)
    lax.fori_loop(0, COMB_TM // 8, lambda t8, c: (group(t8, False), c)[1], 0)
    mf = mf_ref[...]
    y = mf[:, 0:1] * buf_ref[0] + mf[:, 1:2] * buf_ref[1]
    o_ref[...] = x_ref[...] + gate_ref[...] * y


def _combine(dest_t, mf, x, mod, l, y):
    tm = COMB_TM
    return pl.pallas_call(
        _combine_kernel,
        grid=(NT // tm,),
        in_specs=[
            pl.BlockSpec((2, tm), lambda i: (0, i), memory_space=pltpu.SMEM),
            pl.BlockSpec((tm, 128), lambda i: (i, 0)),
            pl.BlockSpec((tm, D), lambda i: (i, 0)),
            _mod_spec(l, 5, tm),
            pl.BlockSpec(memory_space=pl.ANY),
        ],
        out_specs=pl.BlockSpec((tm, D), lambda i: (i, 0)),
        out_shape=jax.ShapeDtypeStruct((NT, D), F32),
        scratch_shapes=[pltpu.VMEM((2, tm, D), F32), pltpu.SemaphoreType.DMA(())],
        compiler_params=_cp(("arbitrary",)),
        name="moe_combine",
    )(dest_t, mf, x, mod, y)


def _comb_modmm_kernel(dcur_ref, dnxt_ref, mf_ref, x_ref, gate_ref, g_ref, sh_ref, sc_ref, w_ref, y_hbm,
                       xo_ref, p_ref, buf_ref, sem, *, nchunk):
    i = pl.program_id(0)
    n = pl.num_programs(0)
    slot = lax.rem(i, 2)
    tm = x_ref.shape[0]

    def rows(dest_ref, to_slot, lo, hi, start):
        for t in range(lo, hi):
            for k in range(2):
                cp = _row_copy(y_hbm, dest_ref[k, t], buf_ref.at[to_slot, k], t, sem.at[to_slot])
                if start:
                    cp.start(priority=k)
                else:
                    cp.wait()

    def rolled(dest_ref, to_slot, start):
        def body(t8, c):
            base = pl.multiple_of(t8 * 8, 8)
            for s in range(8):
                for k in range(2):
                    cp = _row_copy(y_hbm, dest_ref[k, base + s], buf_ref.at[to_slot, k], base + s,
                                   sem.at[to_slot])
                    if start:
                        cp.start(priority=k)
                    else:
                        cp.wait()
            return c
        lax.fori_loop(0, tm // 8, body, 0)

    @pl.when(i == 0)
    def _():
        rolled(dcur_ref, slot, True)

    rolled(dcur_ref, slot, False)
    mf = mf_ref[...]
    y = mf[:, 0:1] * buf_ref[slot, 0] + mf[:, 1:2] * buf_ref[slot, 1]
    xn = x_ref[...] + gate_ref[...] * y
    xo_ref[...] = xn
    h = _modulate(xn, g_ref[...], sh_ref[...], sc_ref[...]).astype(BF16)
    nout = p_ref.shape[1]
    n_ch = nout // nchunk
    per = -(-tm // n_ch)
    for c in range(n_ch):
        rows(dnxt_ref, 1 - slot, min(c * per, tm), min((c + 1) * per, tm), True)
        p_ref[:, c * nchunk:(c + 1) * nchunk] = _dot(h, w_ref[:, c * nchunk:(c + 1) * nchunk])

    @pl.when(i == n - 1)
    def _():
        rolled(dnxt_ref, 1 - slot, False)


def _comb_modmm(dest_t, mf, x, mod, l, y, g_next, w_bf16, tm=256):
    nout = w_bf16.shape[1]
    nt = NT // tm
    return pl.pallas_call(
        functools.partial(_comb_modmm_kernel, nchunk=256),
        grid=(nt,),
        in_specs=[
            pl.BlockSpec((2, tm), lambda i: (0, i), memory_space=pltpu.SMEM),
            pl.BlockSpec((2, tm), lambda i: (0, jnp.minimum(i + 1, nt - 1)), memory_space=pltpu.SMEM),
            pl.BlockSpec((tm, 128), lambda i: (i, 0)),
            pl.BlockSpec((tm, D), lambda i: (i, 0)),
            _mod_spec(l, 5, tm),
            pl.BlockSpec((1, D), lambda i: (0, 0)),
            _mod_spec(l + 1, 0, tm),
            _mod_spec(l + 1, 1, tm),
            pl.BlockSpec((D, nout), lambda i: (0, 0)),
            pl.BlockSpec(memory_space=pl.ANY),
        ],
        out_specs=[pl.BlockSpec((tm, D), lambda i: (i, 0)), pl.BlockSpec((tm, nout), lambda i: (i, 0))],
        out_shape=[jax.ShapeDtypeStruct((NT, D), F32), jax.ShapeDtypeStruct((NT, nout), F32)],
        scratch_shapes=[pltpu.VMEM((2, 2, tm, D), F32), pltpu.SemaphoreType.DMA((2,))],
        compiler_params=_cp(("arbitrary",)),
        name="moe_combine_modmm",
    )(dest_t, dest_t, mf, x, mod, g_next.reshape(1, D), mod, mod, w_bf16, y)


def _moe_plan(mi, counts):
    cnt = counts[0, N_GROUPS:N_GROUPS + N_EXPERTS].astype(jnp.int32)
    padded = (cnt + MOE_MB - 1) // MOE_MB * MOE_MB
    pad_ends = jnp.cumsum(padded)
    pad_starts = pad_ends - padded
    dest = pad_starts[mi[:, 0:2]] + mi[:, 2:4]
    blk0 = jnp.arange(MOE_NB, dtype=jnp.int32) * MOE_MB
    blk_exp = jnp.sum((pad_ends[None, :] <= blk0[:, None]).astype(jnp.int32), axis=1)
    blk_exp = jnp.minimum(blk_exp, N_EXPERTS - 1)
    blk_valid = jnp.clip(cnt[blk_exp] - (blk0 - pad_starts[blk_exp]), 0, MOE_MB)
    blk_valid = jnp.where(blk0 < pad_ends[-1], blk_valid, 0).astype(jnp.int32)
    blk_first = jnp.concatenate([jnp.ones((1,), jnp.int32),
                                 (blk_exp[1:] != blk_exp[:-1]).astype(jnp.int32)])
    return dest.T.astype(jnp.int32), blk_exp, blk_valid, blk_first


def _moe_experts(x, g, mod, l, wg, bg, we, be, w1, w3, w2):
    h, mi, mf, counts = _router(x, g, mod, l, wg, bg, we, be)
    dest_t, blk_exp, blk_valid, blk_first = _moe_plan(mi, counts)
    xs = _dispatch(dest_t, h)
    y = _experts(blk_exp, blk_valid, blk_first, xs, w1, w3, w2, l)
    return dest_t, mf, y


def kernel(x_prompt, x_sample, cache_k_na, cache_v_na, cache_k_swa, cache_v_swa, state_hgrn, c, c_ctx, w_mod, b_mod, norm_mix_g, norm_ffn_g, w_in_even, w_out_even, na_rel_bias, swa_sink, w_in_odd, w_out_odd, conv_w, conv_b, conv_norm_g, conv_norm_b, hgrn_lb_raw, hgrn_norm_g, router_group_w, router_group_b, router_expert_w, router_expert_b, moe_w1, moe_w3, moe_w2, final_norm_g):
    x = jnp.concatenate([x_prompt.reshape(NT_CTX, D), x_sample.reshape(NT_LAT, D)], axis=0)
    cvec = jnp.zeros((16, D), F32).at[0].set(c_ctx).at[1:1 + DEC_BATCH].set(c)
    mod = _mod_table(cvec, w_mod, b_mod)

    lb_p = jax.nn.softmax(hgrn_lb_raw.astype(F32), axis=0)
    lower_bounds = jnp.cumsum(lb_p, axis=0) - lb_p[0:1]
    cos, sin = _rope_tables()

    k_na, v_na, k_swa, v_swa, s_hg = [], [], [], [], []
    moe = None
    for l in range(DEPTH):
        w_in = (w_in_even if l % 2 == 0 else w_in_odd)[l // 2].astype(BF16)
        if moe is None:
            p = _modmm(x, norm_mix_g[l], mod, l, w_in, tm=512)
        else:
            x, p = _comb_modmm(*moe[:2], x, mod, l - 1, moe[2], norm_mix_g[l], w_in)
        if l % 2 == 0:
            e = l // 2
            o = _ctx_attn(p, swa_sink[e])
            o = _na_attn(p, cache_k_na, cache_v_na, _na_bias_table(na_rel_bias[e]), e, o)
            o = _swa_attn(p, cache_k_swa, cache_v_swa, swa_sink[e], cos, sin, e, o)
            pc = p[:NT_CTX]
            to_heads = lambda u, nh: u.reshape(BATCH, SEQ, nh, HEAD_DIM).transpose(0, 2, 1, 3)
            k_na.append(to_heads(pc[:, _KA:_KA + 512], NA_HEADS))
            v_na.append(to_heads(pc[:, _VA:_VA + 512], NA_HEADS))
            k_swa.append(to_heads(pc[:, _KB:_KB + 128], SWA_KV_HEADS))
            v_swa.append(to_heads(pc[:, _VB:_VB + 128], SWA_KV_HEADS))
            w_out = w_out_even[e]
        else:
            j = l // 2
            o = _conv_module(p, conv_w[j], conv_b[j], conv_norm_g[j], conv_norm_b[j])
            zero_state = jnp.zeros((BATCH, 2, HG_HEADS, 128, 128), F32)
            o, s_ctx = _hgrn(p, lower_bounds[j], hgrn_norm_g[j], zero_state, o, BATCH, SEQ, 0)
            s0_lat = jnp.swapaxes(state_hgrn[:, j], -1, -2)
            o, _ = _hgrn(p, lower_bounds[j], hgrn_norm_g[j], s0_lat, o, DEC_BATCH, DEC_SEQ, NT_CTX)
            s_hg.append(jnp.swapaxes(s_ctx, -1, -2))
            w_out = w_out_odd[j]
        x = _mmres(o, x, mod, l, w_out.astype(BF16))
        moe = _moe_experts(x, norm_ffn_g[l], mod, l, router_group_w[l], router_group_b[l],
                           router_expert_w[l], router_expert_b[l], moe_w1, moe_w3, moe_w2)

    x = _combine(*moe[:2], x, mod, DEPTH - 1, moe[2])
    y_prompt = _final_norm(x, final_norm_g, 0, NT_CTX).reshape(BATCH, SEQ, D)
    y_sample = _final_norm(x, final_norm_g, NT_CTX, NT_LAT).reshape(DEC_BATCH, DEC_SEQ, D)
    return (y_prompt, y_sample, jnp.stack(k_na, axis=1), jnp.stack(v_na, axis=1),
            jnp.stack(k_swa, axis=1), jnp.stack(v_swa, axis=1), jnp.stack(s_hg, axis=1))
```

```python
import functools

import numpy as np
import jax
import jax.numpy as jnp
from jax import lax
from jax.experimental import pallas as pl
from jax.experimental.pallas import tpu as pltpu

F32 = jnp.float32
BF16 = jnp.bfloat16

D = 1024
BATCH = 16
SEQ = 256
DEPTH = 4
DEC_BATCH = 8
DEC_SEQ = 4096
PAST_LEN = 512
GRID_W = 64
HEAD_DIM = 64
EPS = 1e-6
NA_HEADS = 8
NA_WIN_R = 8
NA_WIN_C = 16
SWA_HEADS = 8
SWA_KV_HEADS = 2
SWA_WINDOW = 128
ROPE_BASE = 10000.0
CONV_CH = 512
CONV_WIDTH = 31
HG_HEADS = 4
HG_DK = 128
HG_CHUNK = 64
N_GROUPS = 4
EXPERTS_PER_GROUP = 8
N_EXPERTS = 32
D_EXPERT = 512
N_EVEN = 2
N_ODD = 2
D_IN_EVEN = 2304
D_IN_ODD = 3584

NT_CTX = BATCH * SEQ
NT_LAT = DEC_BATCH * DEC_SEQ
NT = NT_CTX + NT_LAT
SEG = 4096
assert NT_CTX == SEG and DEC_SEQ == SEG
N_MOD = 1 + DEC_BATCH
MASK = -1e30

MOE_MB = 512
MOE_NB = (2 * NT) // MOE_MB + N_EXPERTS
MOE_SLOTS = MOE_NB * MOE_MB

VMEM_LIMIT = 56 * 1024 * 1024


def _cp(sem, vmem=VMEM_LIMIT):
    return pltpu.CompilerParams(dimension_semantics=sem, vmem_limit_bytes=vmem)


def _dot(a, b):
    return jnp.dot(a, b, preferred_element_type=F32)


def _dot_nt(a, b):
    return lax.dot_general(a, b, (((1,), (1,)), ((), ())), preferred_element_type=F32)


def _dot_tn(a, b):
    return lax.dot_general(a, b, (((0,), (0,)), ((), ())), preferred_element_type=F32)


def _split2(a):
    hi = a.astype(BF16)
    lo = (a - hi.astype(F32)).astype(BF16)
    return hi, lo


def _dot3(a, b_hi, b_lo):
    a_hi, a_lo = _split2(a)
    return _dot(a_hi, b_hi) + (_dot(a_hi, b_lo) + _dot(a_lo, b_hi))


def _sigmoid(x):
    return 1.0 / (1.0 + jnp.exp(-x))


def _modulate(x, g, shift, scale):
    ms = jnp.mean(x * x, axis=-1, keepdims=True)
    return (x * lax.rsqrt(ms + EPS) * g) * (1.0 + scale) + shift


def _mod_kernel(c_ref, whi_ref, wlo_ref, b_ref, o_ref):
    cv = c_ref[...]
    s = cv * _sigmoid(cv)
    o_ref[...] = _dot3(s, whi_ref[...], wlo_ref[...]) + b_ref[...]


def _mod_table(cvec, w_mod, b_mod):
    tn = 1536
    w_hi = w_mod.astype(BF16)
    w_lo = (w_mod - w_hi.astype(F32)).astype(BF16)
    out = pl.pallas_call(
        _mod_kernel,
        grid=(DEPTH, 6 * D // tn),
        in_specs=[
            pl.BlockSpec((16, D), lambda l, j: (0, 0)),
            pl.BlockSpec((None, D, tn), lambda l, j: (l, 0, j)),
            pl.BlockSpec((None, D, tn), lambda l, j: (l, 0, j)),
            pl.BlockSpec((None, 1, tn), lambda l, j: (l, 0, j)),
        ],
        out_specs=pl.BlockSpec((None, 16, tn), lambda l, j: (l, 0, j)),
        out_shape=jax.ShapeDtypeStruct((DEPTH, 16, 6 * D), F32),
        compiler_params=_cp(("arbitrary", "arbitrary")),
        name="mod_table",
    )(cvec, w_hi, w_lo, b_mod.reshape(DEPTH, 1, 6 * D))
    return out.reshape(DEPTH, 16, 1, 6 * D)


def _mod_spec(l, which, tm):
    return pl.BlockSpec((None, None, 1, D), lambda i, *_: (l, (i * tm) // SEG, 0, which))


def _modmm_kernel(x_ref, g_ref, sh_ref, sc_ref, w_ref, o_ref, *, nchunk):
    h = _modulate(x_ref[...], g_ref[...], sh_ref[...], sc_ref[...]).astype(BF16)
    n = o_ref.shape[1]
    for n0 in range(0, n, nchunk):
        o_ref[:, n0:n0 + nchunk] = _dot(h, w_ref[:, n0:n0 + nchunk])


def _modmm(x, g, mod, l, w_bf16, tm):
    n = w_bf16.shape[1]
    return pl.pallas_call(
        functools.partial(_modmm_kernel, nchunk=256),
        grid=(NT // tm,),
        in_specs=[
            pl.BlockSpec((tm, D), lambda i: (i, 0)),
            pl.BlockSpec((1, D), lambda i: (0, 0)),
            _mod_spec(l, 0, tm),
            _mod_spec(l, 1, tm),
            pl.BlockSpec((D, n), lambda i: (0, 0)),
        ],
        out_specs=pl.BlockSpec((tm, n), lambda i: (i, 0)),
        out_shape=jax.ShapeDtypeStruct((NT, n), F32),
        compiler_params=_cp(("arbitrary",)),
        name="modmm",
    )(x, g.reshape(1, D), mod, mod, w_bf16)


def _mmres_kernel(o_ref, x_ref, gate_ref, w_ref, out_ref):
    y = _dot(o_ref[...].astype(BF16), w_ref[...])
    out_ref[...] = x_ref[...] + gate_ref[...] * y


def _mmres(o, x, mod, l, w_bf16, tm=512):
    return pl.pallas_call(
        _mmres_kernel,
        grid=(NT // tm,),
        in_specs=[
            pl.BlockSpec((tm, D), lambda i: (i, 0)),
            pl.BlockSpec((tm, D), lambda i: (i, 0)),
            _mod_spec(l, 2, tm),
            pl.BlockSpec((D, D), lambda i: (0, 0)),
        ],
        out_specs=pl.BlockSpec((tm, D), lambda i: (i, 0)),
        out_shape=jax.ShapeDtypeStruct((NT, D), F32),
        compiler_params=_cp(("arbitrary",)),
        name="mmres",
    )(o, x, mod, w_bf16)


def _final_kernel(x_ref, g_ref, o_ref):
    x = x_ref[...]
    ms = jnp.mean(x * x, axis=-1, keepdims=True)
    o_ref[...] = x * lax.rsqrt(ms + EPS) * g_ref[...]


def _final_norm(x, g, row0, n_rows, tm=512):
    r0 = row0 // tm
    return pl.pallas_call(
        _final_kernel,
        grid=(n_rows // tm,),
        in_specs=[pl.BlockSpec((tm, D), lambda i: (r0 + i, 0)), pl.BlockSpec((1, D), lambda i: (0, 0))],
        out_specs=pl.BlockSpec((tm, D), lambda i: (i, 0)),
        out_shape=jax.ShapeDtypeStruct((n_rows, D), F32),
        compiler_params=_cp(("arbitrary",)),
        name="final_norm",
    )(x, g.reshape(1, D))


_QA, _KA, _VA, _QB, _KB, _VB = 0, 512, 1024, 1536, 2048, 2176


def _ctx_attn_kernel(sink_ref, p_ref, o_ref):
    scale = HEAD_DIM ** -0.5

    def head(qc, kc, vc, sink):
        q = p_ref[:, qc:qc + 64].astype(BF16)
        k = p_ref[:, kc:kc + 64].astype(BF16)
        v = p_ref[:, vc:vc + 64].astype(BF16)
        s = _dot_nt(q, k) * scale
        m = jnp.max(s, axis=-1, keepdims=True)
        if sink is not None:
            m = jnp.maximum(m, sink)
        p = jnp.exp(s - m)
        den = jnp.sum(p, axis=-1, keepdims=True)
        if sink is not None:
            den = den + jnp.exp(sink - m)
        return _dot(p.astype(BF16), v) / den

    for h in range(NA_HEADS):
        o_ref[:, 64 * h:64 * h + 64] = head(_QA + 64 * h, _KA + 64 * h, _VA + 64 * h, None)
    for j in range(SWA_HEADS):
        kv = j // (SWA_HEADS // SWA_KV_HEADS)
        o_ref[:, 512 + 64 * j:512 + 64 * j + 64] = head(_QB + 64 * j, _KB + 64 * kv, _VB + 64 * kv, sink_ref[j])


def _ctx_attn(p, sink):
    return pl.pallas_call(
        _ctx_attn_kernel,
        grid=(BATCH,),
        in_specs=[
            pl.BlockSpec(memory_space=pltpu.SMEM),
            pl.BlockSpec((SEQ, D_IN_EVEN), lambda b: (b, 0)),
        ],
        out_specs=pl.BlockSpec((SEQ, D), lambda b: (b, 0)),
        out_shape=jax.ShapeDtypeStruct((NT, D), F32),
        compiler_params=_cp(("arbitrary",)),
        name="ctx_attn",
    )(sink, p)


NA_QROWS = 8
NA_BAND = 16
NA_TQ = NA_QROWS * GRID_W
NA_TK = NA_BAND * GRID_W
NA_RB = (DEC_SEQ // GRID_W) // NA_QROWS


def _na_band_start(rb):
    rows = DEC_SEQ // GRID_W
    return int(np.clip(NA_QROWS * rb - NA_WIN_R // 2, 0, rows - NA_BAND))


def _na_row_index():
    rows = DEC_SEQ // GRID_W
    n_dr = 2 * NA_WIN_R - 1
    out = np.full((3, NA_QROWS, NA_BAND), n_dr, np.int32)
    for ci, rb in enumerate((0, 1, NA_RB - 1)):
        for ql in range(NA_QROWS):
            qrow = NA_QROWS * rb + ql
            r0 = int(np.clip(qrow - NA_WIN_R // 2, 0, rows - NA_WIN_R))
            for kl in range(NA_BAND):
                krow = _na_band_start(rb) + kl
                if r0 <= krow < r0 + NA_WIN_R:
                    out[ci, ql, kl] = krow - qrow + (NA_WIN_R - 1)
    return out


def _na_bias_table(rel_bias):
    h = rel_bias.shape[0]
    n_dr, n_dc = 2 * NA_WIN_R - 1, 2 * NA_WIN_C - 1
    cols = np.arange(GRID_W)
    c0 = np.clip(cols - NA_WIN_C // 2, 0, GRID_W - NA_WIN_C)
    vc = (cols[None, :] >= c0[:, None]) & (cols[None, :] < c0[:, None] + NA_WIN_C)
    dc = cols[None, :] - cols[:, None] + (NA_WIN_C - 1)
    sel = ((np.arange(n_dc)[:, None, None] == dc[None]) & vc[None]).astype(np.float32)
    bc = jnp.dot(rel_bias.reshape(h * n_dr, n_dc), jnp.asarray(sel.reshape(n_dc, -1)),
                 precision=lax.Precision.HIGHEST).reshape(h, n_dr, GRID_W, GRID_W)
    bc = jnp.where(jnp.asarray(vc), bc, MASK)
    bc = jnp.concatenate([bc, jnp.full((h, 1, GRID_W, GRID_W), MASK, F32)], axis=1)
    tab = jnp.take(bc, jnp.asarray(_na_row_index()), axis=1)
    return tab.transpose(0, 1, 2, 4, 3, 5).reshape(h // 2, 2, 3, NA_TQ, NA_TK)


def _na_kernel(q_ref, k_ref, v_ref, kc_ref, vc_ref, bias_ref, oin_ref, o_ref):
    del oin_ref
    rb = pl.program_id(2)
    scale = HEAD_DIM ** -0.5
    start = jnp.clip(NA_QROWS * rb - NA_WIN_R // 2, 0, DEC_SEQ // GRID_W - NA_BAND) * GRID_W
    start = pl.multiple_of(start, 256)
    case = jnp.where(rb == 0, 0, jnp.where(rb == NA_RB - 1, 2, 1))
    kb = k_ref[pl.ds(start, NA_TK), :].astype(BF16)
    vb = v_ref[pl.ds(start, NA_TK), :].astype(BF16)
    q = q_ref[...] * scale
    lane = lax.broadcasted_iota(jnp.int32, (1, 128), 1)
    o_loc, o_ctx = [], []
    for hh in range(2):
        qm = jnp.where((lane >> 6) == hh, q, 0.0).astype(BF16)
        s_loc = _dot_nt(qm, kb) + bias_ref[hh, case]
        qh = q[:, 64 * hh:64 * hh + 64].astype(BF16)
        s_ctx = _dot_nt(qh, kc_ref[hh].astype(BF16))
        m = jnp.maximum(jnp.max(s_loc, axis=-1, keepdims=True), jnp.max(s_ctx, axis=-1, keepdims=True))
        p_loc = jnp.exp(s_loc - m)
        p_ctx = jnp.exp(s_ctx - m)
        inv = 1.0 / (jnp.sum(p_loc, axis=-1, keepdims=True) + jnp.sum(p_ctx, axis=-1, keepdims=True))
        o_loc.append(_dot(p_loc.astype(BF16), vb) * inv)
        o_ctx.append(_dot(p_ctx.astype(BF16), vc_ref[hh].astype(BF16)) * inv)
    o_ref[...] = jnp.where((lane >> 6) == 0, o_loc[0], o_loc[1]) + jnp.concatenate(o_ctx, axis=-1)


def _na_attn(p, cache_k, cache_v, bias_tab, e, o_full):
    n_hp = NA_HEADS // 2
    lat0 = NT_CTX // NA_TQ
    per_b = DEC_SEQ // NA_TQ
    return pl.pallas_call(
        _na_kernel,
        grid=(n_hp, DEC_BATCH, NA_RB),
        in_specs=[
            pl.BlockSpec((NA_TQ, 128), lambda hp, b, rb: (lat0 + b * per_b + rb, _QA // 128 + hp)),
            pl.BlockSpec((DEC_SEQ, 128), lambda hp, b, rb: (1 + b, _KA // 128 + hp)),
            pl.BlockSpec((DEC_SEQ, 128), lambda hp, b, rb: (1 + b, _VA // 128 + hp)),
            pl.BlockSpec((None, None, 2, PAST_LEN, HEAD_DIM), lambda hp, b, rb: (b, e, hp, 0, 0)),
            pl.BlockSpec((None, None, 2, PAST_LEN, HEAD_DIM), lambda hp, b, rb: (b, e, hp, 0, 0)),
            pl.BlockSpec((None, 2, 3, NA_TQ, NA_TK), lambda hp, b, rb: (hp, 0, 0, 0, 0)),
            pl.BlockSpec(memory_space=pl.ANY),
        ],
        out_specs=pl.BlockSpec((NA_TQ, 128), lambda hp, b, rb: (lat0 + b * per_b + rb, hp)),
        out_shape=jax.ShapeDtypeStruct((NT, D), F32),
        input_output_aliases={6: 0},
        compiler_params=_cp(("arbitrary", "arbitrary", "arbitrary")),
        name="na_attn",
    )(p, p, p, cache_k, cache_v, bias_tab, o_full)


SWA_TQ = 256
SWA_TK = SWA_TQ + 2 * SWA_WINDOW


def _rope_tables():
    pos = np.arange(DEC_SEQ)
    row = (pos // GRID_W).astype(np.float32)
    col = (pos % GRID_W).astype(np.float32)
    half = HEAD_DIM // 2
    inv = jnp.asarray(ROPE_BASE, F32) ** (-jnp.arange(0, half, 2, dtype=F32) / half)
    ar = jnp.asarray(row)[:, None] * inv[None, :]
    ac = jnp.asarray(col)[:, None] * inv[None, :]
    cos = jnp.concatenate([jnp.cos(ar), jnp.cos(ar), jnp.cos(ac), jnp.cos(ac)], axis=1)
    sin = jnp.concatenate([-jnp.sin(ar), jnp.sin(ar), -jnp.sin(ac), jnp.sin(ac)], axis=1)
    return jnp.tile(cos, (1, 2)), jnp.tile(sin, (1, 2))


def _rope128(t, cos, sin):
    lane = lax.broadcasted_iota(jnp.int32, (1, 128), 1)
    up = pltpu.roll(t, 112, 1)
    dn = pltpu.roll(t, 16, 1)
    sw = jnp.where((lane & 31) < 16, up, dn)
    return t * cos + sw * sin


def _swa_mask_table():
    out = []
    for qb in (0, 1, DEC_SEQ // SWA_TQ - 1):
        q0 = qb * SWA_TQ
        bs = int(np.clip(q0 - SWA_WINDOW, 0, DEC_SEQ - SWA_TK))
        d = (q0 + np.arange(SWA_TQ))[:, None] - (bs + np.arange(SWA_TK))[None, :]
        out.append(np.where(np.abs(d) <= SWA_WINDOW, 0.0, MASK))
    return np.stack(out).astype(np.float32)


def _swa_kernel(sink_ref, q_ref, k_ref, v_ref, kc_ref, vc_ref, cos_ref, sin_ref, mask_ref, oin_ref, o_ref,
                kr_ref):
    del oin_ref
    qb = pl.program_id(1)
    scale = HEAD_DIM ** -0.5
    g = SWA_HEADS // SWA_KV_HEADS

    @pl.when(qb == 0)
    def _():
        kr_ref[...] = _rope128(k_ref[...], cos_ref[...], sin_ref[...]).astype(BF16)

    q0 = pl.multiple_of(qb * SWA_TQ, SWA_TQ)
    bs = pl.multiple_of(jnp.clip(q0 - SWA_WINDOW, 0, DEC_SEQ - SWA_TK), 128)
    cq = cos_ref[pl.ds(q0, SWA_TQ), :]
    sq = sin_ref[pl.ds(q0, SWA_TQ), :]
    kb = kr_ref[pl.ds(bs, SWA_TK), :]
    vb = v_ref[pl.ds(bs, SWA_TK), :].astype(BF16)
    qr = [_rope128(q_ref[:, 128 * j:128 * j + 128], cq, sq) * scale for j in range(SWA_HEADS // 2)]
    case = jnp.where(qb == 0, 0, jnp.where(qb == pl.num_programs(1) - 1, 2, 1))
    wmask = mask_ref[case]
    rowc = lax.broadcasted_iota(jnp.int32, (g * SWA_TQ, 1), 0) >> 8
    lane_half = lax.broadcasted_iota(jnp.int32, (1, 128), 1) >> 6

    for kvh in range(SWA_KV_HEADS):
        heads = []
        for gi in range(g):
            hq = kvh * g + gi
            heads.append(qr[hq // 2][:, 64 * (hq % 2):64 * (hq % 2) + 64])
        q4 = jnp.concatenate(heads, axis=0).astype(BF16)
        sink = jnp.zeros((g * SWA_TQ, 1), F32)
        for gi in range(g):
            sink = jnp.where(rowc == gi, sink_ref[kvh * g + gi], sink)
        kh = kb[:, 64 * kvh:64 * kvh + 64]
        vh = jnp.where(lane_half == kvh, vb, jnp.ones_like(vb))
        s_loc = (_dot_nt(q4, kh).reshape(g, SWA_TQ, SWA_TK) + wmask[None]).reshape(g * SWA_TQ, SWA_TK)
        s_ctx = _dot_nt(q4, kc_ref[kvh].astype(BF16))
        m = jnp.maximum(jnp.max(jnp.maximum(s_loc, s_ctx), axis=-1, keepdims=True), sink)
        p_loc = jnp.exp(s_loc - m).astype(BF16)
        p_ctx = jnp.exp(s_ctx - m).astype(BF16)
        pv = _dot(p_loc, vh) + _dot(p_ctx, vc_ref[kvh].astype(BF16))
        lo, hi = 64 * kvh, 64 * (1 - kvh)
        den = pv[:, hi:hi + 1] + jnp.exp(sink - m)
        o = pv[:, lo:lo + 64] / den
        for gi in range(g):
            hq = kvh * g + gi
            o_ref[:, 64 * hq:64 * hq + 64] = o[SWA_TQ * gi:SWA_TQ * (gi + 1), :]


def _swa_value_ext(cache_v):
    ones = jnp.ones_like(cache_v[..., 0, :, :])
    return jnp.stack([jnp.concatenate([cache_v[..., 0, :, :], ones], axis=-1),
                      jnp.concatenate([ones, cache_v[..., 1, :, :]], axis=-1)], axis=-3)


def _swa_attn(p, cache_k, cache_v_ext, sink, cos, sin, e, o_full):
    assert SWA_TQ == 256
    lat0 = NT_CTX // SWA_TQ
    per_b = DEC_SEQ // SWA_TQ
    return pl.pallas_call(
        _swa_kernel,
        grid=(DEC_BATCH, per_b),
        in_specs=[
            pl.BlockSpec(memory_space=pltpu.SMEM),
            pl.BlockSpec((SWA_TQ, 512), lambda b, qb: (lat0 + b * per_b + qb, _QB // 512)),
            pl.BlockSpec((DEC_SEQ, 128), lambda b, qb: (1 + b, _KB // 128)),
            pl.BlockSpec((DEC_SEQ, 128), lambda b, qb: (1 + b, _VB // 128)),
            pl.BlockSpec((None, None, 2, PAST_LEN, HEAD_DIM), lambda b, qb: (b, e, 0, 0, 0)),
            pl.BlockSpec((None, None, 2, PAST_LEN, 128), lambda b, qb: (b, e, 0, 0, 0)),
            pl.BlockSpec((DEC_SEQ, 128), lambda b, qb: (0, 0)),
            pl.BlockSpec((DEC_SEQ, 128), lambda b, qb: (0, 0)),
            pl.BlockSpec((3, SWA_TQ, SWA_TK), lambda b, qb: (0, 0, 0)),
            pl.BlockSpec(memory_space=pl.ANY),
        ],
        out_specs=pl.BlockSpec((SWA_TQ, 512), lambda b, qb: (lat0 + b * per_b + qb, 1)),
        out_shape=jax.ShapeDtypeStruct((NT, D), F32),
        scratch_shapes=[pltpu.VMEM((DEC_SEQ, 128), BF16)],
        input_output_aliases={9: 0},
        compiler_params=_cp(("arbitrary", "arbitrary")),
        name="swa_attn",
    )(sink, p, p, p, cache_k, cache_v_ext, cos, sin, jnp.asarray(_swa_mask_table()), o_full)


CONV_TT = 256
CONV_HALO = 16
CONV_RC = 32


def _conv_kernel(u_ref, ul_ref, ur_ref, w_ref, cb_ref, lg_ref, lb_ref, o_ref, xs_ref):
    i = pl.program_id(0)
    n_ctx_tiles = NT_CTX // CONV_TT
    per_seq = DEC_SEQ // CONV_TT
    j = i - n_ctx_tiles
    is_ctx = i < n_ctx_tiles
    is_start = is_ctx | ((j % per_seq) == 0)
    is_end = is_ctx | ((j % per_seq) == per_seq - 1)

    def glu(u):
        return u[:, :CONV_CH] * _sigmoid(u[:, CONV_CH:])

    xs_ref[0, CONV_HALO:CONV_HALO + CONV_TT, :] = glu(u_ref[...])
    xs_ref[0, 0:CONV_HALO, :] = jnp.where(is_start, 0.0, glu(ul_ref[...]))
    xs_ref[0, CONV_HALO + CONV_TT:, :] = jnp.where(is_end, 0.0, glu(ur_ref[...]))
    n_sh = CONV_TT + 2 * CONV_HALO - 8
    for b in range(1, 8):
        xs_ref[b, 0:n_sh, :] = xs_ref[0, b:b + n_sh, :]

    pad = CONV_WIDTH // 2
    for c in range(CONV_TT // CONV_RC):
        base = CONV_HALO + c * CONV_RC - pad
        acc = jnp.zeros((CONV_RC, CONV_CH), F32)
        for t in range(CONV_WIDTH):
            off = base + t
            acc = acc + xs_ref[off % 8, off - off % 8:off - off % 8 + CONV_RC, :] * w_ref[t:t + 1, :]
        y = acc + cb_ref[...]
        mu = jnp.mean(y, axis=-1, keepdims=True)
        yc = y - mu
        var = jnp.mean(yc * yc, axis=-1, keepdims=True)
        yn = yc * lax.rsqrt(var + EPS) * lg_ref[...] + lb_ref[...]
        o_ref[c * CONV_RC:(c + 1) * CONV_RC, :] = yn * _sigmoid(yn)


def _conv_module(p, conv_w, conv_b, ln_g, ln_b):
    nh = CONV_TT // CONV_HALO
    last = NT // CONV_HALO - 1
    return pl.pallas_call(
        _conv_kernel,
        grid=(NT // CONV_TT,),
        in_specs=[
            pl.BlockSpec((CONV_TT, 2 * CONV_CH), lambda i: (i, 0)),
            pl.BlockSpec((CONV_HALO, 2 * CONV_CH), lambda i: (jnp.maximum(i * nh - 1, 0), 0)),
            pl.BlockSpec((CONV_HALO, 2 * CONV_CH), lambda i: (jnp.minimum((i + 1) * nh, last), 0)),
            pl.BlockSpec((CONV_WIDTH, CONV_CH), lambda i: (0, 0)),
            pl.BlockSpec((1, CONV_CH), lambda i: (0, 0)),
            pl.BlockSpec((1, CONV_CH), lambda i: (0, 0)),
            pl.BlockSpec((1, CONV_CH), lambda i: (0, 0)),
        ],
        out_specs=pl.BlockSpec((CONV_TT, CONV_CH), lambda i: (i, 0)),
        out_shape=jax.ShapeDtypeStruct((NT, D), F32),
        scratch_shapes=[pltpu.VMEM((8, CONV_TT + 2 * CONV_HALO, CONV_CH), F32)],
        compiler_params=_cp(("arbitrary",)),
        name="conv_module",
    )(p, p, p, conv_w, conv_b.reshape(1, -1), ln_g.reshape(1, -1), ln_b.reshape(1, -1))


_U, _HQ, _HI, _HFF, _HFB, _HGT = 0, 1024, 1536, 2048, 2560, 3072
HG_LEVELS = 6
HG_W = HG_HEADS * HG_DK


def _hgrn_constants():
    c = HG_CHUNK
    mm = np.zeros((HG_LEVELS + 1, c, c), np.float32)
    for lv in range(HG_LEVELS):
        n = c >> lv
        half = n // 2
        for t in range(c):
            for s in range(c):
                if t // n == s // n and t % n >= half and s % n < half:
                    mm[lv, t, s] = 1.0
    mm[HG_LEVELS] = np.eye(c, dtype=np.float32)
    tri = np.tril(np.ones((c, c), np.float32))
    return np.stack([tri, tri.T]), np.stack([mm, mm[:, ::-1, ::-1]])


def _hgrn_level_refs(cum, cum_ref, fwd):
    c = HG_CHUNK
    cum_ref[...] = cum
    refs = []
    for lv in range(HG_LEVELS - 2):
        n = c >> lv
        rows = [b * n + n // 2 - (1 if fwd else 0) for b in range(c // n)]
        refs.append(jnp.concatenate([jnp.broadcast_to(cum_ref[r:r + 1, :], (n, HG_W)) for r in rows], axis=0))
    row = lax.broadcasted_iota(jnp.int32, (c, HG_W), 0)
    for n in (4, 2):
        p = n // 2 - (1 if fwd else 0)
        res = row & (n - 1)
        ref = None
        for j in range(n):
            shift = (j - p) % c
            cand = cum if shift == 0 else pltpu.roll(cum, shift, 0)
            ref = cand if ref is None else jnp.where(res == j, cand, ref)
        refs.append(ref)
    return refs


def _hgrn_chunk(qs, f, v, lb, tri, masks, st_ref, cum_ref, fwd):
    c = HG_CHUNK
    g = jnp.log(lb + (1.0 - lb) * _sigmoid(f))
    k = (1.0 - lb) * _sigmoid(-f)
    g_hi = g.astype(BF16)
    r1 = g - g_hi.astype(F32)
    g_mid = r1.astype(BF16)
    g_lo = (r1 - g_mid.astype(F32)).astype(BF16)
    cum = _dot(tri, g_hi) + (_dot(tri, g_mid) + _dot(tri, g_lo))
    refs = _hgrn_level_refs(cum, cum_ref, fwd)
    end_row = c - 1 if fwd else 0
    ecum = jnp.exp(cum)
    eend = jnp.exp(-jnp.abs(cum - jnp.broadcast_to(cum_ref[end_row:end_row + 1, :], (c, HG_W))))
    dec = ecum[end_row:end_row + 1, :]
    qk = [((qs * el).astype(BF16), (k * el).astype(BF16))
          for el in (jnp.exp(-jnp.abs(cum - ref)) for ref in refs)]
    qb, kb, vb = qs.astype(BF16), k.astype(BF16), v.astype(BF16)
    qc = (qs * ecum).astype(BF16)
    kend = (k * eend).astype(BF16)
    outs = []
    for h in range(HG_HEADS):
        sl = slice(h * HG_DK, (h + 1) * HG_DK)
        a = masks[HG_LEVELS] * _dot_nt(qb[:, sl], kb[:, sl])
        for lv in range(HG_LEVELS):
            a = a + masks[lv] * _dot_nt(qk[lv][0][:, sl], qk[lv][1][:, sl])
        s_t = st_ref[h]
        outs.append(_dot(a.astype(BF16), vb[:, sl]) + _dot_nt(qc[:, sl], s_t.astype(BF16)))
        st_ref[h] = s_t * dec[:, sl] + _dot_tn(vb[:, sl], kend[:, sl])
    return jnp.concatenate(outs, axis=1)


def _hgrn_kernel(qf_ref, if_ref, ff_ref, qb_ref, ib_ref, fb_ref, lb_ref, s0_ref, tri_ref, m_ref,
                 of_ref, ob_ref, sfin_ref, stf_ref, stb_ref, cumf_ref, cumb_ref, *, tile):
    c = HG_CHUNK
    nct = tile // c
    tt = pl.program_id(1)

    @pl.when(tt == 0)
    def _():
        stf_ref[...] = s0_ref[0]
        stb_ref[...] = s0_ref[1]

    lbf = lb_ref[0:1, :]
    lbb = lb_ref[1:2, :]

    def body(ci, carry):
        rf = pl.multiple_of(ci * c, c)
        rb = pl.multiple_of((nct - 1 - ci) * c, c)
        q = qf_ref[pl.ds(rf, c), :]
        of_ref[pl.ds(rf, c), :] = _hgrn_chunk(
            q * _sigmoid(q), ff_ref[pl.ds(rf, c), :], if_ref[pl.ds(rf, c), :], lbf, tri_ref[0],
            [m_ref[0, x] for x in range(HG_LEVELS + 1)], stf_ref, cumf_ref, True)
        q = qb_ref[pl.ds(rb, c), :]
        ob_ref[pl.ds(rb, c), :] = _hgrn_chunk(
            q * _sigmoid(q), fb_ref[pl.ds(rb, c), :], ib_ref[pl.ds(rb, c), :], lbb, tri_ref[1],
            [m_ref[1, x] for x in range(HG_LEVELS + 1)], stb_ref, cumb_ref, False)
        return carry

    lax.fori_loop(0, nct, body, 0)

    @pl.when(tt == pl.num_programs(1) - 1)
    def _():
        sfin_ref[0] = stf_ref[...]
        sfin_ref[1] = stb_ref[...]


def _hgrn_scan(p, lb, s0_t, n_batch, t_len, row0):
    tri, mm = _hgrn_constants()
    tri = jnp.asarray(tri, BF16)
    mm = jnp.asarray(mm, F32)
    tile = min(t_len, 512)
    ntt = t_len // tile
    r0 = row0 // tile

    def fwd(c0):
        return pl.BlockSpec((tile, HG_W), lambda b, t: (r0 + b * ntt + t, c0 // HG_W))

    def bwd(c0):
        return pl.BlockSpec((tile, HG_W), lambda b, t: (r0 + b * ntt + ntt - 1 - t, c0 // HG_W))

    st_spec = pl.BlockSpec((None, 2, HG_HEADS, 128, 128), lambda b, t: (b, 0, 0, 0, 0))
    n_rows = n_batch * t_len
    return pl.pallas_call(
        functools.partial(_hgrn_kernel, tile=tile),
        grid=(n_batch, ntt),
        in_specs=[
            fwd(_HQ), fwd(_HI), fwd(_HFF), bwd(_HQ), bwd(_HI), bwd(_HFB),
            pl.BlockSpec((2, HG_W), lambda b, t: (0, 0)),
            st_spec,
            pl.BlockSpec((2, HG_CHUNK, HG_CHUNK), lambda b, t: (0, 0, 0)),
            pl.BlockSpec((2, HG_LEVELS + 1, HG_CHUNK, HG_CHUNK), lambda b, t: (0, 0, 0, 0)),
        ],
        out_specs=[
            pl.BlockSpec((tile, HG_W), lambda b, t: (b * ntt + t, 0)),
            pl.BlockSpec((tile, HG_W), lambda b, t: (b * ntt + ntt - 1 - t, 0)),
            st_spec,
        ],
        out_shape=[jax.ShapeDtypeStruct((n_rows, HG_W), F32),
                   jax.ShapeDtypeStruct((n_rows, HG_W), F32),
                   jax.ShapeDtypeStruct((n_batch, 2, HG_HEADS, 128, 128), F32)],
        scratch_shapes=[pltpu.VMEM((HG_HEADS, 128, 128), F32), pltpu.VMEM((HG_HEADS, 128, 128), F32),
                        pltpu.VMEM((HG_CHUNK, HG_W), F32), pltpu.VMEM((HG_CHUNK, HG_W), F32)],
        compiler_params=_cp(("arbitrary", "arbitrary")),
        name="hgrn_scan",
    )(p, p, p, p, p, p, lb, s0_t, tri, mm)


def _hgrn_fin_kernel(of_ref, ob_ref, gt_ref, ng_ref, oin_ref, o_ref):
    del oin_ref
    od = of_ref[...] + ob_ref[...]
    gt = gt_ref[...]
    gate = gt * _sigmoid(gt)
    for h in range(HG_HEADS):
        sl = slice(h * HG_DK, (h + 1) * HG_DK)
        x = od[:, sl]
        ms = jnp.mean(x * x, axis=-1, keepdims=True)
        o_ref[:, sl] = x * lax.rsqrt(ms + EPS) * ng_ref[...] * gate[:, sl]


def _hgrn_finish(o_f, o_b, p, norm_g, o_full, row0, tm=512):
    n_rows = o_f.shape[0]
    r0 = row0 // tm
    return pl.pallas_call(
        _hgrn_fin_kernel,
        grid=(n_rows // tm,),
        in_specs=[
            pl.BlockSpec((tm, HG_W), lambda i: (i, 0)),
            pl.BlockSpec((tm, HG_W), lambda i: (i, 0)),
            pl.BlockSpec((tm, HG_W), lambda i: (r0 + i, _HGT // HG_W)),
            pl.BlockSpec((1, 128), lambda i: (0, 0)),
            pl.BlockSpec(memory_space=pl.ANY),
        ],
        out_specs=pl.BlockSpec((tm, HG_W), lambda i: (r0 + i, 1)),
        out_shape=jax.ShapeDtypeStruct((NT, D), F32),
        input_output_aliases={4: 0},
        compiler_params=_cp(("arbitrary",)),
        name="hgrn_finish",
    )(o_f, o_b, p, norm_g.reshape(1, 128), o_full)


def _hgrn(p, lb, norm_g, s0_t, o_full, n_batch, t_len, row0):
    o_f, o_b, s_fin = _hgrn_scan(p, lb, s0_t, n_batch, t_len, row0)
    return _hgrn_finish(o_f, o_b, p, norm_g, o_full, row0), s_fin


ROUTE_TM = 512


def _router_kernel(x_ref, g_ref, sh_ref, sc_ref, whi_ref, wlo_ref, br_ref, tri_ref,
                   h_ref, mi_ref, mf_ref, cnt_ref, carry_ref):
    i = pl.program_id(0)

    @pl.when(i == 0)
    def _():
        carry_ref[...] = jnp.zeros_like(carry_ref)

    h = _modulate(x_ref[...], g_ref[...], sh_ref[...], sc_ref[...])
    h_ref[...] = h
    logits = _dot3(h, whi_ref[...], wlo_ref[...]) + br_ref[...]
    lane = lax.broadcasted_iota(jnp.int32, logits.shape, 1)
    neg = jnp.float32(-3e38)
    is_g = lane < N_GROUPS
    lg = jnp.where(is_g, logits, neg)
    mg = jnp.max(lg, axis=-1, keepdims=True)
    grp = jnp.min(jnp.where(lg == mg, lane, 128), axis=-1, keepdims=True)
    pg = 1.0 / jnp.sum(jnp.where(is_g, jnp.exp(lg - mg), 0.0), axis=-1, keepdims=True)
    ex = lane - N_GROUPS
    in_grp = (ex >= 0) & (ex < N_EXPERTS) & ((ex >> 3) == grp)
    le = jnp.where(in_grp, logits, neg)
    v1 = jnp.max(le, axis=-1, keepdims=True)
    i1 = jnp.min(jnp.where(le == v1, lane, 128), axis=-1, keepdims=True)
    le2 = jnp.where(lane == i1, neg, le)
    v2 = jnp.max(le2, axis=-1, keepdims=True)
    i2 = jnp.min(jnp.where(le2 == v2, lane, 128), axis=-1, keepdims=True)
    t = jnp.exp(v2 - v1)
    g1 = pg / (1.0 + t)
    g2 = pg * t / (1.0 + t)
    oh = jnp.where((lane == i1) | (lane == i2), 1.0, 0.0)
    prefix = _dot(tri_ref[...], oh.astype(BF16)) + carry_ref[...]
    r1 = jnp.sum(jnp.where(lane == i1, prefix, 0.0), axis=-1, keepdims=True).astype(jnp.int32)
    r2 = jnp.sum(jnp.where(lane == i2, prefix, 0.0), axis=-1, keepdims=True).astype(jnp.int32)
    carry_ref[...] = carry_ref[...] + jnp.sum(oh, axis=0, keepdims=True)
    cnt_ref[...] = carry_ref[...]
    mi_ref[...] = jnp.where(lane == 0, i1 - N_GROUPS, jnp.where(lane == 1, i2 - N_GROUPS,
                            jnp.where(lane == 2, r1, jnp.where(lane == 3, r2, 0))))
    mf_ref[...] = jnp.where(lane == 0, g1, jnp.where(lane == 1, g2, 0.0))


def _router(x, g, mod, l, wg, bg, we, be):
    tm = ROUTE_TM
    wr = jnp.zeros((D, 128), F32).at[:, :N_GROUPS].set(wg).at[:, N_GROUPS:N_GROUPS + N_EXPERTS].set(we)
    br = jnp.zeros((1, 128), F32).at[0, :N_GROUPS].set(bg).at[0, N_GROUPS:N_GROUPS + N_EXPERTS].set(be)
    w_hi = wr.astype(BF16)
    w_lo = (wr - w_hi.astype(F32)).astype(BF16)
    tri = jnp.asarray(np.tril(np.ones((tm, tm), np.float32), -1), BF16)
    const = lambda i: (0, 0)
    return pl.pallas_call(
        _router_kernel,
        grid=(NT // tm,),
        in_specs=[
            pl.BlockSpec((tm, D), lambda i: (i, 0)),
            pl.BlockSpec((1, D), const),
            _mod_spec(l, 3, tm),
            _mod_spec(l, 4, tm),
            pl.BlockSpec((D, 128), const),
            pl.BlockSpec((D, 128), const),
            pl.BlockSpec((1, 128), const),
            pl.BlockSpec((tm, tm), const),
        ],
        out_specs=[
            pl.BlockSpec((tm, D), lambda i: (i, 0)),
            pl.BlockSpec((tm, 128), lambda i: (i, 0)),
            pl.BlockSpec((tm, 128), lambda i: (i, 0)),
            pl.BlockSpec((1, 128), const),
        ],
        out_shape=[
            jax.ShapeDtypeStruct((NT, D), F32),
            jax.ShapeDtypeStruct((NT, 128), jnp.int32),
            jax.ShapeDtypeStruct((NT, 128), F32),
            jax.ShapeDtypeStruct((1, 128), F32),
        ],
        scratch_shapes=[pltpu.VMEM((1, 128), F32)],
        compiler_params=_cp(("arbitrary",)),
        name="router",
    )(x, g.reshape(1, D), mod, mod, w_hi, w_lo, br, tri)


DISP_TM = 512


def _row_copy(src, s_row, dst, d_row, sem):
    return pltpu.make_async_copy(src.at[pl.ds(s_row, 1)], dst.at[pl.ds(d_row, 1)], sem)


def _dispatch_kernel(dest_ref, h_ref, xs_hbm, sem):
    def group(t8, start):
        base = pl.multiple_of(t8 * 8, 8)
        for s in range(8):
            for k in range(2):
                cp = _row_copy(h_ref, base + s, xs_hbm, dest_ref[k, base + s], sem)
                if start:
                    cp.start(priority=k)
                else:
                    cp.wait()

    lax.fori_loop(0, DISP_TM // 8, lambda t8, c: (group(t8, True), c)[1], 0)
    lax.fori_loop(0, DISP_TM // 8, lambda t8, c: (group(t8, False), c)[1], 0)


def _dispatch(dest_t, h):
    return pl.pallas_call(
        _dispatch_kernel,
        grid=(NT // DISP_TM,),
        in_specs=[
            pl.BlockSpec((2, DISP_TM), lambda i: (0, i), memory_space=pltpu.SMEM),
            pl.BlockSpec((DISP_TM, D), lambda i: (i, 0)),
        ],
        out_specs=pl.BlockSpec(memory_space=pl.ANY),
        out_shape=jax.ShapeDtypeStruct((MOE_SLOTS, D), F32),
        scratch_shapes=[pltpu.SemaphoreType.DMA(())],
        compiler_params=_cp(("arbitrary",)),
        name="moe_dispatch",
    )(dest_t, h)


def _expert_kernel(bexp_ref, bval_ref, bfirst_ref, xs_ref, w1_ref, w3_ref, w2_ref, y_ref,
                   w1b_ref, w3b_ref, w2b_ref):
    del bexp_ref
    i = pl.program_id(0)
    nv = bval_ref[i]

    @pl.when(bfirst_ref[i] == 1)
    def _():
        w1b_ref[...] = w1_ref[...].astype(BF16)
        w3b_ref[...] = w3_ref[...].astype(BF16)
        w2b_ref[...] = w2_ref[...].astype(BF16)

    @pl.when(nv > 0)
    def _():
        row = lax.broadcasted_iota(jnp.int32, (MOE_MB, 1), 0)
        xb = jnp.where(row < nv, xs_ref[...], 0.0).astype(BF16)
        a = _dot(xb, w1b_ref[...])
        b = _dot(xb, w3b_ref[...])
        hid = (a * _sigmoid(a) * b).astype(BF16)
        y_ref[...] = _dot(hid, w2b_ref[...])

    @pl.when(nv <= 0)
    def _():
        y_ref[...] = jnp.zeros_like(y_ref)


def _experts(blk_exp, blk_valid, blk_first, xs, w1, w3, w2, l):
    grid_spec = pltpu.PrefetchScalarGridSpec(
        num_scalar_prefetch=3,
        grid=(MOE_NB,),
        in_specs=[
            pl.BlockSpec((MOE_MB, D), lambda i, be, bv, bf: (i, 0)),
            pl.BlockSpec((None, None, D, D_EXPERT), lambda i, be, bv, bf: (l, be[i], 0, 0)),
            pl.BlockSpec((None, None, D, D_EXPERT), lambda i, be, bv, bf: (l, be[i], 0, 0)),
            pl.BlockSpec((None, None, D_EXPERT, D), lambda i, be, bv, bf: (l, be[i], 0, 0)),
        ],
        out_specs=pl.BlockSpec((MOE_MB, D), lambda i, be, bv, bf: (i, 0)),
        scratch_shapes=[pltpu.VMEM((D, D_EXPERT), BF16), pltpu.VMEM((D, D_EXPERT), BF16),
                        pltpu.VMEM((D_EXPERT, D), BF16)],
    )
    return pl.pallas_call(
        _expert_kernel,
        grid_spec=grid_spec,
        out_shape=jax.ShapeDtypeStruct((MOE_SLOTS, D), F32),
        compiler_params=_cp(("arbitrary",)),
        name="moe_experts",
    )(blk_exp, blk_valid, blk_first, xs, w1, w3, w2)


COMB_TM = 256


def _combine_kernel(dest_ref, mf_ref, x_ref, gate_ref, y_hbm, o_ref, buf_ref, sem):
    def group(t8, start):
        base = pl.multiple_of(t8 * 8, 8)
        for s in range(8):
            for k in range(2):
                cp = _row_copy(y_hbm, dest_ref[k, base + s], buf_ref.at[k], base + s, sem)
                if start:
                    cp.start(priority=k)
                else:
                    cp.wait()

    lax.fori_loop(0, COMB_TM // 8, lambda t8, c: (group(t8, True), c)[1], 0)
    lax.fori_loop(0, COMB_TM // 8, lambda t8, c: (group(t8, False), c)[1], 0)
    mf = mf_ref[...]
    y = mf[:, 0:1] * buf_ref[0] + mf[:, 1:2] * buf_ref[1]
    o_ref[...] = x_ref[...] + gate_ref[...] * y


def _combine(dest_t, mf, x, mod, l, y):
    tm = COMB_TM
    return pl.pallas_call(
        _combine_kernel,
        grid=(NT // tm,),
        in_specs=[
            pl.BlockSpec((2, tm), lambda i: (0, i), memory_space=pltpu.SMEM),
            pl.BlockSpec((tm, 128), lambda i: (i, 0)),
            pl.BlockSpec((tm, D), lambda i: (i, 0)),
            _mod_spec(l, 5, tm),
            pl.BlockSpec(memory_space=pl.ANY),
        ],
        out_specs=pl.BlockSpec((tm, D), lambda i: (i, 0)),
        out_shape=jax.ShapeDtypeStruct((NT, D), F32),
        scratch_shapes=[pltpu.VMEM((2, tm, D), F32), pltpu.SemaphoreType.DMA(())],
        compiler_params=_cp(("arbitrary",)),
        name="moe_combine",
    )(dest_t, mf, x, mod, y)


def _comb_modmm_kernel(dcur_ref, dnxt_ref, mf_ref, x_ref, gate_ref, g_ref, sh_ref, sc_ref, w_ref, y_hbm,
                       xo_ref, p_ref, buf_ref, sem, *, nchunk):
    i = pl.program_id(0)
    n = pl.num_programs(0)
    slot = lax.rem(i, 2)
    tm = x_ref.shape[0]

    def rows(dest_ref, to_slot, lo, hi, start):
        for t in range(lo, hi):
            for k in range(2):
                cp = _row_copy(y_hbm, dest_ref[k, t], buf_ref.at[to_slot, k], t, sem.at[to_slot])
                if start:
                    cp.start(priority=k)
                else:
                    cp.wait()

    def rolled(dest_ref, to_slot, start):
        def body(t8, c):
            base = pl.multiple_of(t8 * 8, 8)
            for s in range(8):
                for k in range(2):
                    cp = _row_copy(y_hbm, dest_ref[k, base + s], buf_ref.at[to_slot, k], base + s,
                                   sem.at[to_slot])
                    if start:
                        cp.start(priority=k)
                    else:
                        cp.wait()
            return c
        lax.fori_loop(0, tm // 8, body, 0)

    @pl.when(i == 0)
    def _():
        rolled(dcur_ref, slot, True)

    rolled(dcur_ref, slot, False)
    mf = mf_ref[...]
    y = mf[:, 0:1] * buf_ref[slot, 0] + mf[:, 1:2] * buf_ref[slot, 1]
    xn = x_ref[...] + gate_ref[...] * y
    xo_ref[...] = xn
    h = _modulate(xn, g_ref[...], sh_ref[...], sc_ref[...]).astype(BF16)
    nout = p_ref.shape[1]
    n_ch = nout // nchunk
    per = -(-tm // n_ch)
    for c in range(n_ch):
        rows(dnxt_ref, 1 - slot, min(c * per, tm), min((c + 1) * per, tm), True)
        p_ref[:, c * nchunk:(c + 1) * nchunk] = _dot(h, w_ref[:, c * nchunk:(c + 1) * nchunk])

    @pl.when(i == n - 1)
    def _():
        rolled(dnxt_ref, 1 - slot, False)


def _comb_modmm(dest_t, mf, x, mod, l, y, g_next, w_bf16, tm=256):
    nout = w_bf16.shape[1]
    nt = NT // tm
    return pl.pallas_call(
        functools.partial(_comb_modmm_kernel, nchunk=256),
        grid=(nt,),
        in_specs=[
            pl.BlockSpec((2, tm), lambda i: (0, i), memory_space=pltpu.SMEM),
            pl.BlockSpec((2, tm), lambda i: (0, jnp.minimum(i + 1, nt - 1)), memory_space=pltpu.SMEM),
            pl.BlockSpec((tm, 128), lambda i: (i, 0)),
            pl.BlockSpec((tm, D), lambda i: (i, 0)),
            _mod_spec(l, 5, tm),
            pl.BlockSpec((1, D), lambda i: (0, 0)),
            _mod_spec(l + 1, 0, tm),
            _mod_spec(l + 1, 1, tm),
            pl.BlockSpec((D, nout), lambda i: (0, 0)),
            pl.BlockSpec(memory_space=pl.ANY),
        ],
        out_specs=[pl.BlockSpec((tm, D), lambda i: (i, 0)), pl.BlockSpec((tm, nout), lambda i: (i, 0))],
        out_shape=[jax.ShapeDtypeStruct((NT, D), F32), jax.ShapeDtypeStruct((NT, nout), F32)],
        scratch_shapes=[pltpu.VMEM((2, 2, tm, D), F32), pltpu.SemaphoreType.DMA((2,))],
        compiler_params=_cp(("arbitrary",)),
        name="moe_combine_modmm",
    )(dest_t, dest_t, mf, x, mod, g_next.reshape(1, D), mod, mod, w_bf16, y)


def _moe_plan(mi, counts):
    cnt = counts[0, N_GROUPS:N_GROUPS + N_EXPERTS].astype(jnp.int32)
    padded = (cnt + MOE_MB - 1) // MOE_MB * MOE_MB
    pad_ends = jnp.cumsum(padded)
    pad_starts = pad_ends - padded
    dest = pad_starts[mi[:, 0:2]] + mi[:, 2:4]
    blk0 = jnp.arange(MOE_NB, dtype=jnp.int32) * MOE_MB
    blk_exp = jnp.sum((pad_ends[None, :] <= blk0[:, None]).astype(jnp.int32), axis=1)
    blk_exp = jnp.minimum(blk_exp, N_EXPERTS - 1)
    blk_valid = jnp.clip(cnt[blk_exp] - (blk0 - pad_starts[blk_exp]), 0, MOE_MB)
    blk_valid = jnp.where(blk0 < pad_ends[-1], blk_valid, 0).astype(jnp.int32)
    blk_first = jnp.concatenate([jnp.ones((1,), jnp.int32),
                                 (blk_exp[1:] != blk_exp[:-1]).astype(jnp.int32)])
    return dest.T.astype(jnp.int32), blk_exp, blk_valid, blk_first


def _moe_experts(x, g, mod, l, wg, bg, we, be, w1, w3, w2):
    h, mi, mf, counts = _router(x, g, mod, l, wg, bg, we, be)
    dest_t, blk_exp, blk_valid, blk_first = _moe_plan(mi, counts)
    xs = _dispatch(dest_t, h)
    y = _experts(blk_exp, blk_valid, blk_first, xs, w1, w3, w2, l)
    return dest_t, mf, y


def kernel(x_prompt, x_sample, cache_k_na, cache_v_na, cache_k_swa, cache_v_swa, state_hgrn, c, c_ctx, w_mod, b_mod, norm_mix_g, norm_ffn_g, w_in_even, w_out_even, na_rel_bias, swa_sink, w_in_odd, w_out_odd, conv_w, conv_b, conv_norm_g, conv_norm_b, hgrn_lb_raw, hgrn_norm_g, router_group_w, router_group_b, router_expert_w, router_expert_b, moe_w1, moe_w3, moe_w2, final_norm_g):
    x = jnp.concatenate([x_prompt.reshape(NT_CTX, D), x_sample.reshape(NT_LAT, D)], axis=0)
    cvec = jnp.zeros((16, D), F32).at[0].set(c_ctx).at[1:1 + DEC_BATCH].set(c)
    mod = _mod_table(cvec, w_mod, b_mod)

    lb_p = jax.nn.softmax(hgrn_lb_raw.astype(F32), axis=0)
    lower_bounds = jnp.cumsum(lb_p, axis=0) - lb_p[0:1]
    cos, sin = _rope_tables()
    cache_v_swa_ext = _swa_value_ext(cache_v_swa)

    k_na, v_na, k_swa, v_swa, s_hg = [], [], [], [], []
    moe = None
    for l in range(DEPTH):
        w_in = (w_in_even if l % 2 == 0 else w_in_odd)[l // 2].astype(BF16)
        if moe is None:
            p = _modmm(x, norm_mix_g[l], mod, l, w_in, tm=512)
        else:
            x, p = _comb_modmm(*moe[:2], x, mod, l - 1, moe[2], norm_mix_g[l], w_in)
        if l % 2 == 0:
            e = l // 2
            o = _ctx_attn(p, swa_sink[e])
            o = _na_attn(p, cache_k_na, cache_v_na, _na_bias_table(na_rel_bias[e]), e, o)
            o = _swa_attn(p, cache_k_swa, cache_v_swa_ext, swa_sink[e], cos, sin, e, o)
            pc = p[:NT_CTX]
            to_heads = lambda u, nh: u.reshape(BATCH, SEQ, nh, HEAD_DIM).transpose(0, 2, 1, 3)
            k_na.append(to_heads(pc[:, _KA:_KA + 512], NA_HEADS))
            v_na.append(to_heads(pc[:, _VA:_VA + 512], NA_HEADS))
            k_swa.append(to_heads(pc[:, _KB:_KB + 128], SWA_KV_HEADS))
            v_swa.append(to_heads(pc[:, _VB:_VB + 128], SWA_KV_HEADS))
            w_out = w_out_even[e]
        else:
            j = l // 2
            o = _conv_module(p, conv_w[j], conv_b[j], conv_norm_g[j], conv_norm_b[j])
            zero_state = jnp.zeros((BATCH, 2, HG_HEADS, 128, 128), F32)
            o, s_ctx = _hgrn(p, lower_bounds[j], hgrn_norm_g[j], zero_state, o, BATCH, SEQ, 0)
            s0_lat = jnp.swapaxes(state_hgrn[:, j], -1, -2)
            o, _ = _hgrn(p, lower_bounds[j], hgrn_norm_g[j], s0_lat, o, DEC_BATCH, DEC_SEQ, NT_CTX)
            s_hg.append(jnp.swapaxes(s_ctx, -1, -2))
            w_out = w_out_odd[j]
        x = _mmres(o, x, mod, l, w_out.astype(BF16))
        moe = _moe_experts(x, norm_ffn_g[l], mod, l, router_group_w[l], router_group_b[l],
                           router_expert_w[l], router_expert_b[l], moe_w1, moe_w3, moe_w2)

    x = _combine(*moe[:2], x, mod, DEPTH - 1, moe[2])
    y_prompt = _final_norm(x, final_norm_g, 0, NT_CTX).reshape(BATCH, SEQ, D)
    y_sample = _final_norm(x, final_norm_g, NT_CTX, NT_LAT).reshape(DEC_BATCH, DEC_SEQ, D)
    return (y_prompt, y_sample, jnp.stack(k_na, axis=1), jnp.stack(v_na, axis=1),
            jnp.stack(k_swa, axis=1), jnp.stack(v_swa, axis=1), jnp.stack(s_hg, axis=1))
```

```python
import functools

import numpy as np
import jax
import jax.numpy as jnp
from jax import lax
from jax.experimental import pallas as pl
from jax.experimental.pallas import tpu as pltpu

F32 = jnp.float32
BF16 = jnp.bfloat16

D = 1024
BATCH = 16
SEQ = 256
DEPTH = 4
DEC_BATCH = 8
DEC_SEQ = 4096
PAST_LEN = 512
GRID_W = 64
HEAD_DIM = 64
EPS = 1e-6
NA_HEADS = 8
NA_WIN_R = 8
NA_WIN_C = 16
SWA_HEADS = 8
SWA_KV_HEADS = 2
SWA_WINDOW = 128
ROPE_BASE = 10000.0
CONV_CH = 512
CONV_WIDTH = 31
HG_HEADS = 4
HG_DK = 128
HG_CHUNK = 64
N_GROUPS = 4
EXPERTS_PER_GROUP = 8
N_EXPERTS = 32
D_EXPERT = 512
N_EVEN = 2
N_ODD = 2
D_IN_EVEN = 2304
D_IN_ODD = 3584

NT_CTX = BATCH * SEQ
NT_LAT = DEC_BATCH * DEC_SEQ
NT = NT_CTX + NT_LAT
SEG = 4096
assert NT_CTX == SEG and DEC_SEQ == SEG
N_MOD = 1 + DEC_BATCH
MASK = -1e30

MOE_MB = 512
MOE_NB = (2 * NT) // MOE_MB + N_EXPERTS
MOE_SLOTS = MOE_NB * MOE_MB

VMEM_LIMIT = 56 * 1024 * 1024


def _cp(sem, vmem=VMEM_LIMIT):
    return pltpu.CompilerParams(dimension_semantics=sem, vmem_limit_bytes=vmem)


def _dot(a, b):
    return jnp.dot(a, b, preferred_element_type=F32)


def _dot_nt(a, b):
    return lax.dot_general(a, b, (((1,), (1,)), ((), ())), preferred_element_type=F32)


def _dot_tn(a, b):
    return lax.dot_general(a, b, (((0,), (0,)), ((), ())), preferred_element_type=F32)


def _split2(a):
    hi = a.astype(BF16)
    lo = (a - hi.astype(F32)).astype(BF16)
    return hi, lo


def _dot3(a, b_hi, b_lo):
    a_hi, a_lo = _split2(a)
    return _dot(a_hi, b_hi) + (_dot(a_hi, b_lo) + _dot(a_lo, b_hi))


def _sigmoid(x):
    return 1.0 / (1.0 + jnp.exp(-x))


def _modulate(x, g, shift, scale):
    ms = jnp.mean(x * x, axis=-1, keepdims=True)
    return (x * lax.rsqrt(ms + EPS) * g) * (1.0 + scale) + shift


def _mod_kernel(c_ref, whi_ref, wlo_ref, b_ref, o_ref):
    cv = c_ref[...]
    s = cv * _sigmoid(cv)
    o_ref[...] = _dot3(s, whi_ref[...], wlo_ref[...]) + b_ref[...]


def _mod_table(cvec, w_mod, b_mod):
    tn = 1536
    w_hi = w_mod.astype(BF16)
    w_lo = (w_mod - w_hi.astype(F32)).astype(BF16)
    out = pl.pallas_call(
        _mod_kernel,
        grid=(DEPTH, 6 * D // tn),
        in_specs=[
            pl.BlockSpec((16, D), lambda l, j: (0, 0)),
            pl.BlockSpec((None, D, tn), lambda l, j: (l, 0, j)),
            pl.BlockSpec((None, D, tn), lambda l, j: (l, 0, j)),
            pl.BlockSpec((None, 1, tn), lambda l, j: (l, 0, j)),
        ],
        out_specs=pl.BlockSpec((None, 16, tn), lambda l, j: (l, 0, j)),
        out_shape=jax.ShapeDtypeStruct((DEPTH, 16, 6 * D), F32),
        compiler_params=_cp(("arbitrary", "arbitrary")),
        name="mod_table",
    )(cvec, w_hi, w_lo, b_mod.reshape(DEPTH, 1, 6 * D))
    return out.reshape(DEPTH, 16, 1, 6 * D)


def _mod_spec(l, which, tm):
    return pl.BlockSpec((None, None, 1, D), lambda i, *_: (l, (i * tm) // SEG, 0, which))


def _modmm_kernel(x_ref, g_ref, sh_ref, sc_ref, w_ref, o_ref, *, nchunk):
    h = _modulate(x_ref[...], g_ref[...], sh_ref[...], sc_ref[...]).astype(BF16)
    n = o_ref.shape[1]
    for n0 in range(0, n, nchunk):
        o_ref[:, n0:n0 + nchunk] = _dot(h, w_ref[:, n0:n0 + nchunk])


def _modmm(x, g, mod, l, w_bf16, tm):
    n = w_bf16.shape[1]
    return pl.pallas_call(
        functools.partial(_modmm_kernel, nchunk=256),
        grid=(NT // tm,),
        in_specs=[
            pl.BlockSpec((tm, D), lambda i: (i, 0)),
            pl.BlockSpec((1, D), lambda i: (0, 0)),
            _mod_spec(l, 0, tm),
            _mod_spec(l, 1, tm),
            pl.BlockSpec((D, n), lambda i: (0, 0)),
        ],
        out_specs=pl.BlockSpec((tm, n), lambda i: (i, 0)),
        out_shape=jax.ShapeDtypeStruct((NT, n), F32),
        compiler_params=_cp(("arbitrary",)),
        name="modmm",
    )(x, g.reshape(1, D), mod, mod, w_bf16)


def _final_kernel(x_ref, g_ref, o_ref):
    x = x_ref[...]
    ms = jnp.mean(x * x, axis=-1, keepdims=True)
    o_ref[...] = x * lax.rsqrt(ms + EPS) * g_ref[...]


def _final_norm(x, g, row0, n_rows, tm=512):
    r0 = row0 // tm
    return pl.pallas_call(
        _final_kernel,
        grid=(n_rows // tm,),
        in_specs=[pl.BlockSpec((tm, D), lambda i: (r0 + i, 0)), pl.BlockSpec((1, D), lambda i: (0, 0))],
        out_specs=pl.BlockSpec((tm, D), lambda i: (i, 0)),
        out_shape=jax.ShapeDtypeStruct((n_rows, D), F32),
        compiler_params=_cp(("arbitrary",)),
        name="final_norm",
    )(x, g.reshape(1, D))


_QA, _KA, _VA, _QB, _KB, _VB = 0, 512, 1024, 1536, 2048, 2176


def _ctx_attn_kernel(sink_ref, p_ref, o_ref):
    scale = HEAD_DIM ** -0.5

    def head(qc, kc, vc, sink):
        q = p_ref[:, qc:qc + 64].astype(BF16)
        k = p_ref[:, kc:kc + 64].astype(BF16)
        v = p_ref[:, vc:vc + 64].astype(BF16)
        s = _dot_nt(q, k) * scale
        m = jnp.max(s, axis=-1, keepdims=True)
        if sink is not None:
            m = jnp.maximum(m, sink)
        p = jnp.exp(s - m)
        den = jnp.sum(p, axis=-1, keepdims=True)
        if sink is not None:
            den = den + jnp.exp(sink - m)
        return _dot(p.astype(BF16), v) / den

    for h in range(NA_HEADS):
        o_ref[:, 64 * h:64 * h + 64] = head(_QA + 64 * h, _KA + 64 * h, _VA + 64 * h, None)
    for j in range(SWA_HEADS):
        kv = j // (SWA_HEADS // SWA_KV_HEADS)
        o_ref[:, 512 + 64 * j:512 + 64 * j + 64] = head(_QB + 64 * j, _KB + 64 * kv, _VB + 64 * kv, sink_ref[j])


def _ctx_attn(p, sink):
    return pl.pallas_call(
        _ctx_attn_kernel,
        grid=(BATCH,),
        in_specs=[
            pl.BlockSpec(memory_space=pltpu.SMEM),
            pl.BlockSpec((SEQ, D_IN_EVEN), lambda b: (b, 0)),
        ],
        out_specs=pl.BlockSpec((SEQ, D), lambda b: (b, 0)),
        out_shape=jax.ShapeDtypeStruct((NT, D), F32),
        compiler_params=_cp(("arbitrary",)),
        name="ctx_attn",
    )(sink, p)


NA_QROWS = 8
NA_BAND = 16
NA_TQ = NA_QROWS * GRID_W
NA_TK = NA_BAND * GRID_W
NA_RB = (DEC_SEQ // GRID_W) // NA_QROWS


def _na_band_start(rb):
    rows = DEC_SEQ // GRID_W
    return int(np.clip(NA_QROWS * rb - NA_WIN_R // 2, 0, rows - NA_BAND))


def _na_row_index():
    rows = DEC_SEQ // GRID_W
    n_dr = 2 * NA_WIN_R - 1
    out = np.full((3, NA_QROWS, NA_BAND), n_dr, np.int32)
    for ci, rb in enumerate((0, 1, NA_RB - 1)):
        for ql in range(NA_QROWS):
            qrow = NA_QROWS * rb + ql
            r0 = int(np.clip(qrow - NA_WIN_R // 2, 0, rows - NA_WIN_R))
            for kl in range(NA_BAND):
                krow = _na_band_start(rb) + kl
                if r0 <= krow < r0 + NA_WIN_R:
                    out[ci, ql, kl] = krow - qrow + (NA_WIN_R - 1)
    return out


def _na_bias_table(rel_bias):
    h = rel_bias.shape[0]
    n_dr, n_dc = 2 * NA_WIN_R - 1, 2 * NA_WIN_C - 1
    cols = np.arange(GRID_W)
    c0 = np.clip(cols - NA_WIN_C // 2, 0, GRID_W - NA_WIN_C)
    vc = (cols[None, :] >= c0[:, None]) & (cols[None, :] < c0[:, None] + NA_WIN_C)
    dc = cols[None, :] - cols[:, None] + (NA_WIN_C - 1)
    sel = ((np.arange(n_dc)[:, None, None] == dc[None]) & vc[None]).astype(np.float32)
    bc = jnp.dot(rel_bias.reshape(h * n_dr, n_dc), jnp.asarray(sel.reshape(n_dc, -1)),
                 precision=lax.Precision.HIGHEST).reshape(h, n_dr, GRID_W, GRID_W)
    bc = jnp.where(jnp.asarray(vc), bc, MASK)
    bc = jnp.concatenate([bc, jnp.full((h, 1, GRID_W, GRID_W), MASK, F32)], axis=1)
    tab = jnp.take(bc, jnp.asarray(_na_row_index()), axis=1)
    return tab.transpose(0, 1, 2, 4, 3, 5).reshape(h // 2, 2, 3, NA_TQ, NA_TK)


def _na_kernel(q_ref, k_ref, v_ref, kc_ref, vc_ref, bias_ref, oin_ref, o_ref):
    del oin_ref
    rb = pl.program_id(2)
    scale = HEAD_DIM ** -0.5
    start = jnp.clip(NA_QROWS * rb - NA_WIN_R // 2, 0, DEC_SEQ // GRID_W - NA_BAND) * GRID_W
    start = pl.multiple_of(start, 256)
    case = jnp.where(rb == 0, 0, jnp.where(rb == NA_RB - 1, 2, 1))
    kb = k_ref[pl.ds(start, NA_TK), :].astype(BF16)
    vb = v_ref[pl.ds(start, NA_TK), :].astype(BF16)
    q = q_ref[...] * scale
    lane = lax.broadcasted_iota(jnp.int32, (1, 128), 1)
    o_loc, o_ctx = [], []
    for hh in range(2):
        qm = jnp.where((lane >> 6) == hh, q, 0.0).astype(BF16)
        s_loc = _dot_nt(qm, kb) + bias_ref[hh, case]
        qh = q[:, 64 * hh:64 * hh + 64].astype(BF16)
        s_ctx = _dot_nt(qh, kc_ref[hh].astype(BF16))
        m = jnp.maximum(jnp.max(s_loc, axis=-1, keepdims=True), jnp.max(s_ctx, axis=-1, keepdims=True))
        p_loc = jnp.exp(s_loc - m)
        p_ctx = jnp.exp(s_ctx - m)
        inv = 1.0 / (jnp.sum(p_loc, axis=-1, keepdims=True) + jnp.sum(p_ctx, axis=-1, keepdims=True))
        o_loc.append(_dot(p_loc.astype(BF16), vb) * inv)
        o_ctx.append(_dot(p_ctx.astype(BF16), vc_ref[hh].astype(BF16)) * inv)
    o_ref[...] = jnp.where((lane >> 6) == 0, o_loc[0], o_loc[1]) + jnp.concatenate(o_ctx, axis=-1)


def _na_attn(p, cache_k, cache_v, bias_tab, e, o_full):
    n_hp = NA_HEADS // 2
    lat0 = NT_CTX // NA_TQ
    per_b = DEC_SEQ // NA_TQ
    return pl.pallas_call(
        _na_kernel,
        grid=(n_hp, DEC_BATCH, NA_RB),
        in_specs=[
            pl.BlockSpec((NA_TQ, 128), lambda hp, b, rb: (lat0 + b * per_b + rb, _QA // 128 + hp)),
            pl.BlockSpec((DEC_SEQ, 128), lambda hp, b, rb: (1 + b, _KA // 128 + hp)),
            pl.BlockSpec((DEC_SEQ, 128), lambda hp, b, rb: (1 + b, _VA // 128 + hp)),
            pl.BlockSpec((None, None, 2, PAST_LEN, HEAD_DIM), lambda hp, b, rb: (b, e, hp, 0, 0)),
            pl.BlockSpec((None, None, 2, PAST_LEN, HEAD_DIM), lambda hp, b, rb: (b, e, hp, 0, 0)),
            pl.BlockSpec((None, 2, 3, NA_TQ, NA_TK), lambda hp, b, rb: (hp, 0, 0, 0, 0)),
            pl.BlockSpec(memory_space=pl.ANY),
        ],
        out_specs=pl.BlockSpec((NA_TQ, 128), lambda hp, b, rb: (lat0 + b * per_b + rb, hp)),
        out_shape=jax.ShapeDtypeStruct((NT, D), F32),
        input_output_aliases={6: 0},
        compiler_params=_cp(("arbitrary", "arbitrary", "arbitrary")),
        name="na_attn",
    )(p, p, p, cache_k, cache_v, bias_tab, o_full)


SWA_TQ = 256
SWA_TK = SWA_TQ + 2 * SWA_WINDOW


def _rope_tables():
    pos = np.arange(DEC_SEQ)
    row = (pos // GRID_W).astype(np.float32)
    col = (pos % GRID_W).astype(np.float32)
    half = HEAD_DIM // 2
    inv = jnp.asarray(ROPE_BASE, F32) ** (-jnp.arange(0, half, 2, dtype=F32) / half)
    ar = jnp.asarray(row)[:, None] * inv[None, :]
    ac = jnp.asarray(col)[:, None] * inv[None, :]
    cos = jnp.concatenate([jnp.cos(ar), jnp.cos(ar), jnp.cos(ac), jnp.cos(ac)], axis=1)
    sin = jnp.concatenate([-jnp.sin(ar), jnp.sin(ar), -jnp.sin(ac), jnp.sin(ac)], axis=1)
    return jnp.tile(cos, (1, 2)), jnp.tile(sin, (1, 2))


def _rope128(t, cos, sin):
    lane = lax.broadcasted_iota(jnp.int32, (1, 128), 1)
    up = pltpu.roll(t, 112, 1)
    dn = pltpu.roll(t, 16, 1)
    sw = jnp.where((lane & 31) < 16, up, dn)
    return t * cos + sw * sin


def _swa_mask_table():
    out = []
    for qb in (0, 1, DEC_SEQ // SWA_TQ - 1):
        q0 = qb * SWA_TQ
        bs = int(np.clip(q0 - SWA_WINDOW, 0, DEC_SEQ - SWA_TK))
        d = (q0 + np.arange(SWA_TQ))[:, None] - (bs + np.arange(SWA_TK))[None, :]
        out.append(np.where(np.abs(d) <= SWA_WINDOW, 0.0, MASK))
    return np.stack(out).astype(np.float32)


def _swa_kernel(sink_ref, q_ref, k_ref, v_ref, kc_ref, vc_ref, cos_ref, sin_ref, mask_ref, oin_ref, o_ref,
                kr_ref):
    del oin_ref
    qb = pl.program_id(1)
    scale = HEAD_DIM ** -0.5
    g = SWA_HEADS // SWA_KV_HEADS

    @pl.when(qb == 0)
    def _():
        kr_ref[...] = _rope128(k_ref[...], cos_ref[...], sin_ref[...]).astype(BF16)

    q0 = pl.multiple_of(qb * SWA_TQ, SWA_TQ)
    bs = pl.multiple_of(jnp.clip(q0 - SWA_WINDOW, 0, DEC_SEQ - SWA_TK), 128)
    cq = cos_ref[pl.ds(q0, SWA_TQ), :]
    sq = sin_ref[pl.ds(q0, SWA_TQ), :]
    kb = kr_ref[pl.ds(bs, SWA_TK), :]
    vb = v_ref[pl.ds(bs, SWA_TK), :].astype(BF16)
    qr = [_rope128(q_ref[:, 128 * j:128 * j + 128], cq, sq) * scale for j in range(SWA_HEADS // 2)]
    case = jnp.where(qb == 0, 0, jnp.where(qb == pl.num_programs(1) - 1, 2, 1))
    wmask = mask_ref[case]
    rowc = lax.broadcasted_iota(jnp.int32, (g * SWA_TQ, 1), 0) >> 8
    lane_half = lax.broadcasted_iota(jnp.int32, (1, 128), 1) >> 6

    for kvh in range(SWA_KV_HEADS):
        heads = []
        for gi in range(g):
            hq = kvh * g + gi
            heads.append(qr[hq // 2][:, 64 * (hq % 2):64 * (hq % 2) + 64])
        q4 = jnp.concatenate(heads, axis=0).astype(BF16)
        sink = jnp.zeros((g * SWA_TQ, 1), F32)
        for gi in range(g):
            sink = jnp.where(rowc == gi, sink_ref[kvh * g + gi], sink)
        kh = kb[:, 64 * kvh:64 * kvh + 64]
        vh = jnp.where(lane_half == kvh, vb, jnp.ones_like(vb))
        s_loc = (_dot_nt(q4, kh).reshape(g, SWA_TQ, SWA_TK) + wmask[None]).reshape(g * SWA_TQ, SWA_TK)
        s_ctx = _dot_nt(q4, kc_ref[kvh].astype(BF16))
        m = jnp.maximum(jnp.max(jnp.maximum(s_loc, s_ctx), axis=-1, keepdims=True), sink)
        p_loc = jnp.exp(s_loc - m).astype(BF16)
        p_ctx = jnp.exp(s_ctx - m).astype(BF16)
        pv = _dot(p_loc, vh) + _dot(p_ctx, vc_ref[kvh].astype(BF16))
        lo, hi = 64 * kvh, 64 * (1 - kvh)
        den = pv[:, hi:hi + 1] + jnp.exp(sink - m)
        o = pv[:, lo:lo + 64] / den
        for gi in range(g):
            hq = kvh * g + gi
            o_ref[:, 64 * hq:64 * hq + 64] = o[SWA_TQ * gi:SWA_TQ * (gi + 1), :]


def _swa_value_ext(cache_v):
    ones = jnp.ones_like(cache_v[..., 0, :, :])
    return jnp.stack([jnp.concatenate([cache_v[..., 0, :, :], ones], axis=-1),
                      jnp.concatenate([ones, cache_v[..., 1, :, :]], axis=-1)], axis=-3)


def _swa_attn(p, cache_k, cache_v_ext, sink, cos, sin, e, o_full):
    assert SWA_TQ == 256
    lat0 = NT_CTX // SWA_TQ
    per_b = DEC_SEQ // SWA_TQ
    return pl.pallas_call(
        _swa_kernel,
        grid=(DEC_BATCH, per_b),
        in_specs=[
            pl.BlockSpec(memory_space=pltpu.SMEM),
            pl.BlockSpec((SWA_TQ, 512), lambda b, qb: (lat0 + b * per_b + qb, _QB // 512)),
            pl.BlockSpec((DEC_SEQ, 128), lambda b, qb: (1 + b, _KB // 128)),
            pl.BlockSpec((DEC_SEQ, 128), lambda b, qb: (1 + b, _VB // 128)),
            pl.BlockSpec((None, None, 2, PAST_LEN, HEAD_DIM), lambda b, qb: (b, e, 0, 0, 0)),
            pl.BlockSpec((None, None, 2, PAST_LEN, 128), lambda b, qb: (b, e, 0, 0, 0)),
            pl.BlockSpec((DEC_SEQ, 128), lambda b, qb: (0, 0)),
            pl.BlockSpec((DEC_SEQ, 128), lambda b, qb: (0, 0)),
            pl.BlockSpec((3, SWA_TQ, SWA_TK), lambda b, qb: (0, 0, 0)),
            pl.BlockSpec(memory_space=pl.ANY),
        ],
        out_specs=pl.BlockSpec((SWA_TQ, 512), lambda b, qb: (lat0 + b * per_b + qb, 1)),
        out_shape=jax.ShapeDtypeStruct((NT, D), F32),
        scratch_shapes=[pltpu.VMEM((DEC_SEQ, 128), BF16)],
        input_output_aliases={9: 0},
        compiler_params=_cp(("arbitrary", "arbitrary")),
        name="swa_attn",
    )(sink, p, p, p, cache_k, cache_v_ext, cos, sin, jnp.asarray(_swa_mask_table()), o_full)


CONV_TT = 256
CONV_HALO = 16
CONV_RC = 32


def _conv_kernel(u_ref, ul_ref, ur_ref, w_ref, cb_ref, lg_ref, lb_ref, o_ref, xs_ref):
    i = pl.program_id(0)
    n_ctx_tiles = NT_CTX // CONV_TT
    per_seq = DEC_SEQ // CONV_TT
    j = i - n_ctx_tiles
    is_ctx = i < n_ctx_tiles
    is_start = is_ctx | ((j % per_seq) == 0)
    is_end = is_ctx | ((j % per_seq) == per_seq - 1)

    def glu(u):
        return u[:, :CONV_CH] * _sigmoid(u[:, CONV_CH:])

    xs_ref[0, CONV_HALO:CONV_HALO + CONV_TT, :] = glu(u_ref[...])
    xs_ref[0, 0:CONV_HALO, :] = jnp.where(is_start, 0.0, glu(ul_ref[...]))
    xs_ref[0, CONV_HALO + CONV_TT:, :] = jnp.where(is_end, 0.0, glu(ur_ref[...]))
    n_sh = CONV_TT + 2 * CONV_HALO - 8
    for b in range(1, 8):
        xs_ref[b, 0:n_sh, :] = xs_ref[0, b:b + n_sh, :]

    pad = CONV_WIDTH // 2
    for c in range(CONV_TT // CONV_RC):
        base = CONV_HALO + c * CONV_RC - pad
        acc = jnp.zeros((CONV_RC, CONV_CH), F32)
        for t in range(CONV_WIDTH):
            off = base + t
            acc = acc + xs_ref[off % 8, off - off % 8:off - off % 8 + CONV_RC, :] * w_ref[t:t + 1, :]
        y = acc + cb_ref[...]
        mu = jnp.mean(y, axis=-1, keepdims=True)
        yc = y - mu
        var = jnp.mean(yc * yc, axis=-1, keepdims=True)
        yn = yc * lax.rsqrt(var + EPS) * lg_ref[...] + lb_ref[...]
        o_ref[c * CONV_RC:(c + 1) * CONV_RC, :] = yn * _sigmoid(yn)


def _conv_module(p, conv_w, conv_b, ln_g, ln_b):
    nh = CONV_TT // CONV_HALO
    last = NT // CONV_HALO - 1
    return pl.pallas_call(
        _conv_kernel,
        grid=(NT // CONV_TT,),
        in_specs=[
            pl.BlockSpec((CONV_TT, 2 * CONV_CH), lambda i: (i, 0)),
            pl.BlockSpec((CONV_HALO, 2 * CONV_CH), lambda i: (jnp.maximum(i * nh - 1, 0), 0)),
            pl.BlockSpec((CONV_HALO, 2 * CONV_CH), lambda i: (jnp.minimum((i + 1) * nh, last), 0)),
            pl.BlockSpec((CONV_WIDTH, CONV_CH), lambda i: (0, 0)),
            pl.BlockSpec((1, CONV_CH), lambda i: (0, 0)),
            pl.BlockSpec((1, CONV_CH), lambda i: (0, 0)),
            pl.BlockSpec((1, CONV_CH), lambda i: (0, 0)),
        ],
        out_specs=pl.BlockSpec((CONV_TT, CONV_CH), lambda i: (i, 0)),
        out_shape=jax.ShapeDtypeStruct((NT, D), F32),
        scratch_shapes=[pltpu.VMEM((8, CONV_TT + 2 * CONV_HALO, CONV_CH), F32)],
        compiler_params=_cp(("arbitrary",)),
        name="conv_module",
    )(p, p, p, conv_w, conv_b.reshape(1, -1), ln_g.reshape(1, -1), ln_b.reshape(1, -1))


_U, _HQ, _HI, _HFF, _HFB, _HGT = 0, 1024, 1536, 2048, 2560, 3072
HG_LEVELS = 6
HG_W = HG_HEADS * HG_DK


def _hgrn_constants():
    c = HG_CHUNK
    mm = np.zeros((HG_LEVELS + 1, c, c), np.float32)
    for lv in range(HG_LEVELS):
        n = c >> lv
        half = n // 2
        for t in range(c):
            for s in range(c):
                if t // n == s // n and t % n >= half and s % n < half:
                    mm[lv, t, s] = 1.0
    mm[HG_LEVELS] = np.eye(c, dtype=np.float32)
    tri = np.tril(np.ones((c, c), np.float32))
    return np.stack([tri, tri.T]), np.stack([mm, mm[:, ::-1, ::-1]])


def _hgrn_level_refs(cum, cum_ref, fwd):
    c = HG_CHUNK
    cum_ref[...] = cum
    refs = []
    for lv in range(HG_LEVELS - 2):
        n = c >> lv
        rows = [b * n + n // 2 - (1 if fwd else 0) for b in range(c // n)]
        refs.append(jnp.concatenate([jnp.broadcast_to(cum_ref[r:r + 1, :], (n, HG_W)) for r in rows], axis=0))
    row = lax.broadcasted_iota(jnp.int32, (c, HG_W), 0)
    for n in (4, 2):
        p = n // 2 - (1 if fwd else 0)
        res = row & (n - 1)
        ref = None
        for j in range(n):
            shift = (j - p) % c
            cand = cum if shift == 0 else pltpu.roll(cum, shift, 0)
            ref = cand if ref is None else jnp.where(res == j, cand, ref)
        refs.append(ref)
    return refs


def _hgrn_chunk(qs, f, v, lb, tri, masks, st_ref, cum_ref, fwd):
    c = HG_CHUNK
    g = jnp.log(lb + (1.0 - lb) * _sigmoid(f))
    k = (1.0 - lb) * _sigmoid(-f)
    g_hi = g.astype(BF16)
    r1 = g - g_hi.astype(F32)
    g_mid = r1.astype(BF16)
    g_lo = (r1 - g_mid.astype(F32)).astype(BF16)
    cum = _dot(tri, g_hi) + (_dot(tri, g_mid) + _dot(tri, g_lo))
    refs = _hgrn_level_refs(cum, cum_ref, fwd)
    end_row = c - 1 if fwd else 0
    ecum = jnp.exp(cum)
    eend = jnp.exp(-jnp.abs(cum - jnp.broadcast_to(cum_ref[end_row:end_row + 1, :], (c, HG_W))))
    dec = ecum[end_row:end_row + 1, :]
    qk = [((qs * el).astype(BF16), (k * el).astype(BF16))
          for el in (jnp.exp(-jnp.abs(cum - ref)) for ref in refs)]
    qb, kb, vb = qs.astype(BF16), k.astype(BF16), v.astype(BF16)
    qc = (qs * ecum).astype(BF16)
    kend = (k * eend).astype(BF16)
    outs = []
    for h in range(HG_HEADS):
        sl = slice(h * HG_DK, (h + 1) * HG_DK)
        a = masks[HG_LEVELS] * _dot_nt(qb[:, sl], kb[:, sl])
        for lv in range(HG_LEVELS):
            a = a + masks[lv] * _dot_nt(qk[lv][0][:, sl], qk[lv][1][:, sl])
        s_t = st_ref[h]
        outs.append(_dot(a.astype(BF16), vb[:, sl]) + _dot_nt(qc[:, sl], s_t.astype(BF16)))
        st_ref[h] = s_t * dec[:, sl] + _dot_tn(vb[:, sl], kend[:, sl])
    return jnp.concatenate(outs, axis=1)


def _hgrn_kernel(qf_ref, if_ref, ff_ref, qb_ref, ib_ref, fb_ref, lb_ref, s0_ref, tri_ref, m_ref,
                 of_ref, ob_ref, sfin_ref, stf_ref, stb_ref, cumf_ref, cumb_ref, *, tile):
    c = HG_CHUNK
    nct = tile // c
    tt = pl.program_id(1)

    @pl.when(tt == 0)
    def _():
        stf_ref[...] = s0_ref[0]
        stb_ref[...] = s0_ref[1]

    lbf = lb_ref[0:1, :]
    lbb = lb_ref[1:2, :]

    def body(ci, carry):
        rf = pl.multiple_of(ci * c, c)
        rb = pl.multiple_of((nct - 1 - ci) * c, c)
        q = qf_ref[pl.ds(rf, c), :]
        of_ref[pl.ds(rf, c), :] = _hgrn_chunk(
            q * _sigmoid(q), ff_ref[pl.ds(rf, c), :], if_ref[pl.ds(rf, c), :], lbf, tri_ref[0],
            [m_ref[0, x] for x in range(HG_LEVELS + 1)], stf_ref, cumf_ref, True)
        q = qb_ref[pl.ds(rb, c), :]
        ob_ref[pl.ds(rb, c), :] = _hgrn_chunk(
            q * _sigmoid(q), fb_ref[pl.ds(rb, c), :], ib_ref[pl.ds(rb, c), :], lbb, tri_ref[1],
            [m_ref[1, x] for x in range(HG_LEVELS + 1)], stb_ref, cumb_ref, False)
        return carry

    lax.fori_loop(0, nct, body, 0, unroll=2)

    @pl.when(tt == pl.num_programs(1) - 1)
    def _():
        sfin_ref[0] = stf_ref[...]
        sfin_ref[1] = stb_ref[...]


def _hgrn_scan(p, lb, s0_t, n_batch, t_len, row0):
    tri, mm = _hgrn_constants()
    tri = jnp.asarray(tri, BF16)
    mm = jnp.asarray(mm, F32)
    tile = min(t_len, 512)
    ntt = t_len // tile
    r0 = row0 // tile

    def fwd(c0):
        return pl.BlockSpec((tile, HG_W), lambda b, t: (r0 + b * ntt + t, c0 // HG_W))

    def bwd(c0):
        return pl.BlockSpec((tile, HG_W), lambda b, t: (r0 + b * ntt + ntt - 1 - t, c0 // HG_W))

    st_spec = pl.BlockSpec((None, 2, HG_HEADS, 128, 128), lambda b, t: (b, 0, 0, 0, 0))
    n_rows = n_batch * t_len
    return pl.pallas_call(
        functools.partial(_hgrn_kernel, tile=tile),
        grid=(n_batch, ntt),
        in_specs=[
            fwd(_HQ), fwd(_HI), fwd(_HFF), bwd(_HQ), bwd(_HI), bwd(_HFB),
            pl.BlockSpec((2, HG_W), lambda b, t: (0, 0)),
            st_spec,
            pl.BlockSpec((2, HG_CHUNK, HG_CHUNK), lambda b, t: (0, 0, 0)),
            pl.BlockSpec((2, HG_LEVELS + 1, HG_CHUNK, HG_CHUNK), lambda b, t: (0, 0, 0, 0)),
        ],
        out_specs=[
            pl.BlockSpec((tile, HG_W), lambda b, t: (b * ntt + t, 0)),
            pl.BlockSpec((tile, HG_W), lambda b, t: (b * ntt + ntt - 1 - t, 0)),
            st_spec,
        ],
        out_shape=[jax.ShapeDtypeStruct((n_rows, HG_W), F32),
                   jax.ShapeDtypeStruct((n_rows, HG_W), F32),
                   jax.ShapeDtypeStruct((n_batch, 2, HG_HEADS, 128, 128), F32)],
        scratch_shapes=[pltpu.VMEM((HG_HEADS, 128, 128), F32), pltpu.VMEM((HG_HEADS, 128, 128), F32),
                        pltpu.VMEM((HG_CHUNK, HG_W), F32), pltpu.VMEM((HG_CHUNK, HG_W), F32)],
        compiler_params=_cp(("arbitrary", "arbitrary")),
        name="hgrn_scan",
    )(p, p, p, p, p, p, lb, s0_t, tri, mm)


def _hgrn_fin_kernel(of_ref, ob_ref, gt_ref, ng_ref, oin_ref, o_ref):
    del oin_ref
    od = of_ref[...] + ob_ref[...]
    gt = gt_ref[...]
    gate = gt * _sigmoid(gt)
    for h in range(HG_HEADS):
        sl = slice(h * HG_DK, (h + 1) * HG_DK)
        x = od[:, sl]
        ms = jnp.mean(x * x, axis=-1, keepdims=True)
        o_ref[:, sl] = x * lax.rsqrt(ms + EPS) * ng_ref[...] * gate[:, sl]


def _hgrn_finish(o_f, o_b, p, norm_g, o_full, row0, tm=512):
    n_rows = o_f.shape[0]
    r0 = row0 // tm
    return pl.pallas_call(
        _hgrn_fin_kernel,
        grid=(n_rows // tm,),
        in_specs=[
            pl.BlockSpec((tm, HG_W), lambda i: (i, 0)),
            pl.BlockSpec((tm, HG_W), lambda i: (i, 0)),
            pl.BlockSpec((tm, HG_W), lambda i: (r0 + i, _HGT // HG_W)),
            pl.BlockSpec((1, 128), lambda i: (0, 0)),
            pl.BlockSpec(memory_space=pl.ANY),
        ],
        out_specs=pl.BlockSpec((tm, HG_W), lambda i: (r0 + i, 1)),
        out_shape=jax.ShapeDtypeStruct((NT, D), F32),
        input_output_aliases={4: 0},
        compiler_params=_cp(("arbitrary",)),
        name="hgrn_finish",
    )(o_f, o_b, p, norm_g.reshape(1, 128), o_full)


def _hgrn(p, lb, norm_g, s0_t, o_full, n_batch, t_len, row0):
    o_f, o_b, s_fin = _hgrn_scan(p, lb, s0_t, n_batch, t_len, row0)
    return _hgrn_finish(o_f, o_b, p, norm_g, o_full, row0), s_fin


ROUTE_TM = 512


def _mmres_router_kernel(o_ref, x_ref, gate_ref, w_ref, g_ref, sh_ref, sc_ref, whi_ref, br_ref,
                         tri_ref, xo_ref, mi_ref, mf_ref, cnt_ref, carry_ref):
    i = pl.program_id(0)

    @pl.when(i == 0)
    def _():
        carry_ref[...] = jnp.zeros_like(carry_ref)

    xn = x_ref[...] + gate_ref[...] * _dot(o_ref[...].astype(BF16), w_ref[...])
    xo_ref[...] = xn
    h = _modulate(xn, g_ref[...], sh_ref[...], sc_ref[...])
    h_hi, h_lo = _split2(h)
    hw = _dot(h_hi, whi_ref[...])
    logits = hw[:, :128] + (hw[:, 128:] + _dot(h_lo, whi_ref[:, :128])) + br_ref[...]
    lane = lax.broadcasted_iota(jnp.int32, logits.shape, 1)
    neg = jnp.float32(-3e38)
    is_g = lane < N_GROUPS
    lg = jnp.where(is_g, logits, neg)
    mg = jnp.max(lg, axis=-1, keepdims=True)
    grp = jnp.min(jnp.where(lg == mg, lane, 128), axis=-1, keepdims=True)
    pg = 1.0 / jnp.sum(jnp.where(is_g, jnp.exp(lg - mg), 0.0), axis=-1, keepdims=True)
    ex = lane - N_GROUPS
    in_grp = (ex >= 0) & (ex < N_EXPERTS) & ((ex >> 3) == grp)
    le = jnp.where(in_grp, logits, neg)
    v1 = jnp.max(le, axis=-1, keepdims=True)
    i1 = jnp.min(jnp.where(le == v1, lane, 128), axis=-1, keepdims=True)
    le2 = jnp.where(lane == i1, neg, le)
    v2 = jnp.max(le2, axis=-1, keepdims=True)
    i2 = jnp.min(jnp.where(le2 == v2, lane, 128), axis=-1, keepdims=True)
    t = jnp.exp(v2 - v1)
    g1 = pg / (1.0 + t)
    g2 = pg * t / (1.0 + t)
    oh = jnp.where((lane == i1) | (lane == i2), 1.0, 0.0)
    prefix = _dot(tri_ref[...], oh.astype(BF16)) + carry_ref[...]
    r1 = jnp.sum(jnp.where(lane == i1, prefix, 0.0), axis=-1, keepdims=True).astype(jnp.int32)
    r2 = jnp.sum(jnp.where(lane == i2, prefix, 0.0), axis=-1, keepdims=True).astype(jnp.int32)
    carry_ref[...] = carry_ref[...] + jnp.sum(oh, axis=0, keepdims=True)
    cnt_ref[...] = carry_ref[...]
    mi_ref[...] = jnp.where(lane == 0, i1 - N_GROUPS, jnp.where(lane == 1, i2 - N_GROUPS,
                            jnp.where(lane == 2, r1, jnp.where(lane == 3, r2, 0))))
    mf_ref[...] = jnp.where(lane == 0, g1, jnp.where(lane == 1, g2, 0.0))


def _mmres_router(o, x, mod, l, w_out_bf16, g, wg, bg, we, be):
    tm = ROUTE_TM
    wr = jnp.zeros((D, 128), F32).at[:, :N_GROUPS].set(wg).at[:, N_GROUPS:N_GROUPS + N_EXPERTS].set(we)
    br = jnp.zeros((1, 128), F32).at[0, :N_GROUPS].set(bg).at[0, N_GROUPS:N_GROUPS + N_EXPERTS].set(be)
    w_hi = wr.astype(BF16)
    w_lo = (wr - w_hi.astype(F32)).astype(BF16)
    tri = jnp.asarray(np.tril(np.ones((tm, tm), np.float32), -1), BF16)
    const = lambda i: (0, 0)
    return pl.pallas_call(
        _mmres_router_kernel,
        grid=(NT // tm,),
        in_specs=[
            pl.BlockSpec((tm, D), lambda i: (i, 0)),
            pl.BlockSpec((tm, D), lambda i: (i, 0)),
            _mod_spec(l, 2, tm),
            pl.BlockSpec((D, D), const),
            pl.BlockSpec((1, D), const),
            _mod_spec(l, 3, tm),
            _mod_spec(l, 4, tm),
            pl.BlockSpec((D, 256), const),
            pl.BlockSpec((1, 128), const),
            pl.BlockSpec((tm, tm), const),
        ],
        out_specs=[
            pl.BlockSpec((tm, D), lambda i: (i, 0)),
            pl.BlockSpec((tm, 128), lambda i: (i, 0)),
            pl.BlockSpec((tm, 128), lambda i: (i, 0)),
            pl.BlockSpec((1, 128), const),
        ],
        out_shape=[
            jax.ShapeDtypeStruct((NT, D), F32),
            jax.ShapeDtypeStruct((NT, 128), jnp.int32),
            jax.ShapeDtypeStruct((NT, 128), F32),
            jax.ShapeDtypeStruct((1, 128), F32),
        ],
        scratch_shapes=[pltpu.VMEM((1, 128), F32)],
        compiler_params=_cp(("arbitrary",)),
        name="mmres_router",
    )(o, x, mod, w_out_bf16, g.reshape(1, D), mod, mod, jnp.concatenate([w_hi, w_lo], axis=1), br, tri)


DISP_TM = 512


def _row_copy(src, s_row, dst, d_row, sem):
    return pltpu.make_async_copy(src.at[pl.ds(s_row, 1)], dst.at[pl.ds(d_row, 1)], sem)


def _dispatch_kernel(dest_ref, x_ref, g_ref, sh_ref, sc_ref, xs_hbm, h_ref, sem):
    h_ref[...] = _modulate(x_ref[...], g_ref[...], sh_ref[...], sc_ref[...])

    def group(t8, start):
        base = pl.multiple_of(t8 * 8, 8)
        for s in range(8):
            for k in range(2):
                cp = _row_copy(h_ref, base + s, xs_hbm, dest_ref[k, base + s], sem)
                if start:
                    cp.start(priority=k)
                else:
                    cp.wait()

    lax.fori_loop(0, DISP_TM // 8, lambda t8, c: (group(t8, True), c)[1], 0)
    lax.fori_loop(0, DISP_TM // 8, lambda t8, c: (group(t8, False), c)[1], 0)


def _dispatch(dest_t, x, g, mod, l):
    return pl.pallas_call(
        _dispatch_kernel,
        grid=(NT // DISP_TM,),
        in_specs=[
            pl.BlockSpec((2, DISP_TM), lambda i: (0, i), memory_space=pltpu.SMEM),
            pl.BlockSpec((DISP_TM, D), lambda i: (i, 0)),
            pl.BlockSpec((1, D), lambda i: (0, 0)),
            _mod_spec(l, 3, DISP_TM),
            _mod_spec(l, 4, DISP_TM),
        ],
        out_specs=pl.BlockSpec(memory_space=pl.ANY),
        out_shape=jax.ShapeDtypeStruct((MOE_SLOTS, D), F32),
        scratch_shapes=[pltpu.VMEM((DISP_TM, D), F32), pltpu.SemaphoreType.DMA(())],
        compiler_params=_cp(("arbitrary",)),
        name="moe_dispatch",
    )(dest_t, x, g.reshape(1, D), mod, mod)


def _expert_kernel(bexp_ref, bval_ref, bfirst_ref, xs_ref, w1_ref, w3_ref, w2_ref, y_ref,
                   w1b_ref, w3b_ref, w2b_ref):
    del bexp_ref
    i = pl.program_id(0)
    nv = bval_ref[i]

    @pl.when(bfirst_ref[i] == 1)
    def _():
        w1b_ref[...] = w1_ref[...].astype(BF16)
        w3b_ref[...] = w3_ref[...].astype(BF16)
        w2b_ref[...] = w2_ref[...].astype(BF16)

    @pl.when(nv > 0)
    def _():
        row = lax.broadcasted_iota(jnp.int32, (MOE_MB, 1), 0)
        xb = jnp.where(row < nv, xs_ref[...], 0.0).astype(BF16)
        a = _dot(xb, w1b_ref[...])
        b = _dot(xb, w3b_ref[...])
        hid = (a * _sigmoid(a) * b).astype(BF16)
        y_ref[...] = _dot(hid, w2b_ref[...])

    @pl.when(nv <= 0)
    def _():
        y_ref[...] = jnp.zeros_like(y_ref)


def _experts(blk_exp, blk_valid, blk_first, xs, w1, w3, w2, l):
    grid_spec = pltpu.PrefetchScalarGridSpec(
        num_scalar_prefetch=3,
        grid=(MOE_NB,),
        in_specs=[
            pl.BlockSpec((MOE_MB, D), lambda i, be, bv, bf: (i, 0)),
            pl.BlockSpec((None, None, D, D_EXPERT), lambda i, be, bv, bf: (l, be[i], 0, 0)),
            pl.BlockSpec((None, None, D, D_EXPERT), lambda i, be, bv, bf: (l, be[i], 0, 0)),
            pl.BlockSpec((None, None, D_EXPERT, D), lambda i, be, bv, bf: (l, be[i], 0, 0)),
        ],
        out_specs=pl.BlockSpec((MOE_MB, D), lambda i, be, bv, bf: (i, 0)),
        scratch_shapes=[pltpu.VMEM((D, D_EXPERT), BF16), pltpu.VMEM((D, D_EXPERT), BF16),
                        pltpu.VMEM((D_EXPERT, D), BF16)],
    )
    return pl.pallas_call(
        _expert_kernel,
        grid_spec=grid_spec,
        out_shape=jax.ShapeDtypeStruct((MOE_SLOTS, D), F32),
        compiler_params=_cp(("arbitrary",)),
        name="moe_experts",
    )(blk_exp, blk_valid, blk_first, xs, w1, w3, w2)


COMB_TM = 256


def _combine_kernel(dest_ref, mf_ref, x_ref, gate_ref, y_hbm, o_ref, buf_ref, sem):
    def group(t8, start):
        base = pl.multiple_of(t8 * 8, 8)
        for s in range(8):
            for k in range(2):
                cp = _row_copy(y_hbm, dest_ref[k, base + s], buf_ref.at[k], base + s, sem)
                if start:
                    cp.start(priority=k)
                else:
                    cp.wait()

    lax.fori_loop(0, COMB_TM // 8, lambda t8, c: (group(t8, True), c)[1], 0)
    lax.fori_loop(0, COMB_TM // 8, lambda t8, c: (group(t8, False), c)[1], 0)
    mf = mf_ref[...]
    y = mf[:, 0:1] * buf_ref[0] + mf[:, 1:2] * buf_ref[1]
    o_ref[...] = x_ref[...] + gate_ref[...] * y


def _combine(dest_t, mf, x, mod, l, y):
    tm = COMB_TM
    return pl.pallas_call(
        _combine_kernel,
        grid=(NT // tm,),
        in_specs=[
            pl.BlockSpec((2, tm), lambda i: (0, i), memory_space=pltpu.SMEM),
            pl.BlockSpec((tm, 128), lambda i: (i, 0)),
            pl.BlockSpec((tm, D), lambda i: (i, 0)),
            _mod_spec(l, 5, tm),
            pl.BlockSpec(memory_space=pl.ANY),
        ],
        out_specs=pl.BlockSpec((tm, D), lambda i: (i, 0)),
        out_shape=jax.ShapeDtypeStruct((NT, D), F32),
        scratch_shapes=[pltpu.VMEM((2, tm, D), F32), pltpu.SemaphoreType.DMA(())],
        compiler_params=_cp(("arbitrary",)),
        name="moe_combine",
    )(dest_t, mf, x, mod, y)


def _comb_modmm_kernel(dcur_ref, dnxt_ref, mf_ref, x_ref, gate_ref, g_ref, sh_ref, sc_ref, w_ref, y_hbm,
                       xo_ref, p_ref, buf_ref, sem, *, nchunk):
    i = pl.program_id(0)
    n = pl.num_programs(0)
    slot = lax.rem(i, 2)
    tm = x_ref.shape[0]

    def rows(dest_ref, to_slot, lo, hi, start):
        for t in range(lo, hi):
            for k in range(2):
                cp = _row_copy(y_hbm, dest_ref[k, t], buf_ref.at[to_slot, k], t, sem.at[to_slot])
                if start:
                    cp.start(priority=k)
                else:
                    cp.wait()

    def rolled(dest_ref, to_slot, start):
        def body(t8, c):
            base = pl.multiple_of(t8 * 8, 8)
            for s in range(8):
                for k in range(2):
                    cp = _row_copy(y_hbm, dest_ref[k, base + s], buf_ref.at[to_slot, k], base + s,
                                   sem.at[to_slot])
                    if start:
                        cp.start(priority=k)
                    else:
                        cp.wait()
            return c
        lax.fori_loop(0, tm // 8, body, 0)

    @pl.when(i == 0)
    def _():
        rolled(dcur_ref, slot, True)

    rolled(dcur_ref, slot, False)
    mf = mf_ref[...]
    y = mf[:, 0:1] * buf_ref[slot, 0] + mf[:, 1:2] * buf_ref[slot, 1]
    xn = x_ref[...] + gate_ref[...] * y
    xo_ref[...] = xn
    h = _modulate(xn, g_ref[...], sh_ref[...], sc_ref[...]).astype(BF16)
    nout = p_ref.shape[1]
    n_ch = nout // nchunk
    per = -(-tm // n_ch)
    for c in range(n_ch):
        rows(dnxt_ref, 1 - slot, min(c * per, tm), min((c + 1) * per, tm), True)
        p_ref[:, c * nchunk:(c + 1) * nchunk] = _dot(h, w_ref[:, c * nchunk:(c + 1) * nchunk])

    @pl.when(i == n - 1)
    def _():
        rolled(dnxt_ref, 1 - slot, False)


def _comb_modmm(dest_t, mf, x, mod, l, y, g_next, w_bf16, tm=256):
    nout = w_bf16.shape[1]
    nt = NT // tm
    return pl.pallas_call(
        functools.partial(_comb_modmm_kernel, nchunk=256),
        grid=(nt,),
        in_specs=[
            pl.BlockSpec((2, tm), lambda i: (0, i), memory_space=pltpu.SMEM),
            pl.BlockSpec((2, tm), lambda i: (0, jnp.minimum(i + 1, nt - 1)), memory_space=pltpu.SMEM),
            pl.BlockSpec((tm, 128), lambda i: (i, 0)),
            pl.BlockSpec((tm, D), lambda i: (i, 0)),
            _mod_spec(l, 5, tm),
            pl.BlockSpec((1, D), lambda i: (0, 0)),
            _mod_spec(l + 1, 0, tm),
            _mod_spec(l + 1, 1, tm),
            pl.BlockSpec((D, nout), lambda i: (0, 0)),
            pl.BlockSpec(memory_space=pl.ANY),
        ],
        out_specs=[pl.BlockSpec((tm, D), lambda i: (i, 0)), pl.BlockSpec((tm, nout), lambda i: (i, 0))],
        out_shape=[jax.ShapeDtypeStruct((NT, D), F32), jax.ShapeDtypeStruct((NT, nout), F32)],
        scratch_shapes=[pltpu.VMEM((2, 2, tm, D), F32), pltpu.SemaphoreType.DMA((2,))],
        compiler_params=_cp(("arbitrary",)),
        name="moe_combine_modmm",
    )(dest_t, dest_t, mf, x, mod, g_next.reshape(1, D), mod, mod, w_bf16, y)


def _moe_plan(mi, counts):
    cnt = counts[0, N_GROUPS:N_GROUPS + N_EXPERTS].astype(jnp.int32)
    padded = (cnt + MOE_MB - 1) // MOE_MB * MOE_MB
    pad_ends = jnp.cumsum(padded)
    pad_starts = pad_ends - padded
    dest = pad_starts[mi[:, 0:2]] + mi[:, 2:4]
    blk0 = jnp.arange(MOE_NB, dtype=jnp.int32) * MOE_MB
    blk_exp = jnp.sum((pad_ends[None, :] <= blk0[:, None]).astype(jnp.int32), axis=1)
    blk_exp = jnp.minimum(blk_exp, N_EXPERTS - 1)
    blk_valid = jnp.clip(cnt[blk_exp] - (blk0 - pad_starts[blk_exp]), 0, MOE_MB)
    blk_valid = jnp.where(blk0 < pad_ends[-1], blk_valid, 0).astype(jnp.int32)
    blk_first = jnp.concatenate([jnp.ones((1,), jnp.int32),
                                 (blk_exp[1:] != blk_exp[:-1]).astype(jnp.int32)])
    return dest.T.astype(jnp.int32), blk_exp, blk_valid, blk_first


def _moe_experts(x, g, mod, l, mi, mf, counts, w1, w3, w2):
    dest_t, blk_exp, blk_valid, blk_first = _moe_plan(mi, counts)
    xs = _dispatch(dest_t, x, g, mod, l)
    y = _experts(blk_exp, blk_valid, blk_first, xs, w1, w3, w2, l)
    return dest_t, mf, y


def kernel(x_prompt, x_sample, cache_k_na, cache_v_na, cache_k_swa, cache_v_swa, state_hgrn, c, c_ctx, w_mod, b_mod, norm_mix_g, norm_ffn_g, w_in_even, w_out_even, na_rel_bias, swa_sink, w_in_odd, w_out_odd, conv_w, conv_b, conv_norm_g, conv_norm_b, hgrn_lb_raw, hgrn_norm_g, router_group_w, router_group_b, router_expert_w, router_expert_b, moe_w1, moe_w3, moe_w2, final_norm_g):
    x = jnp.concatenate([x_prompt.reshape(NT_CTX, D), x_sample.reshape(NT_LAT, D)], axis=0)
    cvec = jnp.zeros((16, D), F32).at[0].set(c_ctx).at[1:1 + DEC_BATCH].set(c)
    mod = _mod_table(cvec, w_mod, b_mod)

    lb_p = jax.nn.softmax(hgrn_lb_raw.astype(F32), axis=0)
    lower_bounds = jnp.cumsum(lb_p, axis=0) - lb_p[0:1]
    cos, sin = _rope_tables()
    cache_v_swa_ext = _swa_value_ext(cache_v_swa)

    k_na, v_na, k_swa, v_swa, s_hg = [], [], [], [], []
    moe = None
    for l in range(DEPTH):
        w_in = (w_in_even if l % 2 == 0 else w_in_odd)[l // 2].astype(BF16)
        if moe is None:
            p = _modmm(x, norm_mix_g[l], mod, l, w_in, tm=512)
        else:
            x, p = _comb_modmm(*moe[:2], x, mod, l - 1, moe[2], norm_mix_g[l], w_in)
        if l % 2 == 0:
            e = l // 2
            o = _ctx_attn(p, swa_sink[e])
            o = _na_attn(p, cache_k_na, cache_v_na, _na_bias_table(na_rel_bias[e]), e, o)
            o = _swa_attn(p, cache_k_swa, cache_v_swa_ext, swa_sink[e], cos, sin, e, o)
            pc = p[:NT_CTX]
            to_heads = lambda u, nh: u.reshape(BATCH, SEQ, nh, HEAD_DIM).transpose(0, 2, 1, 3)
            k_na.append(to_heads(pc[:, _KA:_KA + 512], NA_HEADS))
            v_na.append(to_heads(pc[:, _VA:_VA + 512], NA_HEADS))
            k_swa.append(to_heads(pc[:, _KB:_KB + 128], SWA_KV_HEADS))
            v_swa.append(to_heads(pc[:, _VB:_VB + 128], SWA_KV_HEADS))
            w_out = w_out_even[e]
        else:
            j = l // 2
            o = _conv_module(p, conv_w[j], conv_b[j], conv_norm_g[j], conv_norm_b[j])
            zero_state = jnp.zeros((BATCH, 2, HG_HEADS, 128, 128), F32)
            o, s_ctx = _hgrn(p, lower_bounds[j], hgrn_norm_g[j], zero_state, o, BATCH, SEQ, 0)
            s0_lat = jnp.swapaxes(state_hgrn[:, j], -1, -2)
            o, _ = _hgrn(p, lower_bounds[j], hgrn_norm_g[j], s0_lat, o, DEC_BATCH, DEC_SEQ, NT_CTX)
            s_hg.append(jnp.swapaxes(s_ctx, -1, -2))
            w_out = w_out_odd[j]
        x, mi, mf, counts = _mmres_router(o, x, mod, l, w_out.astype(BF16), norm_ffn_g[l],
                                          router_group_w[l], router_group_b[l],
                                          router_expert_w[l], router_expert_b[l])
        moe = _moe_experts(x, norm_ffn_g[l], mod, l, mi, mf, counts, moe_w1, moe_w3, moe_w2)

    x = _combine(*moe[:2], x, mod, DEPTH - 1, moe[2])
    y_prompt = _final_norm(x, final_norm_g, 0, NT_CTX).reshape(BATCH, SEQ, D)
    y_sample = _final_norm(x, final_norm_g, NT_CTX, NT_LAT).reshape(DEC_BATCH, DEC_SEQ, D)
    return (y_prompt, y_sample, jnp.stack(k_na, axis=1), jnp.stack(v_na, axis=1),
            jnp.stack(k_swa, axis=1), jnp.stack(v_swa, axis=1), jnp.stack(s_hg, axis=1))
```

```python
import functools

import numpy as np
import jax
import jax.numpy as jnp
from jax import lax
from jax.experimental import pallas as pl
from jax.experimental.pallas import tpu as pltpu

F32 = jnp.float32
BF16 = jnp.bfloat16

D = 1024
BATCH = 16
SEQ = 256
DEPTH = 4
DEC_BATCH = 8
DEC_SEQ = 4096
PAST_LEN = 512
GRID_W = 64
HEAD_DIM = 64
EPS = 1e-6
NA_HEADS = 8
NA_WIN_R = 8
NA_WIN_C = 16
SWA_HEADS = 8
SWA_KV_HEADS = 2
SWA_WINDOW = 128
ROPE_BASE = 10000.0
CONV_CH = 512
CONV_WIDTH = 31
HG_HEADS = 4
HG_DK = 128
HG_CHUNK = 64
N_GROUPS = 4
EXPERTS_PER_GROUP = 8
N_EXPERTS = 32
D_EXPERT = 512
N_EVEN = 2
N_ODD = 2
D_IN_EVEN = 2304
D_IN_ODD = 3584

NT_CTX = BATCH * SEQ
NT_LAT = DEC_BATCH * DEC_SEQ
NT = NT_CTX + NT_LAT
SEG = 4096
assert NT_CTX == SEG and DEC_SEQ == SEG
N_MOD = 1 + DEC_BATCH
MASK = -1e30

MOE_MB = 512
MOE_NB = (2 * NT) // MOE_MB + N_EXPERTS
MOE_SLOTS = MOE_NB * MOE_MB

VMEM_LIMIT = 56 * 1024 * 1024


def _cp(sem, vmem=VMEM_LIMIT):
    return pltpu.CompilerParams(dimension_semantics=sem, vmem_limit_bytes=vmem)


def _dot(a, b):
    return jnp.dot(a, b, preferred_element_type=F32)


def _dot_nt(a, b):
    return lax.dot_general(a, b, (((1,), (1,)), ((), ())), preferred_element_type=F32)


def _dot_tn(a, b):
    return lax.dot_general(a, b, (((0,), (0,)), ((), ())), preferred_element_type=F32)


def _split2(a):
    hi = a.astype(BF16)
    lo = (a - hi.astype(F32)).astype(BF16)
    return hi, lo


def _dot3(a, b_hi, b_lo):
    a_hi, a_lo = _split2(a)
    return _dot(a_hi, b_hi) + (_dot(a_hi, b_lo) + _dot(a_lo, b_hi))


def _sigmoid(x):
    return 1.0 / (1.0 + jnp.exp(-x))


def _modulate(x, g, shift, scale):
    ms = jnp.mean(x * x, axis=-1, keepdims=True)
    return (x * lax.rsqrt(ms + EPS) * g) * (1.0 + scale) + shift


def _mod_kernel(c_ref, whi_ref, wlo_ref, b_ref, o_ref):
    cv = c_ref[...]
    s = cv * _sigmoid(cv)
    o_ref[...] = _dot3(s, whi_ref[...], wlo_ref[...]) + b_ref[...]


def _mod_table(cvec, w_mod, b_mod):
    tn = 1536
    w_hi = w_mod.astype(BF16)
    w_lo = (w_mod - w_hi.astype(F32)).astype(BF16)
    out = pl.pallas_call(
        _mod_kernel,
        grid=(DEPTH, 6 * D // tn),
        in_specs=[
            pl.BlockSpec((16, D), lambda l, j: (0, 0)),
            pl.BlockSpec((None, D, tn), lambda l, j: (l, 0, j)),
            pl.BlockSpec((None, D, tn), lambda l, j: (l, 0, j)),
            pl.BlockSpec((None, 1, tn), lambda l, j: (l, 0, j)),
        ],
        out_specs=pl.BlockSpec((None, 16, tn), lambda l, j: (l, 0, j)),
        out_shape=jax.ShapeDtypeStruct((DEPTH, 16, 6 * D), F32),
        compiler_params=_cp(("arbitrary", "arbitrary")),
        name="mod_table",
    )(cvec, w_hi, w_lo, b_mod.reshape(DEPTH, 1, 6 * D))
    return out.reshape(DEPTH, 16, 1, 6 * D)


def _mod_spec(l, which, tm):
    return pl.BlockSpec((None, None, 1, D), lambda i, *_: (l, (i * tm) // SEG, 0, which))


def _modmm_kernel(x_ref, g_ref, sh_ref, sc_ref, w_ref, o_ref, *, nchunk):
    h = _modulate(x_ref[...], g_ref[...], sh_ref[...], sc_ref[...]).astype(BF16)
    n = o_ref.shape[1]
    for n0 in range(0, n, nchunk):
        o_ref[:, n0:n0 + nchunk] = _dot(h, w_ref[:, n0:n0 + nchunk])


def _modmm(x, g, mod, l, w_bf16, tm):
    n = w_bf16.shape[1]
    return pl.pallas_call(
        functools.partial(_modmm_kernel, nchunk=256),
        grid=(NT // tm,),
        in_specs=[
            pl.BlockSpec((tm, D), lambda i: (i, 0)),
            pl.BlockSpec((1, D), lambda i: (0, 0)),
            _mod_spec(l, 0, tm),
            _mod_spec(l, 1, tm),
            pl.BlockSpec((D, n), lambda i: (0, 0)),
        ],
        out_specs=pl.BlockSpec((tm, n), lambda i: (i, 0)),
        out_shape=jax.ShapeDtypeStruct((NT, n), F32),
        compiler_params=_cp(("arbitrary",)),
        name="modmm",
    )(x, g.reshape(1, D), mod, mod, w_bf16)


def _final_kernel(x_ref, g_ref, o_ref):
    x = x_ref[...]
    ms = jnp.mean(x * x, axis=-1, keepdims=True)
    o_ref[...] = x * lax.rsqrt(ms + EPS) * g_ref[...]


def _final_norm(x, g, row0, n_rows, tm=512):
    r0 = row0 // tm
    return pl.pallas_call(
        _final_kernel,
        grid=(n_rows // tm,),
        in_specs=[pl.BlockSpec((tm, D), lambda i: (r0 + i, 0)), pl.BlockSpec((1, D), lambda i: (0, 0))],
        out_specs=pl.BlockSpec((tm, D), lambda i: (i, 0)),
        out_shape=jax.ShapeDtypeStruct((n_rows, D), F32),
        compiler_params=_cp(("arbitrary",)),
        name="final_norm",
    )(x, g.reshape(1, D))


_QA, _KA, _VA, _QB, _KB, _VB = 0, 512, 1024, 1536, 2048, 2176


def _ctx_attn_kernel(sink_ref, p_ref, o_ref):
    scale = HEAD_DIM ** -0.5

    def head(qc, kc, vc, sink):
        q = p_ref[:, qc:qc + 64].astype(BF16)
        k = p_ref[:, kc:kc + 64].astype(BF16)
        v = p_ref[:, vc:vc + 64].astype(BF16)
        s = _dot_nt(q, k) * scale
        m = jnp.max(s, axis=-1, keepdims=True)
        if sink is not None:
            m = jnp.maximum(m, sink)
        p = jnp.exp(s - m)
        den = jnp.sum(p, axis=-1, keepdims=True)
        if sink is not None:
            den = den + jnp.exp(sink - m)
        return _dot(p.astype(BF16), v) / den

    for h in range(NA_HEADS):
        o_ref[:, 64 * h:64 * h + 64] = head(_QA + 64 * h, _KA + 64 * h, _VA + 64 * h, None)
    for j in range(SWA_HEADS):
        kv = j // (SWA_HEADS // SWA_KV_HEADS)
        o_ref[:, 512 + 64 * j:512 + 64 * j + 64] = head(_QB + 64 * j, _KB + 64 * kv, _VB + 64 * kv, sink_ref[j])


def _ctx_attn(p, sink):
    return pl.pallas_call(
        _ctx_attn_kernel,
        grid=(BATCH,),
        in_specs=[
            pl.BlockSpec(memory_space=pltpu.SMEM),
            pl.BlockSpec((SEQ, D_IN_EVEN), lambda b: (b, 0)),
        ],
        out_specs=pl.BlockSpec((SEQ, D), lambda b: (b, 0)),
        out_shape=jax.ShapeDtypeStruct((NT, D), F32),
        compiler_params=_cp(("arbitrary",)),
        name="ctx_attn",
    )(sink, p)


NA_QROWS = 8
NA_BAND = 16
NA_TQ = NA_QROWS * GRID_W
NA_TK = NA_BAND * GRID_W
NA_RB = (DEC_SEQ // GRID_W) // NA_QROWS


def _na_band_start(rb):
    rows = DEC_SEQ // GRID_W
    return int(np.clip(NA_QROWS * rb - NA_WIN_R // 2, 0, rows - NA_BAND))


def _na_row_index():
    rows = DEC_SEQ // GRID_W
    n_dr = 2 * NA_WIN_R - 1
    out = np.full((3, NA_QROWS, NA_BAND), n_dr, np.int32)
    for ci, rb in enumerate((0, 1, NA_RB - 1)):
        for ql in range(NA_QROWS):
            qrow = NA_QROWS * rb + ql
            r0 = int(np.clip(qrow - NA_WIN_R // 2, 0, rows - NA_WIN_R))
            for kl in range(NA_BAND):
                krow = _na_band_start(rb) + kl
                if r0 <= krow < r0 + NA_WIN_R:
                    out[ci, ql, kl] = krow - qrow + (NA_WIN_R - 1)
    return out


def _na_bias_table(rel_bias):
    h = rel_bias.shape[0]
    n_dr, n_dc = 2 * NA_WIN_R - 1, 2 * NA_WIN_C - 1
    cols = np.arange(GRID_W)
    c0 = np.clip(cols - NA_WIN_C // 2, 0, GRID_W - NA_WIN_C)
    vc = (cols[None, :] >= c0[:, None]) & (cols[None, :] < c0[:, None] + NA_WIN_C)
    dc = cols[None, :] - cols[:, None] + (NA_WIN_C - 1)
    sel = ((np.arange(n_dc)[:, None, None] == dc[None]) & vc[None]).astype(np.float32)
    bc = jnp.dot(rel_bias.reshape(h * n_dr, n_dc), jnp.asarray(sel.reshape(n_dc, -1)),
                 precision=lax.Precision.HIGHEST).reshape(h, n_dr, GRID_W, GRID_W)
    bc = jnp.where(jnp.asarray(vc), bc, MASK)
    bc = jnp.concatenate([bc, jnp.full((h, 1, GRID_W, GRID_W), MASK, F32)], axis=1)
    tab = jnp.take(bc, jnp.asarray(_na_row_index()), axis=1)
    return tab.transpose(0, 1, 2, 4, 3, 5).reshape(h // 2, 2, 3, NA_TQ, NA_TK)


def _na_kernel(q_ref, k_ref, v_ref, kc_ref, vc_ref, bias_ref, oin_ref, o_ref):
    del oin_ref
    rb = pl.program_id(2)
    scale = HEAD_DIM ** -0.5
    start = jnp.clip(NA_QROWS * rb - NA_WIN_R // 2, 0, DEC_SEQ // GRID_W - NA_BAND) * GRID_W
    start = pl.multiple_of(start, 256)
    case = jnp.where(rb == 0, 0, jnp.where(rb == NA_RB - 1, 2, 1))
    kb = k_ref[pl.ds(start, NA_TK), :].astype(BF16)
    vb = v_ref[pl.ds(start, NA_TK), :].astype(BF16)
    q = q_ref[...] * scale
    lane = lax.broadcasted_iota(jnp.int32, (1, 128), 1)
    hs = range(2)
    s_loc = [_dot_nt(jnp.where((lane >> 6) == hh, q, 0.0).astype(BF16), kb) + bias_ref[hh, case] for hh in hs]
    s_ctx = [_dot_nt(q[:, 64 * hh:64 * hh + 64].astype(BF16), kc_ref[hh].astype(BF16)) for hh in hs]
    m = [jnp.maximum(jnp.max(s_loc[hh], axis=-1, keepdims=True), jnp.max(s_ctx[hh], axis=-1, keepdims=True))
         for hh in hs]
    p_loc = [jnp.exp(s_loc[hh] - m[hh]) for hh in hs]
    p_ctx = [jnp.exp(s_ctx[hh] - m[hh]) for hh in hs]
    inv = [1.0 / (jnp.sum(p_loc[hh], axis=-1, keepdims=True) + jnp.sum(p_ctx[hh], axis=-1, keepdims=True))
           for hh in hs]
    o_loc = [_dot(p_loc[hh].astype(BF16), vb) * inv[hh] for hh in hs]
    o_ctx = [_dot(p_ctx[hh].astype(BF16), vc_ref[hh].astype(BF16)) * inv[hh] for hh in hs]
    o_ref[...] = jnp.where((lane >> 6) == 0, o_loc[0], o_loc[1]) + jnp.concatenate(o_ctx, axis=-1)


def _na_attn(p, cache_k, cache_v, bias_tab, e, o_full):
    n_hp = NA_HEADS // 2
    lat0 = NT_CTX // NA_TQ
    per_b = DEC_SEQ // NA_TQ
    return pl.pallas_call(
        _na_kernel,
        grid=(n_hp, DEC_BATCH, NA_RB),
        in_specs=[
            pl.BlockSpec((NA_TQ, 128), lambda hp, b, rb: (lat0 + b * per_b + rb, _QA // 128 + hp)),
            pl.BlockSpec((DEC_SEQ, 128), lambda hp, b, rb: (1 + b, _KA // 128 + hp)),
            pl.BlockSpec((DEC_SEQ, 128), lambda hp, b, rb: (1 + b, _VA // 128 + hp)),
            pl.BlockSpec((None, None, 2, PAST_LEN, HEAD_DIM), lambda hp, b, rb: (b, e, hp, 0, 0)),
            pl.BlockSpec((None, None, 2, PAST_LEN, HEAD_DIM), lambda hp, b, rb: (b, e, hp, 0, 0)),
            pl.BlockSpec((None, 2, 3, NA_TQ, NA_TK), lambda hp, b, rb: (hp, 0, 0, 0, 0)),
            pl.BlockSpec(memory_space=pl.ANY),
        ],
        out_specs=pl.BlockSpec((NA_TQ, 128), lambda hp, b, rb: (lat0 + b * per_b + rb, hp)),
        out_shape=jax.ShapeDtypeStruct((NT, D), F32),
        input_output_aliases={6: 0},
        compiler_params=_cp(("arbitrary", "arbitrary", "arbitrary")),
        name="na_attn",
    )(p, p, p, cache_k, cache_v, bias_tab, o_full)


SWA_TQ = 256
SWA_TK = SWA_TQ + 2 * SWA_WINDOW


def _rope_tables():
    pos = np.arange(DEC_SEQ)
    row = (pos // GRID_W).astype(np.float32)
    col = (pos % GRID_W).astype(np.float32)
    half = HEAD_DIM // 2
    inv = jnp.asarray(ROPE_BASE, F32) ** (-jnp.arange(0, half, 2, dtype=F32) / half)
    ar = jnp.asarray(row)[:, None] * inv[None, :]
    ac = jnp.asarray(col)[:, None] * inv[None, :]
    cos = jnp.concatenate([jnp.cos(ar), jnp.cos(ar), jnp.cos(ac), jnp.cos(ac)], axis=1)
    sin = jnp.concatenate([-jnp.sin(ar), jnp.sin(ar), -jnp.sin(ac), jnp.sin(ac)], axis=1)
    return jnp.tile(cos, (1, 2)), jnp.tile(sin, (1, 2))


def _rope128(t, cos, sin):
    lane = lax.broadcasted_iota(jnp.int32, (1, 128), 1)
    up = pltpu.roll(t, 112, 1)
    dn = pltpu.roll(t, 16, 1)
    sw = jnp.where((lane & 31) < 16, up, dn)
    return t * cos + sw * sin


def _swa_mask_table():
    out = []
    for qb in (0, 1, DEC_SEQ // SWA_TQ - 1):
        q0 = qb * SWA_TQ
        bs = int(np.clip(q0 - SWA_WINDOW, 0, DEC_SEQ - SWA_TK))
        d = (q0 + np.arange(SWA_TQ))[:, None] - (bs + np.arange(SWA_TK))[None, :]
        out.append(np.where(np.abs(d) <= SWA_WINDOW, 0.0, MASK))
    return np.stack(out).astype(np.float32)


def _swa_kernel(sink_ref, q_ref, k_ref, v_ref, kc_ref, vc_ref, cos_ref, sin_ref, mask_ref, oin_ref, o_ref,
                kr_ref):
    del oin_ref
    qb = pl.program_id(1)
    scale = HEAD_DIM ** -0.5
    g = SWA_HEADS // SWA_KV_HEADS

    @pl.when(qb == 0)
    def _():
        kr_ref[...] = _rope128(k_ref[...], cos_ref[...], sin_ref[...]).astype(BF16)

    q0 = pl.multiple_of(qb * SWA_TQ, SWA_TQ)
    bs = pl.multiple_of(jnp.clip(q0 - SWA_WINDOW, 0, DEC_SEQ - SWA_TK), 128)
    cq = cos_ref[pl.ds(q0, SWA_TQ), :]
    sq = sin_ref[pl.ds(q0, SWA_TQ), :]
    kb = kr_ref[pl.ds(bs, SWA_TK), :]
    vb = v_ref[pl.ds(bs, SWA_TK), :].astype(BF16)
    qr = [_rope128(q_ref[:, 128 * j:128 * j + 128], cq, sq) * scale for j in range(SWA_HEADS // 2)]
    case = jnp.where(qb == 0, 0, jnp.where(qb == pl.num_programs(1) - 1, 2, 1))
    wmask = mask_ref[case]
    rowc = lax.broadcasted_iota(jnp.int32, (g * SWA_TQ, 1), 0) >> 8
    lane_half = lax.broadcasted_iota(jnp.int32, (1, 128), 1) >> 6

    kvs = range(SWA_KV_HEADS)
    q4, sink = [], []
    for kvh in kvs:
        heads = []
        sk = jnp.zeros((g * SWA_TQ, 1), F32)
        for gi in range(g):
            hq = kvh * g + gi
            heads.append(qr[hq // 2][:, 64 * (hq % 2):64 * (hq % 2) + 64])
            sk = jnp.where(rowc == gi, sink_ref[hq], sk)
        q4.append(jnp.concatenate(heads, axis=0).astype(BF16))
        sink.append(sk)
    s_loc = [(_dot_nt(q4[kvh], kb[:, 64 * kvh:64 * kvh + 64]).reshape(g, SWA_TQ, SWA_TK)
              + wmask[None]).reshape(g * SWA_TQ, SWA_TK) for kvh in kvs]
    s_ctx = [_dot_nt(q4[kvh], kc_ref[kvh].astype(BF16)) for kvh in kvs]
    m = [jnp.maximum(jnp.max(jnp.maximum(s_loc[kvh], s_ctx[kvh]), axis=-1, keepdims=True), sink[kvh])
         for kvh in kvs]
    p_loc = [jnp.exp(s_loc[kvh] - m[kvh]).astype(BF16) for kvh in kvs]
    p_ctx = [jnp.exp(s_ctx[kvh] - m[kvh]).astype(BF16) for kvh in kvs]
    pv = [_dot(p_loc[kvh], jnp.where(lane_half == kvh, vb, jnp.ones_like(vb)))
          + _dot(p_ctx[kvh], vc_ref[kvh].astype(BF16)) for kvh in kvs]
    for kvh in kvs:
        lo, hi = 64 * kvh, 64 * (1 - kvh)
        den = pv[kvh][:, hi:hi + 1] + jnp.exp(sink[kvh] - m[kvh])
        o = pv[kvh][:, lo:lo + 64] / den
        for gi in range(g):
            hq = kvh * g + gi
            o_ref[:, 64 * hq:64 * hq + 64] = o[SWA_TQ * gi:SWA_TQ * (gi + 1), :]


def _swa_value_ext(cache_v):
    ones = jnp.ones_like(cache_v[..., 0, :, :])
    return jnp.stack([jnp.concatenate([cache_v[..., 0, :, :], ones], axis=-1),
                      jnp.concatenate([ones, cache_v[..., 1, :, :]], axis=-1)], axis=-3)


def _swa_attn(p, cache_k, cache_v_ext, sink, cos, sin, e, o_full):
    assert SWA_TQ == 256
    lat0 = NT_CTX // SWA_TQ
    per_b = DEC_SEQ // SWA_TQ
    return pl.pallas_call(
        _swa_kernel,
        grid=(DEC_BATCH, per_b),
        in_specs=[
            pl.BlockSpec(memory_space=pltpu.SMEM),
            pl.BlockSpec((SWA_TQ, 512), lambda b, qb: (lat0 + b * per_b + qb, _QB // 512)),
            pl.BlockSpec((DEC_SEQ, 128), lambda b, qb: (1 + b, _KB // 128)),
            pl.BlockSpec((DEC_SEQ, 128), lambda b, qb: (1 + b, _VB // 128)),
            pl.BlockSpec((None, None, 2, PAST_LEN, HEAD_DIM), lambda b, qb: (b, e, 0, 0, 0)),
            pl.BlockSpec((None, None, 2, PAST_LEN, 128), lambda b, qb: (b, e, 0, 0, 0)),
            pl.BlockSpec((DEC_SEQ, 128), lambda b, qb: (0, 0)),
            pl.BlockSpec((DEC_SEQ, 128), lambda b, qb: (0, 0)),
            pl.BlockSpec((3, SWA_TQ, SWA_TK), lambda b, qb: (0, 0, 0)),
            pl.BlockSpec(memory_space=pl.ANY),
        ],
        out_specs=pl.BlockSpec((SWA_TQ, 512), lambda b, qb: (lat0 + b * per_b + qb, 1)),
        out_shape=jax.ShapeDtypeStruct((NT, D), F32),
        scratch_shapes=[pltpu.VMEM((DEC_SEQ, 128), BF16)],
        input_output_aliases={9: 0},
        compiler_params=_cp(("arbitrary", "arbitrary")),
        name="swa_attn",
    )(sink, p, p, p, cache_k, cache_v_ext, cos, sin, jnp.asarray(_swa_mask_table()), o_full)


CONV_TT = 256
CONV_HALO = 16
CONV_RC = 32


def _conv_kernel(u_ref, ul_ref, ur_ref, w_ref, cb_ref, lg_ref, lb_ref, o_ref, xs_ref):
    i = pl.program_id(0)
    n_ctx_tiles = NT_CTX // CONV_TT
    per_seq = DEC_SEQ // CONV_TT
    j = i - n_ctx_tiles
    is_ctx = i < n_ctx_tiles
    is_start = is_ctx | ((j % per_seq) == 0)
    is_end = is_ctx | ((j % per_seq) == per_seq - 1)

    def glu(u):
        return u[:, :CONV_CH] * _sigmoid(u[:, CONV_CH:])

    xs_ref[0, CONV_HALO:CONV_HALO + CONV_TT, :] = glu(u_ref[...])
    xs_ref[0, 0:CONV_HALO, :] = jnp.where(is_start, 0.0, glu(ul_ref[...]))
    xs_ref[0, CONV_HALO + CONV_TT:, :] = jnp.where(is_end, 0.0, glu(ur_ref[...]))
    n_sh = CONV_TT + 2 * CONV_HALO - 8
    for b in range(1, 8):
        xs_ref[b, 0:n_sh, :] = xs_ref[0, b:b + n_sh, :]

    pad = CONV_WIDTH // 2
    for c in range(CONV_TT // CONV_RC):
        base = CONV_HALO + c * CONV_RC - pad
        acc = jnp.zeros((CONV_RC, CONV_CH), F32)
        for t in range(CONV_WIDTH):
            off = base + t
            acc = acc + xs_ref[off % 8, off - off % 8:off - off % 8 + CONV_RC, :] * w_ref[t:t + 1, :]
        y = acc + cb_ref[...]
        mu = jnp.mean(y, axis=-1, keepdims=True)
        yc = y - mu
        var = jnp.mean(yc * yc, axis=-1, keepdims=True)
        yn = yc * lax.rsqrt(var + EPS) * lg_ref[...] + lb_ref[...]
        o_ref[c * CONV_RC:(c + 1) * CONV_RC, :] = yn * _sigmoid(yn)


def _conv_module(p, conv_w, conv_b, ln_g, ln_b):
    nh = CONV_TT // CONV_HALO
    last = NT // CONV_HALO - 1
    return pl.pallas_call(
        _conv_kernel,
        grid=(NT // CONV_TT,),
        in_specs=[
            pl.BlockSpec((CONV_TT, 2 * CONV_CH), lambda i: (i, 0)),
            pl.BlockSpec((CONV_HALO, 2 * CONV_CH), lambda i: (jnp.maximum(i * nh - 1, 0), 0)),
            pl.BlockSpec((CONV_HALO, 2 * CONV_CH), lambda i: (jnp.minimum((i + 1) * nh, last), 0)),
            pl.BlockSpec((CONV_WIDTH, CONV_CH), lambda i: (0, 0)),
            pl.BlockSpec((1, CONV_CH), lambda i: (0, 0)),
            pl.BlockSpec((1, CONV_CH), lambda i: (0, 0)),
            pl.BlockSpec((1, CONV_CH), lambda i: (0, 0)),
        ],
        out_specs=pl.BlockSpec((CONV_TT, CONV_CH), lambda i: (i, 0)),
        out_shape=jax.ShapeDtypeStruct((NT, D), F32),
        scratch_shapes=[pltpu.VMEM((8, CONV_TT + 2 * CONV_HALO, CONV_CH), F32)],
        compiler_params=_cp(("arbitrary",)),
        name="conv_module",
    )(p, p, p, conv_w, conv_b.reshape(1, -1), ln_g.reshape(1, -1), ln_b.reshape(1, -1))


_U, _HQ, _HI, _HFF, _HFB, _HGT = 0, 1024, 1536, 2048, 2560, 3072
HG_LEVELS = 6
HG_W = HG_HEADS * HG_DK


def _hgrn_constants():
    c = HG_CHUNK
    mm = np.zeros((HG_LEVELS + 1, c, c), np.float32)
    for lv in range(HG_LEVELS):
        n = c >> lv
        half = n // 2
        for t in range(c):
            for s in range(c):
                if t // n == s // n and t % n >= half and s % n < half:
                    mm[lv, t, s] = 1.0
    mm[HG_LEVELS] = np.eye(c, dtype=np.float32)
    tri = np.tril(np.ones((c, c), np.float32))
    return np.stack([tri, tri.T]), np.stack([mm, mm[:, ::-1, ::-1]])


def _hgrn_level_refs(cum, cum_ref, fwd):
    c = HG_CHUNK
    cum_ref[...] = cum
    refs = []
    for lv in range(HG_LEVELS - 2):
        n = c >> lv
        rows = [b * n + n // 2 - (1 if fwd else 0) for b in range(c // n)]
        refs.append(jnp.concatenate([jnp.broadcast_to(cum_ref[r:r + 1, :], (n, HG_W)) for r in rows], axis=0))
    row = lax.broadcasted_iota(jnp.int32, (c, HG_W), 0)
    for n in (4, 2):
        p = n // 2 - (1 if fwd else 0)
        res = row & (n - 1)
        ref = None
        for j in range(n):
            shift = (j - p) % c
            cand = cum if shift == 0 else pltpu.roll(cum, shift, 0)
            ref = cand if ref is None else jnp.where(res == j, cand, ref)
        refs.append(ref)
    return refs


def _hgrn_gates(q, f, lb, tri):
    qs = q * _sigmoid(q)
    g = jnp.log(lb + (1.0 - lb) * _sigmoid(f))
    k = (1.0 - lb) * _sigmoid(-f)
    g_hi = g.astype(BF16)
    r1 = g - g_hi.astype(F32)
    g_mid = r1.astype(BF16)
    g_lo = (r1 - g_mid.astype(F32)).astype(BF16)
    cum = _dot(tri, g_hi) + (_dot(tri, g_mid) + _dot(tri, g_lo))
    return qs, k, cum


def _hgrn_operands(qs, k, v, cum, cum_ref, fwd):
    c = HG_CHUNK
    refs = _hgrn_level_refs(cum, cum_ref, fwd)
    end_row = c - 1 if fwd else 0
    ecum = jnp.exp(cum)
    eend = jnp.exp(-jnp.abs(cum - jnp.broadcast_to(cum_ref[end_row:end_row + 1, :], (c, HG_W))))
    qk = [(qs.astype(BF16), k.astype(BF16))]
    for ref in refs:
        el = jnp.exp(-jnp.abs(cum - ref))
        qk.append(((qs * el).astype(BF16), (k * el).astype(BF16)))
    return dict(qk=qk, vb=v.astype(BF16), qc=(qs * ecum).astype(BF16), kend=(k * eend).astype(BF16),
                dec=ecum[end_row:end_row + 1, :])


def _hgrn_heads(ops, masks, st_ref):
    outs = []
    for h in range(HG_HEADS):
        sl = slice(h * HG_DK, (h + 1) * HG_DK)
        a = masks[HG_LEVELS] * _dot_nt(ops["qk"][0][0][:, sl], ops["qk"][0][1][:, sl])
        for lv in range(HG_LEVELS):
            a = a + masks[lv] * _dot_nt(ops["qk"][lv + 1][0][:, sl], ops["qk"][lv + 1][1][:, sl])
        s_t = st_ref[h]
        outs.append(_dot(a.astype(BF16), ops["vb"][:, sl]) + _dot_nt(ops["qc"][:, sl], s_t.astype(BF16)))
        st_ref[h] = s_t * ops["dec"][:, sl] + _dot_tn(ops["vb"][:, sl], ops["kend"][:, sl])
    return jnp.concatenate(outs, axis=1)


def _hgrn_kernel(qf_ref, if_ref, ff_ref, qb_ref, ib_ref, fb_ref, lb_ref, s0_ref, tri_ref, m_ref,
                 of_ref, ob_ref, sfin_ref, stf_ref, stb_ref, cumf_ref, cumb_ref, *, tile):
    c = HG_CHUNK
    nct = tile // c
    tt = pl.program_id(1)

    @pl.when(tt == 0)
    def _():
        stf_ref[...] = s0_ref[0]
        stb_ref[...] = s0_ref[1]

    lbf = lb_ref[0:1, :]
    lbb = lb_ref[1:2, :]

    masks_f = [m_ref[0, x] for x in range(HG_LEVELS + 1)]
    masks_b = [m_ref[1, x] for x in range(HG_LEVELS + 1)]

    def body(pi, carry):
        rows = []
        for u in range(2):
            ci = 2 * pi + u
            rows.append((pl.multiple_of(ci * c, c), pl.multiple_of((nct - 1 - ci) * c, c)))
        gates = [(_hgrn_gates(qf_ref[pl.ds(rf, c), :], ff_ref[pl.ds(rf, c), :], lbf, tri_ref[0]),
                  _hgrn_gates(qb_ref[pl.ds(rb, c), :], fb_ref[pl.ds(rb, c), :], lbb, tri_ref[1]))
                 for rf, rb in rows]
        ops = [(_hgrn_operands(*gates[u][0][:2], if_ref[pl.ds(rf, c), :], gates[u][0][2], cumf_ref.at[u], True),
                _hgrn_operands(*gates[u][1][:2], ib_ref[pl.ds(rb, c), :], gates[u][1][2], cumb_ref.at[u], False))
               for u, (rf, rb) in enumerate(rows)]
        for u, (rf, rb) in enumerate(rows):
            o_f = _hgrn_heads(ops[u][0], masks_f, stf_ref)
            o_b = _hgrn_heads(ops[u][1], masks_b, stb_ref)
            of_ref[pl.ds(rf, c), :] = o_f
            ob_ref[pl.ds(rb, c), :] = o_b
        return carry

    lax.fori_loop(0, nct // 2, body, 0)

    @pl.when(tt == pl.num_programs(1) - 1)
    def _():
        sfin_ref[0] = stf_ref[...]
        sfin_ref[1] = stb_ref[...]


def _hgrn_scan(p, lb, s0_t, n_batch, t_len, row0):
    tri, mm = _hgrn_constants()
    tri = jnp.asarray(tri, BF16)
    mm = jnp.asarray(mm, F32)
    tile = min(t_len, 512)
    ntt = t_len // tile
    r0 = row0 // tile

    def fwd(c0):
        return pl.BlockSpec((tile, HG_W), lambda b, t: (r0 + b * ntt + t, c0 // HG_W))

    def bwd(c0):
        return pl.BlockSpec((tile, HG_W), lambda b, t: (r0 + b * ntt + ntt - 1 - t, c0 // HG_W))

    st_spec = pl.BlockSpec((None, 2, HG_HEADS, 128, 128), lambda b, t: (b, 0, 0, 0, 0))
    n_rows = n_batch * t_len
    return pl.pallas_call(
        functools.partial(_hgrn_kernel, tile=tile),
        grid=(n_batch, ntt),
        in_specs=[
            fwd(_HQ), fwd(_HI), fwd(_HFF), bwd(_HQ), bwd(_HI), bwd(_HFB),
            pl.BlockSpec((2, HG_W), lambda b, t: (0, 0)),
            st_spec,
            pl.BlockSpec((2, HG_CHUNK, HG_CHUNK), lambda b, t: (0, 0, 0)),
            pl.BlockSpec((2, HG_LEVELS + 1, HG_CHUNK, HG_CHUNK), lambda b, t: (0, 0, 0, 0)),
        ],
        out_specs=[
            pl.BlockSpec((tile, HG_W), lambda b, t: (b * ntt + t, 0)),
            pl.BlockSpec((tile, HG_W), lambda b, t: (b * ntt + ntt - 1 - t, 0)),
            st_spec,
        ],
        out_shape=[jax.ShapeDtypeStruct((n_rows, HG_W), F32),
                   jax.ShapeDtypeStruct((n_rows, HG_W), F32),
                   jax.ShapeDtypeStruct((n_batch, 2, HG_HEADS, 128, 128), F32)],
        scratch_shapes=[pltpu.VMEM((HG_HEADS, 128, 128), F32), pltpu.VMEM((HG_HEADS, 128, 128), F32),
                        pltpu.VMEM((2, HG_CHUNK, HG_W), F32), pltpu.VMEM((2, HG_CHUNK, HG_W), F32)],
        compiler_params=_cp(("arbitrary", "arbitrary")),
        name="hgrn_scan",
    )(p, p, p, p, p, p, lb, s0_t, tri, mm)


def _hgrn_fin_kernel(of_ref, ob_ref, gt_ref, ng_ref, oin_ref, o_ref):
    del oin_ref
    od = of_ref[...] + ob_ref[...]
    gt = gt_ref[...]
    gate = gt * _sigmoid(gt)
    for h in range(HG_HEADS):
        sl = slice(h * HG_DK, (h + 1) * HG_DK)
        x = od[:, sl]
        ms = jnp.mean(x * x, axis=-1, keepdims=True)
        o_ref[:, sl] = x * lax.rsqrt(ms + EPS) * ng_ref[...] * gate[:, sl]


def _hgrn_finish(o_f, o_b, p, norm_g, o_full, row0, tm=512):
    n_rows = o_f.shape[0]
    r0 = row0 // tm
    return pl.pallas_call(
        _hgrn_fin_kernel,
        grid=(n_rows // tm,),
        in_specs=[
            pl.BlockSpec((tm, HG_W), lambda i: (i, 0)),
            pl.BlockSpec((tm, HG_W), lambda i: (i, 0)),
            pl.BlockSpec((tm, HG_W), lambda i: (r0 + i, _HGT // HG_W)),
            pl.BlockSpec((1, 128), lambda i: (0, 0)),
            pl.BlockSpec(memory_space=pl.ANY),
        ],
        out_specs=pl.BlockSpec((tm, HG_W), lambda i: (r0 + i, 1)),
        out_shape=jax.ShapeDtypeStruct((NT, D), F32),
        input_output_aliases={4: 0},
        compiler_params=_cp(("arbitrary",)),
        name="hgrn_finish",
    )(o_f, o_b, p, norm_g.reshape(1, 128), o_full)


def _hgrn(p, lb, norm_g, s0_t, o_full, n_batch, t_len, row0):
    o_f, o_b, s_fin = _hgrn_scan(p, lb, s0_t, n_batch, t_len, row0)
    return _hgrn_finish(o_f, o_b, p, norm_g, o_full, row0), s_fin


ROUTE_TM = 512


def _mmres_router_kernel(o_ref, x_ref, gate_ref, w_ref, g_ref, sh_ref, sc_ref, whi_ref, br_ref,
                         tri_ref, xo_ref, mi_ref, mf_ref, cnt_ref, carry_ref):
    i = pl.program_id(0)

    @pl.when(i == 0)
    def _():
        carry_ref[...] = jnp.zeros_like(carry_ref)

    xn = x_ref[...] + gate_ref[...] * _dot(o_ref[...].astype(BF16), w_ref[...])
    xo_ref[...] = xn
    h = _modulate(xn, g_ref[...], sh_ref[...], sc_ref[...])
    h_hi, h_lo = _split2(h)
    hw = _dot(h_hi, whi_ref[...])
    logits = hw[:, :128] + (hw[:, 128:] + _dot(h_lo, whi_ref[:, :128])) + br_ref[...]
    lane = lax.broadcasted_iota(jnp.int32, logits.shape, 1)
    neg = jnp.float32(-3e38)
    is_g = lane < N_GROUPS
    lg = jnp.where(is_g, logits, neg)
    mg = jnp.max(lg, axis=-1, keepdims=True)
    grp = jnp.min(jnp.where(lg == mg, lane, 128), axis=-1, keepdims=True)
    pg = 1.0 / jnp.sum(jnp.where(is_g, jnp.exp(lg - mg), 0.0), axis=-1, keepdims=True)
    ex = lane - N_GROUPS
    in_grp = (ex >= 0) & (ex < N_EXPERTS) & ((ex >> 3) == grp)
    le = jnp.where(in_grp, logits, neg)
    v1 = jnp.max(le, axis=-1, keepdims=True)
    i1 = jnp.min(jnp.where(le == v1, lane, 128), axis=-1, keepdims=True)
    le2 = jnp.where(lane == i1, neg, le)
    v2 = jnp.max(le2, axis=-1, keepdims=True)
    i2 = jnp.min(jnp.where(le2 == v2, lane, 128), axis=-1, keepdims=True)
    t = jnp.exp(v2 - v1)
    g1 = pg / (1.0 + t)
    g2 = pg * t / (1.0 + t)
    oh = jnp.where((lane == i1) | (lane == i2), 1.0, 0.0)
    prefix = _dot(tri_ref[...], oh.astype(BF16)) + carry_ref[...]
    r1 = jnp.sum(jnp.where(lane == i1, prefix, 0.0), axis=-1, keepdims=True).astype(jnp.int32)
    r2 = jnp.sum(jnp.where(lane == i2, prefix, 0.0), axis=-1, keepdims=True).astype(jnp.int32)
    carry_ref[...] = carry_ref[...] + jnp.sum(oh, axis=0, keepdims=True)
    cnt_ref[...] = carry_ref[...]
    mi_ref[...] = jnp.where(lane == 0, i1 - N_GROUPS, jnp.where(lane == 1, i2 - N_GROUPS,
                            jnp.where(lane == 2, r1, jnp.where(lane == 3, r2, 0))))
    mf_ref[...] = jnp.where(lane == 0, g1, jnp.where(lane == 1, g2, 0.0))


def _mmres_router(o, x, mod, l, w_out_bf16, g, wg, bg, we, be):
    tm = ROUTE_TM
    wr = jnp.zeros((D, 128), F32).at[:, :N_GROUPS].set(wg).at[:, N_GROUPS:N_GROUPS + N_EXPERTS].set(we)
    br = jnp.zeros((1, 128), F32).at[0, :N_GROUPS].set(bg).at[0, N_GROUPS:N_GROUPS + N_EXPERTS].set(be)
    w_hi = wr.astype(BF16)
    w_lo = (wr - w_hi.astype(F32)).astype(BF16)
    tri = jnp.asarray(np.tril(np.ones((tm, tm), np.float32), -1), BF16)
    const = lambda i: (0, 0)
    return pl.pallas_call(
        _mmres_router_kernel,
        grid=(NT // tm,),
        in_specs=[
            pl.BlockSpec((tm, D), lambda i: (i, 0)),
            pl.BlockSpec((tm, D), lambda i: (i, 0)),
            _mod_spec(l, 2, tm),
            pl.BlockSpec((D, D), const),
            pl.BlockSpec((1, D), const),
            _mod_spec(l, 3, tm),
            _mod_spec(l, 4, tm),
            pl.BlockSpec((D, 256), const),
            pl.BlockSpec((1, 128), const),
            pl.BlockSpec((tm, tm), const),
        ],
        out_specs=[
            pl.BlockSpec((tm, D), lambda i: (i, 0)),
            pl.BlockSpec((tm, 128), lambda i: (i, 0)),
            pl.BlockSpec((tm, 128), lambda i: (i, 0)),
            pl.BlockSpec((1, 128), const),
        ],
        out_shape=[
            jax.ShapeDtypeStruct((NT, D), F32),
            jax.ShapeDtypeStruct((NT, 128), jnp.int32),
            jax.ShapeDtypeStruct((NT, 128), F32),
            jax.ShapeDtypeStruct((1, 128), F32),
        ],
        scratch_shapes=[pltpu.VMEM((1, 128), F32)],
        compiler_params=_cp(("arbitrary",)),
        name="mmres_router",
    )(o, x, mod, w_out_bf16, g.reshape(1, D), mod, mod, jnp.concatenate([w_hi, w_lo], axis=1), br, tri)


DISP_TM = 512


def _row_copy(src, s_row, dst, d_row, sem):
    return pltpu.make_async_copy(src.at[pl.ds(s_row, 1)], dst.at[pl.ds(d_row, 1)], sem)


def _dispatch_kernel(dest_ref, x_ref, g_ref, sh_ref, sc_ref, xs_hbm, h_ref, sem):
    h_ref[...] = _modulate(x_ref[...], g_ref[...], sh_ref[...], sc_ref[...])

    def group(t8, start):
        base = pl.multiple_of(t8 * 8, 8)
        for s in range(8):
            for k in range(2):
                cp = _row_copy(h_ref, base + s, xs_hbm, dest_ref[k, base + s], sem)
                if start:
                    cp.start(priority=k)
                else:
                    cp.wait()

    lax.fori_loop(0, DISP_TM // 8, lambda t8, c: (group(t8, True), c)[1], 0)
    lax.fori_loop(0, DISP_TM // 8, lambda t8, c: (group(t8, False), c)[1], 0)


def _dispatch(dest_t, x, g, mod, l):
    return pl.pallas_call(
        _dispatch_kernel,
        grid=(NT // DISP_TM,),
        in_specs=[
            pl.BlockSpec((2, DISP_TM), lambda i: (0, i), memory_space=pltpu.SMEM),
            pl.BlockSpec((DISP_TM, D), lambda i: (i, 0)),
            pl.BlockSpec((1, D), lambda i: (0, 0)),
            _mod_spec(l, 3, DISP_TM),
            _mod_spec(l, 4, DISP_TM),
        ],
        out_specs=pl.BlockSpec(memory_space=pl.ANY),
        out_shape=jax.ShapeDtypeStruct((MOE_SLOTS, D), F32),
        scratch_shapes=[pltpu.VMEM((DISP_TM, D), F32), pltpu.SemaphoreType.DMA(())],
        compiler_params=_cp(("arbitrary",)),
        name="moe_dispatch",
    )(dest_t, x, g.reshape(1, D), mod, mod)


def _expert_kernel(bexp_ref, bval_ref, bfirst_ref, xs_ref, w1_ref, w3_ref, w2_ref, y_ref,
                   w1b_ref, w3b_ref, w2b_ref):
    del bexp_ref
    i = pl.program_id(0)
    nv = bval_ref[i]

    @pl.when(bfirst_ref[i] == 1)
    def _():
        w1b_ref[...] = w1_ref[...].astype(BF16)
        w3b_ref[...] = w3_ref[...].astype(BF16)
        w2b_ref[...] = w2_ref[...].astype(BF16)

    @pl.when(nv > 0)
    def _():
        row = lax.broadcasted_iota(jnp.int32, (MOE_MB, 1), 0)
        xb = jnp.where(row < nv, xs_ref[...], 0.0).astype(BF16)
        a = _dot(xb, w1b_ref[...])
        b = _dot(xb, w3b_ref[...])
        hid = (a * _sigmoid(a) * b).astype(BF16)
        y_ref[...] = _dot(hid, w2b_ref[...])

    @pl.when(nv <= 0)
    def _():
        y_ref[...] = jnp.zeros_like(y_ref)


def _experts(blk_exp, blk_valid, blk_first, xs, w1, w3, w2, l):
    grid_spec = pltpu.PrefetchScalarGridSpec(
        num_scalar_prefetch=3,
        grid=(MOE_NB,),
        in_specs=[
            pl.BlockSpec((MOE_MB, D), lambda i, be, bv, bf: (i, 0)),
            pl.BlockSpec((None, None, D, D_EXPERT), lambda i, be, bv, bf: (l, be[i], 0, 0)),
            pl.BlockSpec((None, None, D, D_EXPERT), lambda i, be, bv, bf: (l, be[i], 0, 0)),
            pl.BlockSpec((None, None, D_EXPERT, D), lambda i, be, bv, bf: (l, be[i], 0, 0)),
        ],
        out_specs=pl.BlockSpec((MOE_MB, D), lambda i, be, bv, bf: (i, 0)),
        scratch_shapes=[pltpu.VMEM((D, D_EXPERT), BF16), pltpu.VMEM((D, D_EXPERT), BF16),
                        pltpu.VMEM((D_EXPERT, D), BF16)],
    )
    return pl.pallas_call(
        _expert_kernel,
        grid_spec=grid_spec,
        out_shape=jax.ShapeDtypeStruct((MOE_SLOTS, D), F32),
        compiler_params=_cp(("arbitrary",)),
        name="moe_experts",
    )(blk_exp, blk_valid, blk_first, xs, w1, w3, w2)


COMB_TM = 256


def _combine_kernel(dest_ref, mf_ref, x_ref, gate_ref, y_hbm, o_ref, buf_ref, sem):
    def group(t8, start):
        base = pl.multiple_of(t8 * 8, 8)
        for s in range(8):
            for k in range(2):
                cp = _row_copy(y_hbm, dest_ref[k, base + s], buf_ref.at[k], base + s, sem)
                if start:
                    cp.start(priority=k)
                else:
                    cp.wait()

    lax.fori_loop(0, COMB_TM // 8, lambda t8, c: (group(t8, True), c)[1], 0)
    lax.fori_loop(0, COMB_TM // 8, lambda t8, c: (group(t8, False), c)[1], 0)
    mf = mf_ref[...]
    y = mf[:, 0:1] * buf_ref[0] + mf[:, 1:2] * buf_ref[1]
    o_ref[...] = x_ref[...] + gate_ref[...] * y


def _combine(dest_t, mf, x, mod, l, y):
    tm = COMB_TM
    return pl.pallas_call(
        _combine_kernel,
        grid=(NT // tm,),
        in_specs=[
            pl.BlockSpec((2, tm), lambda i: (0, i), memory_space=pltpu.SMEM),
            pl.BlockSpec((tm, 128), lambda i: (i, 0)),
            pl.BlockSpec((tm, D), lambda i: (i, 0)),
            _mod_spec(l, 5, tm),
            pl.BlockSpec(memory_space=pl.ANY),
        ],
        out_specs=pl.BlockSpec((tm, D), lambda i: (i, 0)),
        out_shape=jax.ShapeDtypeStruct((NT, D), F32),
        scratch_shapes=[pltpu.VMEM((2, tm, D), F32), pltpu.SemaphoreType.DMA(())],
        compiler_params=_cp(("arbitrary",)),
        name="moe_combine",
    )(dest_t, mf, x, mod, y)


def _comb_modmm_kernel(dcur_ref, dnxt_ref, mf_ref, x_ref, gate_ref, g_ref, sh_ref, sc_ref, w_ref, y_hbm,
                       xo_ref, p_ref, buf_ref, sem, *, nchunk):
    i = pl.program_id(0)
    n = pl.num_programs(0)
    slot = lax.rem(i, 2)
    tm = x_ref.shape[0]

    def rows(dest_ref, to_slot, lo, hi, start):
        for t in range(lo, hi):
            for k in range(2):
                cp = _row_copy(y_hbm, dest_ref[k, t], buf_ref.at[to_slot, k], t, sem.at[to_slot])
                if start:
                    cp.start(priority=k)
                else:
                    cp.wait()

    def rolled(dest_ref, to_slot, start):
        def body(t8, c):
            base = pl.multiple_of(t8 * 8, 8)
            for s in range(8):
                for k in range(2):
                    cp = _row_copy(y_hbm, dest_ref[k, base + s], buf_ref.at[to_slot, k], base + s,
                                   sem.at[to_slot])
                    if start:
                        cp.start(priority=k)
                    else:
                        cp.wait()
            return c
        lax.fori_loop(0, tm // 8, body, 0)

    @pl.when(i == 0)
    def _():
        rolled(dcur_ref, slot, True)

    rolled(dcur_ref, slot, False)
    mf = mf_ref[...]
    y = mf[:, 0:1] * buf_ref[slot, 0] + mf[:, 1:2] * buf_ref[slot, 1]
    xn = x_ref[...] + gate_ref[...] * y
    xo_ref[...] = xn
    h = _modulate(xn, g_ref[...], sh_ref[...], sc_ref[...]).astype(BF16)
    nout = p_ref.shape[1]
    n_ch = nout // nchunk
    per = -(-tm // n_ch)
    for c in range(n_ch):
        rows(dnxt_ref, 1 - slot, min(c * per, tm), min((c + 1) * per, tm), True)
        p_ref[:, c * nchunk:(c + 1) * nchunk] = _dot(h, w_ref[:, c * nchunk:(c + 1) * nchunk])

    @pl.when(i == n - 1)
    def _():
        rolled(dnxt_ref, 1 - slot, False)


def _comb_modmm(dest_t, mf, x, mod, l, y, g_next, w_bf16, tm=256):
    nout = w_bf16.shape[1]
    nt = NT // tm
    return pl.pallas_call(
        functools.partial(_comb_modmm_kernel, nchunk=256),
        grid=(nt,),
        in_specs=[
            pl.BlockSpec((2, tm), lambda i: (0, i), memory_space=pltpu.SMEM),
            pl.BlockSpec((2, tm), lambda i: (0, jnp.minimum(i + 1, nt - 1)), memory_space=pltpu.SMEM),
            pl.BlockSpec((tm, 128), lambda i: (i, 0)),
            pl.BlockSpec((tm, D), lambda i: (i, 0)),
            _mod_spec(l, 5, tm),
            pl.BlockSpec((1, D), lambda i: (0, 0)),
            _mod_spec(l + 1, 0, tm),
            _mod_spec(l + 1, 1, tm),
            pl.BlockSpec((D, nout), lambda i: (0, 0)),
            pl.BlockSpec(memory_space=pl.ANY),
        ],
        out_specs=[pl.BlockSpec((tm, D), lambda i: (i, 0)), pl.BlockSpec((tm, nout), lambda i: (i, 0))],
        out_shape=[jax.ShapeDtypeStruct((NT, D), F32), jax.ShapeDtypeStruct((NT, nout), F32)],
        scratch_shapes=[pltpu.VMEM((2, 2, tm, D), F32), pltpu.SemaphoreType.DMA((2,))],
        compiler_params=_cp(("arbitrary",)),
        name="moe_combine_modmm",
    )(dest_t, dest_t, mf, x, mod, g_next.reshape(1, D), mod, mod, w_bf16, y)


def _moe_plan(mi, counts):
    cnt = counts[0, N_GROUPS:N_GROUPS + N_EXPERTS].astype(jnp.int32)
    padded = (cnt + MOE_MB - 1) // MOE_MB * MOE_MB
    pad_ends = jnp.cumsum(padded)
    pad_starts = pad_ends - padded
    dest = pad_starts[mi[:, 0:2]] + mi[:, 2:4]
    blk0 = jnp.arange(MOE_NB, dtype=jnp.int32) * MOE_MB
    blk_exp = jnp.sum((pad_ends[None, :] <= blk0[:, None]).astype(jnp.int32), axis=1)
    blk_exp = jnp.minimum(blk_exp, N_EXPERTS - 1)
    blk_valid = jnp.clip(cnt[blk_exp] - (blk0 - pad_starts[blk_exp]), 0, MOE_MB)
    blk_valid = jnp.where(blk0 < pad_ends[-1], blk_valid, 0).astype(jnp.int32)
    blk_first = jnp.concatenate([jnp.ones((1,), jnp.int32),
                                 (blk_exp[1:] != blk_exp[:-1]).astype(jnp.int32)])
    return dest.T.astype(jnp.int32), blk_exp, blk_valid, blk_first


def _moe_experts(x, g, mod, l, mi, mf, counts, w1, w3, w2):
    dest_t, blk_exp, blk_valid, blk_first = _moe_plan(mi, counts)
    xs = _dispatch(dest_t, x, g, mod, l)
    y = _experts(blk_exp, blk_valid, blk_first, xs, w1, w3, w2, l)
    return dest_t, mf, y


def kernel(x_prompt, x_sample, cache_k_na, cache_v_na, cache_k_swa, cache_v_swa, state_hgrn, c, c_ctx, w_mod, b_mod, norm_mix_g, norm_ffn_g, w_in_even, w_out_even, na_rel_bias, swa_sink, w_in_odd, w_out_odd, conv_w, conv_b, conv_norm_g, conv_norm_b, hgrn_lb_raw, hgrn_norm_g, router_group_w, router_group_b, router_expert_w, router_expert_b, moe_w1, moe_w3, moe_w2, final_norm_g):
    x = jnp.concatenate([x_prompt.reshape(NT_CTX, D), x_sample.reshape(NT_LAT, D)], axis=0)
    cvec = jnp.zeros((16, D), F32).at[0].set(c_ctx).at[1:1 + DEC_BATCH].set(c)
    mod = _mod_table(cvec, w_mod, b_mod)

    lb_p = jax.nn.softmax(hgrn_lb_raw.astype(F32), axis=0)
    lower_bounds = jnp.cumsum(lb_p, axis=0) - lb_p[0:1]
    cos, sin = _rope_tables()
    cache_v_swa_ext = _swa_value_ext(cache_v_swa)

    k_na, v_na, k_swa, v_swa, s_hg = [], [], [], [], []
    moe = None
    for l in range(DEPTH):
        w_in = (w_in_even if l % 2 == 0 else w_in_odd)[l // 2].astype(BF16)
        if moe is None:
            p = _modmm(x, norm_mix_g[l], mod, l, w_in, tm=512)
        else:
            x, p = _comb_modmm(*moe[:2], x, mod, l - 1, moe[2], norm_mix_g[l], w_in)
        if l % 2 == 0:
            e = l // 2
            o = _ctx_attn(p, swa_sink[e])
            o = _na_attn(p, cache_k_na, cache_v_na, _na_bias_table(na_rel_bias[e]), e, o)
            o = _swa_attn(p, cache_k_swa, cache_v_swa_ext, swa_sink[e], cos, sin, e, o)
            pc = p[:NT_CTX]
            to_heads = lambda u, nh: u.reshape(BATCH, SEQ, nh, HEAD_DIM).transpose(0, 2, 1, 3)
            k_na.append(to_heads(pc[:, _KA:_KA + 512], NA_HEADS))
            v_na.append(to_heads(pc[:, _VA:_VA + 512], NA_HEADS))
            k_swa.append(to_heads(pc[:, _KB:_KB + 128], SWA_KV_HEADS))
            v_swa.append(to_heads(pc[:, _VB:_VB + 128], SWA_KV_HEADS))
            w_out = w_out_even[e]
        else:
            j = l // 2
            o = _conv_module(p, conv_w[j], conv_b[j], conv_norm_g[j], conv_norm_b[j])
            zero_state = jnp.zeros((BATCH, 2, HG_HEADS, 128, 128), F32)
            o, s_ctx = _hgrn(p, lower_bounds[j], hgrn_norm_g[j], zero_state, o, BATCH, SEQ, 0)
            s0_lat = jnp.swapaxes(state_hgrn[:, j], -1, -2)
            o, _ = _hgrn(p, lower_bounds[j], hgrn_norm_g[j], s0_lat, o, DEC_BATCH, DEC_SEQ, NT_CTX)
            s_hg.append(jnp.swapaxes(s_ctx, -1, -2))
            w_out = w_out_odd[j]
        x, mi, mf, counts = _mmres_router(o, x, mod, l, w_out.astype(BF16), norm_ffn_g[l],
                                          router_group_w[l], router_group_b[l],
                                          router_expert_w[l], router_expert_b[l])
        moe = _moe_experts(x, norm_ffn_g[l], mod, l, mi, mf, counts, moe_w1, moe_w3, moe_w2)

    x = _combine(*moe[:2], x, mod, DEPTH - 1, moe[2])
    y_prompt = _final_norm(x, final_norm_g, 0, NT_CTX).reshape(BATCH, SEQ, D)
    y_sample = _final_norm(x, final_norm_g, NT_CTX, NT_LAT).reshape(DEC_BATCH, DEC_SEQ, D)
    return (y_prompt, y_sample, jnp.stack(k_na, axis=1), jnp.stack(v_na, axis=1),
            jnp.stack(k_swa, axis=1), jnp.stack(v_swa, axis=1), jnp.stack(s_hg, axis=1))
```

```python
import functools

import numpy as np
import jax
import jax.numpy as jnp
from jax import lax
from jax.experimental import pallas as pl
from jax.experimental.pallas import tpu as pltpu

F32 = jnp.float32
BF16 = jnp.bfloat16

D = 1024
BATCH = 16
SEQ = 256
DEPTH = 4
DEC_BATCH = 8
DEC_SEQ = 4096
PAST_LEN = 512
GRID_W = 64
HEAD_DIM = 64
EPS = 1e-6
NA_HEADS = 8
NA_WIN_R = 8
NA_WIN_C = 16
SWA_HEADS = 8
SWA_KV_HEADS = 2
SWA_WINDOW = 128
ROPE_BASE = 10000.0
CONV_CH = 512
CONV_WIDTH = 31
HG_HEADS = 4
HG_DK = 128
HG_CHUNK = 64
N_GROUPS = 4
EXPERTS_PER_GROUP = 8
N_EXPERTS = 32
D_EXPERT = 512
N_EVEN = 2
N_ODD = 2
D_IN_EVEN = 2304
D_IN_ODD = 3584

NT_CTX = BATCH * SEQ
NT_LAT = DEC_BATCH * DEC_SEQ
NT = NT_CTX + NT_LAT
SEG = 4096
assert NT_CTX == SEG and DEC_SEQ == SEG
N_MOD = 1 + DEC_BATCH
MASK = -1e30

MOE_MB = 512
MOE_NB = (2 * NT) // MOE_MB + N_EXPERTS
MOE_SLOTS = MOE_NB * MOE_MB

VMEM_LIMIT = 56 * 1024 * 1024


def _cp(sem, vmem=VMEM_LIMIT):
    return pltpu.CompilerParams(dimension_semantics=sem, vmem_limit_bytes=vmem)


def _dot(a, b):
    return jnp.dot(a, b, preferred_element_type=F32)


def _dot_nt(a, b):
    return lax.dot_general(a, b, (((1,), (1,)), ((), ())), preferred_element_type=F32)


def _dot_tn(a, b):
    return lax.dot_general(a, b, (((0,), (0,)), ((), ())), preferred_element_type=F32)


def _split2(a):
    hi = a.astype(BF16)
    lo = (a - hi.astype(F32)).astype(BF16)
    return hi, lo


def _dot3(a, b_hi, b_lo):
    a_hi, a_lo = _split2(a)
    return _dot(a_hi, b_hi) + (_dot(a_hi, b_lo) + _dot(a_lo, b_hi))


def _sigmoid(x):
    return 1.0 / (1.0 + jnp.exp(-x))


def _modulate(x, g, shift, scale):
    ms = jnp.mean(x * x, axis=-1, keepdims=True)
    return (x * lax.rsqrt(ms + EPS) * g) * (1.0 + scale) + shift


def _mod_kernel(c_ref, whi_ref, wlo_ref, b_ref, o_ref):
    cv = c_ref[...]
    s = cv * _sigmoid(cv)
    o_ref[...] = _dot3(s, whi_ref[...], wlo_ref[...]) + b_ref[...]


def _mod_table(cvec, w_mod, b_mod):
    tn = 1536
    w_hi = w_mod.astype(BF16)
    w_lo = (w_mod - w_hi.astype(F32)).astype(BF16)
    out = pl.pallas_call(
        _mod_kernel,
        grid=(DEPTH, 6 * D // tn),
        in_specs=[
            pl.BlockSpec((16, D), lambda l, j: (0, 0)),
            pl.BlockSpec((None, D, tn), lambda l, j: (l, 0, j)),
            pl.BlockSpec((None, D, tn), lambda l, j: (l, 0, j)),
            pl.BlockSpec((None, 1, tn), lambda l, j: (l, 0, j)),
        ],
        out_specs=pl.BlockSpec((None, 16, tn), lambda l, j: (l, 0, j)),
        out_shape=jax.ShapeDtypeStruct((DEPTH, 16, 6 * D), F32),
        compiler_params=_cp(("arbitrary", "arbitrary")),
        name="mod_table",
    )(cvec, w_hi, w_lo, b_mod.reshape(DEPTH, 1, 6 * D))
    return out.reshape(DEPTH, 16, 1, 6 * D)


def _mod_spec(l, which, tm):
    return pl.BlockSpec((None, None, 1, D), lambda i, *_: (l, (i * tm) // SEG, 0, which))


def _modmm_kernel(x_ref, g_ref, sh_ref, sc_ref, w_ref, o_ref, *, nchunk):
    h = _modulate(x_ref[...], g_ref[...], sh_ref[...], sc_ref[...]).astype(BF16)
    n = o_ref.shape[1]
    for n0 in range(0, n, nchunk):
        o_ref[:, n0:n0 + nchunk] = _dot(h, w_ref[:, n0:n0 + nchunk])


def _modmm(x, g, mod, l, w_bf16, tm):
    n = w_bf16.shape[1]
    return pl.pallas_call(
        functools.partial(_modmm_kernel, nchunk=256),
        grid=(NT // tm,),
        in_specs=[
            pl.BlockSpec((tm, D), lambda i: (i, 0)),
            pl.BlockSpec((1, D), lambda i: (0, 0)),
            _mod_spec(l, 0, tm),
            _mod_spec(l, 1, tm),
            pl.BlockSpec((D, n), lambda i: (0, 0)),
        ],
        out_specs=pl.BlockSpec((tm, n), lambda i: (i, 0)),
        out_shape=jax.ShapeDtypeStruct((NT, n), F32),
        compiler_params=_cp(("arbitrary",)),
        name="modmm",
    )(x, g.reshape(1, D), mod, mod, w_bf16)


def _final_kernel(x_ref, g_ref, o_ref):
    x = x_ref[...]
    ms = jnp.mean(x * x, axis=-1, keepdims=True)
    o_ref[...] = x * lax.rsqrt(ms + EPS) * g_ref[...]


def _final_norm(x, g, row0, n_rows, tm=512):
    r0 = row0 // tm
    return pl.pallas_call(
        _final_kernel,
        grid=(n_rows // tm,),
        in_specs=[pl.BlockSpec((tm, D), lambda i: (r0 + i, 0)), pl.BlockSpec((1, D), lambda i: (0, 0))],
        out_specs=pl.BlockSpec((tm, D), lambda i: (i, 0)),
        out_shape=jax.ShapeDtypeStruct((n_rows, D), F32),
        compiler_params=_cp(("arbitrary",)),
        name="final_norm",
    )(x, g.reshape(1, D))


_QA, _KA, _VA, _QB, _KB, _VB = 0, 512, 1024, 1536, 2048, 2176


def _ctx_attn_kernel(sink_ref, p_ref, o_ref):
    scale = HEAD_DIM ** -0.5

    def head(qc, kc, vc, sink):
        q = p_ref[:, qc:qc + 64].astype(BF16)
        k = p_ref[:, kc:kc + 64].astype(BF16)
        v = p_ref[:, vc:vc + 64].astype(BF16)
        s = _dot_nt(q, k) * scale
        m = jnp.max(s, axis=-1, keepdims=True)
        if sink is not None:
            m = jnp.maximum(m, sink)
        p = jnp.exp(s - m)
        den = jnp.sum(p, axis=-1, keepdims=True)
        if sink is not None:
            den = den + jnp.exp(sink - m)
        return _dot(p.astype(BF16), v) / den

    for h in range(NA_HEADS):
        o_ref[:, 64 * h:64 * h + 64] = head(_QA + 64 * h, _KA + 64 * h, _VA + 64 * h, None)
    for j in range(SWA_HEADS):
        kv = j // (SWA_HEADS // SWA_KV_HEADS)
        o_ref[:, 512 + 64 * j:512 + 64 * j + 64] = head(_QB + 64 * j, _KB + 64 * kv, _VB + 64 * kv, sink_ref[j])


def _ctx_attn(p, sink):
    return pl.pallas_call(
        _ctx_attn_kernel,
        grid=(BATCH,),
        in_specs=[
            pl.BlockSpec(memory_space=pltpu.SMEM),
            pl.BlockSpec((SEQ, D_IN_EVEN), lambda b: (b, 0)),
        ],
        out_specs=pl.BlockSpec((SEQ, D), lambda b: (b, 0)),
        out_shape=jax.ShapeDtypeStruct((NT, D), F32),
        compiler_params=_cp(("arbitrary",)),
        name="ctx_attn",
    )(sink, p)


NA_QROWS = 8
NA_BAND = 16
NA_TQ = NA_QROWS * GRID_W
NA_TK = NA_BAND * GRID_W
NA_RB = (DEC_SEQ // GRID_W) // NA_QROWS


def _na_band_start(rb):
    rows = DEC_SEQ // GRID_W
    return int(np.clip(NA_QROWS * rb - NA_WIN_R // 2, 0, rows - NA_BAND))


def _na_row_index():
    rows = DEC_SEQ // GRID_W
    n_dr = 2 * NA_WIN_R - 1
    out = np.full((3, NA_QROWS, NA_BAND), n_dr, np.int32)
    for ci, rb in enumerate((0, 1, NA_RB - 1)):
        for ql in range(NA_QROWS):
            qrow = NA_QROWS * rb + ql
            r0 = int(np.clip(qrow - NA_WIN_R // 2, 0, rows - NA_WIN_R))
            for kl in range(NA_BAND):
                krow = _na_band_start(rb) + kl
                if r0 <= krow < r0 + NA_WIN_R:
                    out[ci, ql, kl] = krow - qrow + (NA_WIN_R - 1)
    return out


def _na_bias_table(rel_bias):
    h = rel_bias.shape[0]
    n_dr, n_dc = 2 * NA_WIN_R - 1, 2 * NA_WIN_C - 1
    cols = np.arange(GRID_W)
    c0 = np.clip(cols - NA_WIN_C // 2, 0, GRID_W - NA_WIN_C)
    vc = (cols[None, :] >= c0[:, None]) & (cols[None, :] < c0[:, None] + NA_WIN_C)
    dc = cols[None, :] - cols[:, None] + (NA_WIN_C - 1)
    sel = ((np.arange(n_dc)[:, None, None] == dc[None]) & vc[None]).astype(np.float32)
    bc = jnp.dot(rel_bias.reshape(h * n_dr, n_dc), jnp.asarray(sel.reshape(n_dc, -1)),
                 precision=lax.Precision.HIGHEST).reshape(h, n_dr, GRID_W, GRID_W)
    bc = jnp.where(jnp.asarray(vc), bc, MASK)
    bc = jnp.concatenate([bc, jnp.full((h, 1, GRID_W, GRID_W), MASK, F32)], axis=1)
    tab = jnp.take(bc, jnp.asarray(_na_row_index()), axis=1)
    return tab.transpose(0, 1, 2, 4, 3, 5).reshape(h // 2, 2, 3, NA_TQ, NA_TK)


def _na_kernel(q_ref, k_ref, v_ref, kc_ref, vc_ref, bias_ref, oin_ref, o_ref):
    del oin_ref
    rb = pl.program_id(2)
    scale = HEAD_DIM ** -0.5
    start = jnp.clip(NA_QROWS * rb - NA_WIN_R // 2, 0, DEC_SEQ // GRID_W - NA_BAND) * GRID_W
    start = pl.multiple_of(start, 256)
    case = jnp.where(rb == 0, 0, jnp.where(rb == NA_RB - 1, 2, 1))
    kb = k_ref[pl.ds(start, NA_TK), :].astype(BF16)
    vb = v_ref[pl.ds(start, NA_TK), :].astype(BF16)
    q = q_ref[...] * scale
    lane = lax.broadcasted_iota(jnp.int32, (1, 128), 1)
    hs = range(2)
    s_loc = [_dot_nt(jnp.where((lane >> 6) == hh, q, 0.0).astype(BF16), kb) + bias_ref[hh, case] for hh in hs]
    s_ctx = [_dot_nt(q[:, 64 * hh:64 * hh + 64].astype(BF16), kc_ref[hh].astype(BF16)) for hh in hs]
    m = [jnp.maximum(jnp.max(s_loc[hh], axis=-1, keepdims=True), jnp.max(s_ctx[hh], axis=-1, keepdims=True))
         for hh in hs]
    p_loc = [jnp.exp(s_loc[hh] - m[hh]) for hh in hs]
    p_ctx = [jnp.exp(s_ctx[hh] - m[hh]) for hh in hs]
    inv = [1.0 / (jnp.sum(p_loc[hh], axis=-1, keepdims=True) + jnp.sum(p_ctx[hh], axis=-1, keepdims=True))
           for hh in hs]
    o_loc = [_dot(p_loc[hh].astype(BF16), vb) * inv[hh] for hh in hs]
    o_ctx = [_dot(p_ctx[hh].astype(BF16), vc_ref[hh].astype(BF16)) * inv[hh] for hh in hs]
    o_ref[...] = jnp.where((lane >> 6) == 0, o_loc[0], o_loc[1]) + jnp.concatenate(o_ctx, axis=-1)


def _na_attn(p, cache_k, cache_v, bias_tab, e, o_full):
    n_hp = NA_HEADS // 2
    lat0 = NT_CTX // NA_TQ
    per_b = DEC_SEQ // NA_TQ
    return pl.pallas_call(
        _na_kernel,
        grid=(n_hp, DEC_BATCH, NA_RB),
        in_specs=[
            pl.BlockSpec((NA_TQ, 128), lambda hp, b, rb: (lat0 + b * per_b + rb, _QA // 128 + hp)),
            pl.BlockSpec((DEC_SEQ, 128), lambda hp, b, rb: (1 + b, _KA // 128 + hp)),
            pl.BlockSpec((DEC_SEQ, 128), lambda hp, b, rb: (1 + b, _VA // 128 + hp)),
            pl.BlockSpec((None, None, 2, PAST_LEN, HEAD_DIM), lambda hp, b, rb: (b, e, hp, 0, 0)),
            pl.BlockSpec((None, None, 2, PAST_LEN, HEAD_DIM), lambda hp, b, rb: (b, e, hp, 0, 0)),
            pl.BlockSpec((None, 2, 3, NA_TQ, NA_TK), lambda hp, b, rb: (hp, 0, 0, 0, 0)),
            pl.BlockSpec(memory_space=pl.ANY),
        ],
        out_specs=pl.BlockSpec((NA_TQ, 128), lambda hp, b, rb: (lat0 + b * per_b + rb, hp)),
        out_shape=jax.ShapeDtypeStruct((NT, D), F32),
        input_output_aliases={6: 0},
        compiler_params=_cp(("arbitrary", "arbitrary", "arbitrary")),
        name="na_attn",
    )(p, p, p, cache_k, cache_v, bias_tab, o_full)


SWA_TQ = 256
SWA_TK = SWA_TQ + 2 * SWA_WINDOW


def _rope_tables():
    pos = np.arange(DEC_SEQ)
    row = (pos // GRID_W).astype(np.float32)
    col = (pos % GRID_W).astype(np.float32)
    half = HEAD_DIM // 2
    inv = jnp.asarray(ROPE_BASE, F32) ** (-jnp.arange(0, half, 2, dtype=F32) / half)
    ar = jnp.asarray(row)[:, None] * inv[None, :]
    ac = jnp.asarray(col)[:, None] * inv[None, :]
    cos = jnp.concatenate([jnp.cos(ar), jnp.cos(ar), jnp.cos(ac), jnp.cos(ac)], axis=1)
    sin = jnp.concatenate([-jnp.sin(ar), jnp.sin(ar), -jnp.sin(ac), jnp.sin(ac)], axis=1)
    return jnp.tile(cos, (1, 2)), jnp.tile(sin, (1, 2))


def _rope128(t, cos, sin):
    lane = lax.broadcasted_iota(jnp.int32, (1, 128), 1)
    up = pltpu.roll(t, 112, 1)
    dn = pltpu.roll(t, 16, 1)
    sw = jnp.where((lane & 31) < 16, up, dn)
    return t * cos + sw * sin


def _swa_mask_table():
    out = []
    for qb in (0, 1, DEC_SEQ // SWA_TQ - 1):
        q0 = qb * SWA_TQ
        bs = int(np.clip(q0 - SWA_WINDOW, 0, DEC_SEQ - SWA_TK))
        d = (q0 + np.arange(SWA_TQ))[:, None] - (bs + np.arange(SWA_TK))[None, :]
        out.append(np.where(np.abs(d) <= SWA_WINDOW, 0.0, MASK))
    return np.stack(out).astype(np.float32)


def _swa_kernel(sink_ref, q_ref, k_ref, v_ref, kc_ref, vc_ref, cos_ref, sin_ref, mask_ref, oin_ref, o_ref,
                kr_ref):
    del oin_ref
    qb = pl.program_id(1)
    scale = HEAD_DIM ** -0.5
    g = SWA_HEADS // SWA_KV_HEADS

    @pl.when(qb == 0)
    def _():
        kr_ref[...] = _rope128(k_ref[...], cos_ref[...], sin_ref[...]).astype(BF16)

    q0 = pl.multiple_of(qb * SWA_TQ, SWA_TQ)
    bs = pl.multiple_of(jnp.clip(q0 - SWA_WINDOW, 0, DEC_SEQ - SWA_TK), 128)
    cq = cos_ref[pl.ds(q0, SWA_TQ), :]
    sq = sin_ref[pl.ds(q0, SWA_TQ), :]
    kb = kr_ref[pl.ds(bs, SWA_TK), :]
    vb = v_ref[pl.ds(bs, SWA_TK), :].astype(BF16)
    qr = [_rope128(q_ref[:, 128 * j:128 * j + 128], cq, sq) * scale for j in range(SWA_HEADS // 2)]
    case = jnp.where(qb == 0, 0, jnp.where(qb == pl.num_programs(1) - 1, 2, 1))
    wmask = mask_ref[case]
    rowc = lax.broadcasted_iota(jnp.int32, (g * SWA_TQ, 1), 0) >> 8
    lane_half = lax.broadcasted_iota(jnp.int32, (1, 128), 1) >> 6

    kvs = range(SWA_KV_HEADS)
    q4, sink = [], []
    for kvh in kvs:
        heads = []
        sk = jnp.zeros((g * SWA_TQ, 1), F32)
        for gi in range(g):
            hq = kvh * g + gi
            heads.append(qr[hq // 2][:, 64 * (hq % 2):64 * (hq % 2) + 64])
            sk = jnp.where(rowc == gi, sink_ref[hq], sk)
        q4.append(jnp.concatenate(heads, axis=0).astype(BF16))
        sink.append(sk)
    s_loc = [(_dot_nt(q4[kvh], kb[:, 64 * kvh:64 * kvh + 64]).reshape(g, SWA_TQ, SWA_TK)
              + wmask[None]).reshape(g * SWA_TQ, SWA_TK) for kvh in kvs]
    s_ctx = [_dot_nt(q4[kvh], kc_ref[kvh].astype(BF16)) for kvh in kvs]
    m = [jnp.maximum(jnp.max(jnp.maximum(s_loc[kvh], s_ctx[kvh]), axis=-1, keepdims=True), sink[kvh])
         for kvh in kvs]
    p_loc = [jnp.exp(s_loc[kvh] - m[kvh]).astype(BF16) for kvh in kvs]
    p_ctx = [jnp.exp(s_ctx[kvh] - m[kvh]).astype(BF16) for kvh in kvs]
    pv = [_dot(p_loc[kvh], jnp.where(lane_half == kvh, vb, jnp.ones_like(vb)))
          + _dot(p_ctx[kvh], vc_ref[kvh].astype(BF16)) for kvh in kvs]
    for kvh in kvs:
        lo, hi = 64 * kvh, 64 * (1 - kvh)
        den = pv[kvh][:, hi:hi + 1] + jnp.exp(sink[kvh] - m[kvh])
        o = pv[kvh][:, lo:lo + 64] / den
        for gi in range(g):
            hq = kvh * g + gi
            o_ref[:, 64 * hq:64 * hq + 64] = o[SWA_TQ * gi:SWA_TQ * (gi + 1), :]


def _swa_value_ext(cache_v):
    ones = jnp.ones_like(cache_v[..., 0, :, :])
    return jnp.stack([jnp.concatenate([cache_v[..., 0, :, :], ones], axis=-1),
                      jnp.concatenate([ones, cache_v[..., 1, :, :]], axis=-1)], axis=-3)


def _swa_attn(p, cache_k, cache_v_ext, sink, cos, sin, e, o_full):
    assert SWA_TQ == 256
    lat0 = NT_CTX // SWA_TQ
    per_b = DEC_SEQ // SWA_TQ
    return pl.pallas_call(
        _swa_kernel,
        grid=(DEC_BATCH, per_b),
        in_specs=[
            pl.BlockSpec(memory_space=pltpu.SMEM),
            pl.BlockSpec((SWA_TQ, 512), lambda b, qb: (lat0 + b * per_b + qb, _QB // 512)),
            pl.BlockSpec((DEC_SEQ, 128), lambda b, qb: (1 + b, _KB // 128)),
            pl.BlockSpec((DEC_SEQ, 128), lambda b, qb: (1 + b, _VB // 128)),
            pl.BlockSpec((None, None, 2, PAST_LEN, HEAD_DIM), lambda b, qb: (b, e, 0, 0, 0)),
            pl.BlockSpec((None, None, 2, PAST_LEN, 128), lambda b, qb: (b, e, 0, 0, 0)),
            pl.BlockSpec((DEC_SEQ, 128), lambda b, qb: (0, 0)),
            pl.BlockSpec((DEC_SEQ, 128), lambda b, qb: (0, 0)),
            pl.BlockSpec((3, SWA_TQ, SWA_TK), lambda b, qb: (0, 0, 0)),
            pl.BlockSpec(memory_space=pl.ANY),
        ],
        out_specs=pl.BlockSpec((SWA_TQ, 512), lambda b, qb: (lat0 + b * per_b + qb, 1)),
        out_shape=jax.ShapeDtypeStruct((NT, D), F32),
        scratch_shapes=[pltpu.VMEM((DEC_SEQ, 128), BF16)],
        input_output_aliases={9: 0},
        compiler_params=_cp(("arbitrary", "arbitrary")),
        name="swa_attn",
    )(sink, p, p, p, cache_k, cache_v_ext, cos, sin, jnp.asarray(_swa_mask_table()), o_full)


CONV_TT = 256
CONV_HALO = 16
CONV_RC = 32


def _conv_kernel(u_ref, ul_ref, ur_ref, w_ref, cb_ref, lg_ref, lb_ref, o_ref, xs_ref):
    i = pl.program_id(0)
    n_ctx_tiles = NT_CTX // CONV_TT
    per_seq = DEC_SEQ // CONV_TT
    j = i - n_ctx_tiles
    is_ctx = i < n_ctx_tiles
    is_start = is_ctx | ((j % per_seq) == 0)
    is_end = is_ctx | ((j % per_seq) == per_seq - 1)

    def glu(u):
        return u[:, :CONV_CH] * _sigmoid(u[:, CONV_CH:])

    xs_ref[0, CONV_HALO:CONV_HALO + CONV_TT, :] = glu(u_ref[...])
    xs_ref[0, 0:CONV_HALO, :] = jnp.where(is_start, 0.0, glu(ul_ref[...]))
    xs_ref[0, CONV_HALO + CONV_TT:, :] = jnp.where(is_end, 0.0, glu(ur_ref[...]))
    n_sh = CONV_TT + 2 * CONV_HALO - 8
    for b in range(1, 8):
        xs_ref[b, 0:n_sh, :] = xs_ref[0, b:b + n_sh, :]

    pad = CONV_WIDTH // 2
    for c in range(CONV_TT // CONV_RC):
        base = CONV_HALO + c * CONV_RC - pad
        acc = jnp.zeros((CONV_RC, CONV_CH), F32)
        for t in range(CONV_WIDTH):
            off = base + t
            acc = acc + xs_ref[off % 8, off - off % 8:off - off % 8 + CONV_RC, :] * w_ref[t:t + 1, :]
        y = acc + cb_ref[...]
        mu = jnp.mean(y, axis=-1, keepdims=True)
        yc = y - mu
        var = jnp.mean(yc * yc, axis=-1, keepdims=True)
        yn = yc * lax.rsqrt(var + EPS) * lg_ref[...] + lb_ref[...]
        o_ref[c * CONV_RC:(c + 1) * CONV_RC, :] = yn * _sigmoid(yn)


def _conv_module(p, conv_w, conv_b, ln_g, ln_b):
    nh = CONV_TT // CONV_HALO
    last = NT // CONV_HALO - 1
    return pl.pallas_call(
        _conv_kernel,
        grid=(NT // CONV_TT,),
        in_specs=[
            pl.BlockSpec((CONV_TT, 2 * CONV_CH), lambda i: (i, 0)),
            pl.BlockSpec((CONV_HALO, 2 * CONV_CH), lambda i: (jnp.maximum(i * nh - 1, 0), 0)),
            pl.BlockSpec((CONV_HALO, 2 * CONV_CH), lambda i: (jnp.minimum((i + 1) * nh, last), 0)),
            pl.BlockSpec((CONV_WIDTH, CONV_CH), lambda i: (0, 0)),
            pl.BlockSpec((1, CONV_CH), lambda i: (0, 0)),
            pl.BlockSpec((1, CONV_CH), lambda i: (0, 0)),
            pl.BlockSpec((1, CONV_CH), lambda i: (0, 0)),
        ],
        out_specs=pl.BlockSpec((CONV_TT, CONV_CH), lambda i: (i, 0)),
        out_shape=jax.ShapeDtypeStruct((NT, D), F32),
        scratch_shapes=[pltpu.VMEM((8, CONV_TT + 2 * CONV_HALO, CONV_CH), F32)],
        compiler_params=_cp(("arbitrary",)),
        name="conv_module",
    )(p, p, p, conv_w, conv_b.reshape(1, -1), ln_g.reshape(1, -1), ln_b.reshape(1, -1))


_U, _HQ, _HI, _HFF, _HFB, _HGT = 0, 1024, 1536, 2048, 2560, 3072
HG_LEVELS = 6
HG_W = HG_HEADS * HG_DK


def _hgrn_constants():
    c = HG_CHUNK
    mm = np.zeros((HG_LEVELS + 1, c, c), np.float32)
    for lv in range(HG_LEVELS):
        n = c >> lv
        half = n // 2
        for t in range(c):
            for s in range(c):
                if t // n == s // n and t % n >= half and s % n < half:
                    mm[lv, t, s] = 1.0
    mm[HG_LEVELS] = np.eye(c, dtype=np.float32)
    tri = np.tril(np.ones((c, c), np.float32))
    return np.stack([tri, tri.T]), np.stack([mm, mm[:, ::-1, ::-1]])


def _hgrn_level_decays(cum, forget, cum_ref, fwd):
    c = HG_CHUNK
    cum_ref[...] = cum
    out = []
    for n in (64, 32, 16):
        pieces = []
        for b in range(c // n):
            r = b * n + n // 2 - (1 if fwd else 0)
            ref = jnp.broadcast_to(cum_ref[r:r + 1, :], (n // 2, HG_W))
            lo = cum[b * n:b * n + n // 2]
            hi = cum[b * n + n // 2:(b + 1) * n]
            pieces += [ref - lo, hi - ref] if fwd else [lo - ref, ref - hi]
        out.append(jnp.exp(jnp.concatenate(pieces, axis=0)))
    rows = [b * 8 + 4 - (1 if fwd else 0) for b in range(c // 8)]
    ref8 = jnp.concatenate([jnp.broadcast_to(cum_ref[r:r + 1, :], (8, HG_W)) for r in rows], axis=0)
    out.append(jnp.exp(-jnp.abs(cum - ref8)))
    row = lax.broadcasted_iota(jnp.int32, (c, HG_W), 0)
    p = 2 - (1 if fwd else 0)
    ref4 = None
    for j in range(4):
        shift = (j - p) % c
        cand = cum if shift == 0 else pltpu.roll(cum, shift, 0)
        ref4 = cand if ref4 is None else jnp.where((row & 3) == j, cand, ref4)
    out.append(jnp.exp(-jnp.abs(cum - ref4)))
    out.append(jnp.where((row & 1) == (1 if fwd else 0), forget, 1.0))
    return out


def _hgrn_gates(q, f, lb, tri):
    qs = q * _sigmoid(q)
    forget = lb + (1.0 - lb) * _sigmoid(f)
    g = jnp.log(forget)
    k = (1.0 - lb) * _sigmoid(-f)
    g_hi = g.astype(BF16)
    r1 = g - g_hi.astype(F32)
    g_mid = r1.astype(BF16)
    g_lo = (r1 - g_mid.astype(F32)).astype(BF16)
    cum = _dot(tri, g_hi) + (_dot(tri, g_mid) + _dot(tri, g_lo))
    return qs, k, forget, cum


def _hgrn_operands(qs, k, forget, v, cum, cum_ref, fwd):
    c = HG_CHUNK
    decays = _hgrn_level_decays(cum, forget, cum_ref, fwd)
    end_row = c - 1 if fwd else 0
    ecum = jnp.exp(cum)
    eend = jnp.exp(-jnp.abs(cum - jnp.broadcast_to(cum_ref[end_row:end_row + 1, :], (c, HG_W))))
    qb, kb = qs.astype(BF16), k.astype(BF16)
    qk = [(qb, kb)]
    for el in decays:
        elb = el.astype(BF16)
        qk.append((qb * elb, kb * elb))
    return dict(qk=qk, vb=v.astype(BF16), qc=(qs * ecum).astype(BF16), kend=(k * eend).astype(BF16),
                dec=ecum[end_row:end_row + 1, :])


def _hgrn_scores(ops, masks):
    out = []
    for h in range(HG_HEADS):
        sl = slice(h * HG_DK, (h + 1) * HG_DK)
        prods = [_dot_nt(qb[:, sl], kb[:, sl]) for qb, kb in ops["qk"]]
        a = masks[HG_LEVELS] * prods[0]
        for lv in range(HG_LEVELS):
            a = a + masks[lv] * prods[lv + 1]
        out.append(a.astype(BF16))
    return out


def _hgrn_outputs(ops, scores, st_ref):
    outs = []
    for h in range(HG_HEADS):
        sl = slice(h * HG_DK, (h + 1) * HG_DK)
        s_t = st_ref[h]
        outs.append(_dot(scores[h], ops["vb"][:, sl]) + _dot_nt(ops["qc"][:, sl], s_t.astype(BF16)))
        st_ref[h] = s_t * ops["dec"][:, sl] + _dot_tn(ops["vb"][:, sl], ops["kend"][:, sl])
    return jnp.concatenate(outs, axis=1)


def _hgrn_kernel(qf_ref, if_ref, ff_ref, qb_ref, ib_ref, fb_ref, lb_ref, s0_ref, tri_ref, m_ref,
                 of_ref, ob_ref, sfin_ref, stf_ref, stb_ref, cumf_ref, cumb_ref, *, tile):
    c = HG_CHUNK
    nct = tile // c
    tt = pl.program_id(1)

    @pl.when(tt == 0)
    def _():
        stf_ref[...] = s0_ref[0]
        stb_ref[...] = s0_ref[1]

    lbf = lb_ref[0:1, :]
    lbb = lb_ref[1:2, :]

    masks_f = [m_ref[0, x] for x in range(HG_LEVELS + 1)]
    masks_b = [m_ref[1, x] for x in range(HG_LEVELS + 1)]

    def body(pi, carry):
        rows = []
        for u in range(2):
            ci = 2 * pi + u
            rows.append((pl.multiple_of(ci * c, c), pl.multiple_of((nct - 1 - ci) * c, c)))
        gates = [(_hgrn_gates(qf_ref[pl.ds(rf, c), :], ff_ref[pl.ds(rf, c), :], lbf, tri_ref[0]),
                  _hgrn_gates(qb_ref[pl.ds(rb, c), :], fb_ref[pl.ds(rb, c), :], lbb, tri_ref[1]))
                 for rf, rb in rows]
        ops = [(_hgrn_operands(*gates[u][0][:3], if_ref[pl.ds(rf, c), :], gates[u][0][3], cumf_ref.at[u], True),
                _hgrn_operands(*gates[u][1][:3], ib_ref[pl.ds(rb, c), :], gates[u][1][3], cumb_ref.at[u], False))
               for u, (rf, rb) in enumerate(rows)]
        scores = [(_hgrn_scores(ops[u][0], masks_f), _hgrn_scores(ops[u][1], masks_b)) for u in range(2)]
        for u, (rf, rb) in enumerate(rows):
            o_f = _hgrn_outputs(ops[u][0], scores[u][0], stf_ref)
            o_b = _hgrn_outputs(ops[u][1], scores[u][1], stb_ref)
            of_ref[pl.ds(rf, c), :] = o_f
            ob_ref[pl.ds(rb, c), :] = o_b
        return carry

    lax.fori_loop(0, nct // 2, body, 0)

    @pl.when(tt == pl.num_programs(1) - 1)
    def _():
        sfin_ref[0] = stf_ref[...]
        sfin_ref[1] = stb_ref[...]


def _hgrn_scan(p, lb, s0_t, n_batch, t_len, row0):
    tri, mm = _hgrn_constants()
    tri = jnp.asarray(tri, BF16)
    mm = jnp.asarray(mm, F32)
    tile = min(t_len, 512)
    ntt = t_len // tile
    r0 = row0 // tile

    def fwd(c0):
        return pl.BlockSpec((tile, HG_W), lambda b, t: (r0 + b * ntt + t, c0 // HG_W))

    def bwd(c0):
        return pl.BlockSpec((tile, HG_W), lambda b, t: (r0 + b * ntt + ntt - 1 - t, c0 // HG_W))

    st_spec = pl.BlockSpec((None, 2, HG_HEADS, 128, 128), lambda b, t: (b, 0, 0, 0, 0))
    n_rows = n_batch * t_len
    return pl.pallas_call(
        functools.partial(_hgrn_kernel, tile=tile),
        grid=(n_batch, ntt),
        in_specs=[
            fwd(_HQ), fwd(_HI), fwd(_HFF), bwd(_HQ), bwd(_HI), bwd(_HFB),
            pl.BlockSpec((2, HG_W), lambda b, t: (0, 0)),
            st_spec,
            pl.BlockSpec((2, HG_CHUNK, HG_CHUNK), lambda b, t: (0, 0, 0)),
            pl.BlockSpec((2, HG_LEVELS + 1, HG_CHUNK, HG_CHUNK), lambda b, t: (0, 0, 0, 0)),
        ],
        out_specs=[
            pl.BlockSpec((tile, HG_W), lambda b, t: (b * ntt + t, 0)),
            pl.BlockSpec((tile, HG_W), lambda b, t: (b * ntt + ntt - 1 - t, 0)),
            st_spec,
        ],
        out_shape=[jax.ShapeDtypeStruct((n_rows, HG_W), F32),
                   jax.ShapeDtypeStruct((n_rows, HG_W), F32),
                   jax.ShapeDtypeStruct((n_batch, 2, HG_HEADS, 128, 128), F32)],
        scratch_shapes=[pltpu.VMEM((HG_HEADS, 128, 128), F32), pltpu.VMEM((HG_HEADS, 128, 128), F32),
                        pltpu.VMEM((2, HG_CHUNK, HG_W), F32), pltpu.VMEM((2, HG_CHUNK, HG_W), F32)],
        compiler_params=_cp(("arbitrary", "arbitrary")),
        name="hgrn_scan",
    )(p, p, p, p, p, p, lb, s0_t, tri, mm)


def _hgrn_fin_kernel(of_ref, ob_ref, gt_ref, ng_ref, oin_ref, o_ref):
    del oin_ref
    od = of_ref[...] + ob_ref[...]
    gt = gt_ref[...]
    gate = gt * _sigmoid(gt)
    for h in range(HG_HEADS):
        sl = slice(h * HG_DK, (h + 1) * HG_DK)
        x = od[:, sl]
        ms = jnp.mean(x * x, axis=-1, keepdims=True)
        o_ref[:, sl] = x * lax.rsqrt(ms + EPS) * ng_ref[...] * gate[:, sl]


def _hgrn_finish(o_f, o_b, p, norm_g, o_full, row0, tm=512):
    n_rows = o_f.shape[0]
    r0 = row0 // tm
    return pl.pallas_call(
        _hgrn_fin_kernel,
        grid=(n_rows // tm,),
        in_specs=[
            pl.BlockSpec((tm, HG_W), lambda i: (i, 0)),
            pl.BlockSpec((tm, HG_W), lambda i: (i, 0)),
            pl.BlockSpec((tm, HG_W), lambda i: (r0 + i, _HGT // HG_W)),
            pl.BlockSpec((1, 128), lambda i: (0, 0)),
            pl.BlockSpec(memory_space=pl.ANY),
        ],
        out_specs=pl.BlockSpec((tm, HG_W), lambda i: (r0 + i, 1)),
        out_shape=jax.ShapeDtypeStruct((NT, D), F32),
        input_output_aliases={4: 0},
        compiler_params=_cp(("arbitrary",)),
        name="hgrn_finish",
    )(o_f, o_b, p, norm_g.reshape(1, 128), o_full)


def _hgrn(p, lb, norm_g, s0_t, o_full, n_batch, t_len, row0):
    o_f, o_b, s_fin = _hgrn_scan(p, lb, s0_t, n_batch, t_len, row0)
    return _hgrn_finish(o_f, o_b, p, norm_g, o_full, row0), s_fin


ROUTE_TM = 512


def _mmres_router_kernel(o_ref, x_ref, gate_ref, w_ref, g_ref, sh_ref, sc_ref, whi_ref, br_ref,
                         tri_ref, xo_ref, mi_ref, mf_ref, cnt_ref, carry_ref):
    i = pl.program_id(0)

    @pl.when(i == 0)
    def _():
        carry_ref[...] = jnp.zeros_like(carry_ref)

    xn = x_ref[...] + gate_ref[...] * _dot(o_ref[...].astype(BF16), w_ref[...])
    xo_ref[...] = xn
    h = _modulate(xn, g_ref[...], sh_ref[...], sc_ref[...])
    h_hi, h_lo = _split2(h)
    hw = _dot(h_hi, whi_ref[...])
    logits = hw[:, :128] + (hw[:, 128:] + _dot(h_lo, whi_ref[:, :128])) + br_ref[...]
    lane = lax.broadcasted_iota(jnp.int32, logits.shape, 1)
    neg = jnp.float32(-3e38)
    is_g = lane < N_GROUPS
    lg = jnp.where(is_g, logits, neg)
    mg = jnp.max(lg, axis=-1, keepdims=True)
    grp = jnp.min(jnp.where(lg == mg, lane, 128), axis=-1, keepdims=True)
    pg = 1.0 / jnp.sum(jnp.where(is_g, jnp.exp(lg - mg), 0.0), axis=-1, keepdims=True)
    ex = lane - N_GROUPS
    in_grp = (ex >= 0) & (ex < N_EXPERTS) & ((ex >> 3) == grp)
    le = jnp.where(in_grp, logits, neg)
    v1 = jnp.max(le, axis=-1, keepdims=True)
    i1 = jnp.min(jnp.where(le == v1, lane, 128), axis=-1, keepdims=True)
    le2 = jnp.where(lane == i1, neg, le)
    v2 = jnp.max(le2, axis=-1, keepdims=True)
    i2 = jnp.min(jnp.where(le2 == v2, lane, 128), axis=-1, keepdims=True)
    t = jnp.exp(v2 - v1)
    g1 = pg / (1.0 + t)
    g2 = pg * t / (1.0 + t)
    oh = jnp.where((lane == i1) | (lane == i2), 1.0, 0.0)
    prefix = _dot(tri_ref[...], oh.astype(BF16)) + carry_ref[...]
    r1 = jnp.sum(jnp.where(lane == i1, prefix, 0.0), axis=-1, keepdims=True).astype(jnp.int32)
    r2 = jnp.sum(jnp.where(lane == i2, prefix, 0.0), axis=-1, keepdims=True).astype(jnp.int32)
    carry_ref[...] = carry_ref[...] + jnp.sum(oh, axis=0, keepdims=True)
    cnt_ref[...] = carry_ref[...]
    mi_ref[...] = jnp.where(lane == 0, i1 - N_GROUPS, jnp.where(lane == 1, i2 - N_GROUPS,
                            jnp.where(lane == 2, r1, jnp.where(lane == 3, r2, 0))))
    mf_ref[...] = jnp.where(lane == 0, g1, jnp.where(lane == 1, g2, 0.0))


def _mmres_router(o, x, mod, l, w_out_bf16, g, wg, bg, we, be):
    tm = ROUTE_TM
    wr = jnp.zeros((D, 128), F32).at[:, :N_GROUPS].set(wg).at[:, N_GROUPS:N_GROUPS + N_EXPERTS].set(we)
    br = jnp.zeros((1, 128), F32).at[0, :N_GROUPS].set(bg).at[0, N_GROUPS:N_GROUPS + N_EXPERTS].set(be)
    w_hi = wr.astype(BF16)
    w_lo = (wr - w_hi.astype(F32)).astype(BF16)
    tri = jnp.asarray(np.tril(np.ones((tm, tm), np.float32), -1), BF16)
    const = lambda i: (0, 0)
    return pl.pallas_call(
        _mmres_router_kernel,
        grid=(NT // tm,),
        in_specs=[
            pl.BlockSpec((tm, D), lambda i: (i, 0)),
            pl.BlockSpec((tm, D), lambda i: (i, 0)),
            _mod_spec(l, 2, tm),
            pl.BlockSpec((D, D), const),
            pl.BlockSpec((1, D), const),
            _mod_spec(l, 3, tm),
            _mod_spec(l, 4, tm),
            pl.BlockSpec((D, 256), const),
            pl.BlockSpec((1, 128), const),
            pl.BlockSpec((tm, tm), const),
        ],
        out_specs=[
            pl.BlockSpec((tm, D), lambda i: (i, 0)),
            pl.BlockSpec((tm, 128), lambda i: (i, 0)),
            pl.BlockSpec((tm, 128), lambda i: (i, 0)),
            pl.BlockSpec((1, 128), const),
        ],
        out_shape=[
            jax.ShapeDtypeStruct((NT, D), F32),
            jax.ShapeDtypeStruct((NT, 128), jnp.int32),
            jax.ShapeDtypeStruct((NT, 128), F32),
            jax.ShapeDtypeStruct((1, 128), F32),
        ],
        scratch_shapes=[pltpu.VMEM((1, 128), F32)],
        compiler_params=_cp(("arbitrary",)),
        name="mmres_router",
    )(o, x, mod, w_out_bf16, g.reshape(1, D), mod, mod, jnp.concatenate([w_hi, w_lo], axis=1), br, tri)


DISP_TM = 512


def _row_copy(src, s_row, dst, d_row, sem):
    return pltpu.make_async_copy(src.at[pl.ds(s_row, 1)], dst.at[pl.ds(d_row, 1)], sem)


def _dispatch_kernel(dest_ref, x_ref, g_ref, sh_ref, sc_ref, xs_hbm, h_ref, sem):
    h_ref[...] = _modulate(x_ref[...], g_ref[...], sh_ref[...], sc_ref[...])

    def group(t8, start):
        base = pl.multiple_of(t8 * 8, 8)
        for s in range(8):
            for k in range(2):
                cp = _row_copy(h_ref, base + s, xs_hbm, dest_ref[k, base + s], sem)
                if start:
                    cp.start(priority=k)
                else:
                    cp.wait()

    lax.fori_loop(0, DISP_TM // 8, lambda t8, c: (group(t8, True), c)[1], 0)
    lax.fori_loop(0, DISP_TM // 8, lambda t8, c: (group(t8, False), c)[1], 0)


def _dispatch(dest_t, x, g, mod, l):
    return pl.pallas_call(
        _dispatch_kernel,
        grid=(NT // DISP_TM,),
        in_specs=[
            pl.BlockSpec((2, DISP_TM), lambda i: (0, i), memory_space=pltpu.SMEM),
            pl.BlockSpec((DISP_TM, D), lambda i: (i, 0)),
            pl.BlockSpec((1, D), lambda i: (0, 0)),
            _mod_spec(l, 3, DISP_TM),
            _mod_spec(l, 4, DISP_TM),
        ],
        out_specs=pl.BlockSpec(memory_space=pl.ANY),
        out_shape=jax.ShapeDtypeStruct((MOE_SLOTS, D), F32),
        scratch_shapes=[pltpu.VMEM((DISP_TM, D), F32), pltpu.SemaphoreType.DMA(())],
        compiler_params=_cp(("arbitrary",)),
        name="moe_dispatch",
    )(dest_t, x, g.reshape(1, D), mod, mod)


def _expert_kernel(bexp_ref, bval_ref, bfirst_ref, xs_ref, w1_ref, w3_ref, w2_ref, y_ref,
                   w1b_ref, w3b_ref, w2b_ref):
    del bexp_ref
    i = pl.program_id(0)
    nv = bval_ref[i]

    @pl.when(bfirst_ref[i] == 1)
    def _():
        w1b_ref[...] = w1_ref[...].astype(BF16)
        w3b_ref[...] = w3_ref[...].astype(BF16)
        w2b_ref[...] = w2_ref[...].astype(BF16)

    @pl.when(nv > 0)
    def _():
        row = lax.broadcasted_iota(jnp.int32, (MOE_MB, 1), 0)
        xb = jnp.where(row < nv, xs_ref[...], 0.0).astype(BF16)
        a = _dot(xb, w1b_ref[...])
        b = _dot(xb, w3b_ref[...])
        hid = (a * _sigmoid(a) * b).astype(BF16)
        y_ref[...] = _dot(hid, w2b_ref[...])

    @pl.when(nv <= 0)
    def _():
        y_ref[...] = jnp.zeros_like(y_ref)


def _experts(blk_exp, blk_valid, blk_first, xs, w1, w3, w2, l):
    grid_spec = pltpu.PrefetchScalarGridSpec(
        num_scalar_prefetch=3,
        grid=(MOE_NB,),
        in_specs=[
            pl.BlockSpec((MOE_MB, D), lambda i, be, bv, bf: (i, 0)),
            pl.BlockSpec((None, None, D, D_EXPERT), lambda i, be, bv, bf: (l, be[i], 0, 0)),
            pl.BlockSpec((None, None, D, D_EXPERT), lambda i, be, bv, bf: (l, be[i], 0, 0)),
            pl.BlockSpec((None, None, D_EXPERT, D), lambda i, be, bv, bf: (l, be[i], 0, 0)),
        ],
        out_specs=pl.BlockSpec((MOE_MB, D), lambda i, be, bv, bf: (i, 0)),
        scratch_shapes=[pltpu.VMEM((D, D_EXPERT), BF16), pltpu.VMEM((D, D_EXPERT), BF16),
                        pltpu.VMEM((D_EXPERT, D), BF16)],
    )
    return pl.pallas_call(
        _expert_kernel,
        grid_spec=grid_spec,
        out_shape=jax.ShapeDtypeStruct((MOE_SLOTS, D), F32),
        compiler_params=_cp(("arbitrary",)),
        name="moe_experts",
    )(blk_exp, blk_valid, blk_first, xs, w1, w3, w2)


COMB_TM = 256


def _combine_kernel(dest_ref, mf_ref, x_ref, gate_ref, y_hbm, o_ref, buf_ref, sem):
    def group(t8, start):
        base = pl.multiple_of(t8 * 8, 8)
        for s in range(8):
            for k in range(2):
                cp = _row_copy(y_hbm, dest_ref[k, base + s], buf_ref.at[k], base + s, sem)
                if start:
                    cp.start(priority=k)
                else:
                    cp.wait()

    lax.fori_loop(0, COMB_TM // 8, lambda t8, c: (group(t8, True), c)[1], 0)
    lax.fori_loop(0, COMB_TM // 8, lambda t8, c: (group(t8, False), c)[1], 0)
    mf = mf_ref[...]
    y = mf[:, 0:1] * buf_ref[0] + mf[:, 1:2] * buf_ref[1]
    o_ref[...] = x_ref[...] + gate_ref[...] * y


def _combine(dest_t, mf, x, mod, l, y):
    tm = COMB_TM
    return pl.pallas_call(
        _combine_kernel,
        grid=(NT // tm,),
        in_specs=[
            pl.BlockSpec((2, tm), lambda i: (0, i), memory_space=pltpu.SMEM),
            pl.BlockSpec((tm, 128), lambda i: (i, 0)),
            pl.BlockSpec((tm, D), lambda i: (i, 0)),
            _mod_spec(l, 5, tm),
            pl.BlockSpec(memory_space=pl.ANY),
        ],
        out_specs=pl.BlockSpec((tm, D), lambda i: (i, 0)),
        out_shape=jax.ShapeDtypeStruct((NT, D), F32),
        scratch_shapes=[pltpu.VMEM((2, tm, D), F32), pltpu.SemaphoreType.DMA(())],
        compiler_params=_cp(("arbitrary",)),
        name="moe_combine",
    )(dest_t, mf, x, mod, y)


def _comb_modmm_kernel(dcur_ref, dnxt_ref, mf_ref, x_ref, gate_ref, g_ref, sh_ref, sc_ref, w_ref, y_hbm,
                       xo_ref, p_ref, buf_ref, sem, *, nchunk):
    i = pl.program_id(0)
    n = pl.num_programs(0)
    slot = lax.rem(i, 2)
    tm = x_ref.shape[0]

    def rows(dest_ref, to_slot, lo, hi, start):
        for t in range(lo, hi):
            for k in range(2):
                cp = _row_copy(y_hbm, dest_ref[k, t], buf_ref.at[to_slot, k], t, sem.at[to_slot])
                if start:
                    cp.start(priority=k)
                else:
                    cp.wait()

    def rolled(dest_ref, to_slot, start):
        def body(t8, c):
            base = pl.multiple_of(t8 * 8, 8)
            for s in range(8):
                for k in range(2):
                    cp = _row_copy(y_hbm, dest_ref[k, base + s], buf_ref.at[to_slot, k], base + s,
                                   sem.at[to_slot])
                    if start:
                        cp.start(priority=k)
                    else:
                        cp.wait()
            return c
        lax.fori_loop(0, tm // 8, body, 0)

    @pl.when(i == 0)
    def _():
        rolled(dcur_ref, slot, True)

    rolled(dcur_ref, slot, False)
    mf = mf_ref[...]
    y = mf[:, 0:1] * buf_ref[slot, 0] + mf[:, 1:2] * buf_ref[slot, 1]
    xn = x_ref[...] + gate_ref[...] * y
    xo_ref[...] = xn
    h = _modulate(xn, g_ref[...], sh_ref[...], sc_ref[...]).astype(BF16)
    nout = p_ref.shape[1]
    n_ch = nout // nchunk
    per = -(-tm // n_ch)
    for c in range(n_ch):
        rows(dnxt_ref, 1 - slot, min(c * per, tm), min((c + 1) * per, tm), True)
        p_ref[:, c * nchunk:(c + 1) * nchunk] = _dot(h, w_ref[:, c * nchunk:(c + 1) * nchunk])

    @pl.when(i == n - 1)
    def _():
        rolled(dnxt_ref, 1 - slot, False)


def _comb_modmm(dest_t, mf, x, mod, l, y, g_next, w_bf16, tm=256):
    nout = w_bf16.shape[1]
    nt = NT // tm
    return pl.pallas_call(
        functools.partial(_comb_modmm_kernel, nchunk=256),
        grid=(nt,),
        in_specs=[
            pl.BlockSpec((2, tm), lambda i: (0, i), memory_space=pltpu.SMEM),
            pl.BlockSpec((2, tm), lambda i: (0, jnp.minimum(i + 1, nt - 1)), memory_space=pltpu.SMEM),
            pl.BlockSpec((tm, 128), lambda i: (i, 0)),
            pl.BlockSpec((tm, D), lambda i: (i, 0)),
            _mod_spec(l, 5, tm),
            pl.BlockSpec((1, D), lambda i: (0, 0)),
            _mod_spec(l + 1, 0, tm),
            _mod_spec(l + 1, 1, tm),
            pl.BlockSpec((D, nout), lambda i: (0, 0)),
            pl.BlockSpec(memory_space=pl.ANY),
        ],
        out_specs=[pl.BlockSpec((tm, D), lambda i: (i, 0)), pl.BlockSpec((tm, nout), lambda i: (i, 0))],
        out_shape=[jax.ShapeDtypeStruct((NT, D), F32), jax.ShapeDtypeStruct((NT, nout), F32)],
        scratch_shapes=[pltpu.VMEM((2, 2, tm, D), F32), pltpu.SemaphoreType.DMA((2,))],
        compiler_params=_cp(("arbitrary",)),
        name="moe_combine_modmm",
    )(dest_t, dest_t, mf, x, mod, g_next.reshape(1, D), mod, mod, w_bf16, y)


def _moe_plan(mi, counts):
    cnt = counts[0, N_GROUPS:N_GROUPS + N_EXPERTS].astype(jnp.int32)
    padded = (cnt + MOE_MB - 1) // MOE_MB * MOE_MB
    pad_ends = jnp.cumsum(padded)
    pad_starts = pad_ends - padded
    dest = pad_starts[mi[:, 0:2]] + mi[:, 2:4]
    blk0 = jnp.arange(MOE_NB, dtype=jnp.int32) * MOE_MB
    blk_exp = jnp.sum((pad_ends[None, :] <= blk0[:, None]).astype(jnp.int32), axis=1)
    blk_exp = jnp.minimum(blk_exp, N_EXPERTS - 1)
    blk_valid = jnp.clip(cnt[blk_exp] - (blk0 - pad_starts[blk_exp]), 0, MOE_MB)
    blk_valid = jnp.where(blk0 < pad_ends[-1], blk_valid, 0).astype(jnp.int32)
    blk_first = jnp.concatenate([jnp.ones((1,), jnp.int32),
                                 (blk_exp[1:] != blk_exp[:-1]).astype(jnp.int32)])
    return dest.T.astype(jnp.int32), blk_exp, blk_valid, blk_first


def _moe_experts(x, g, mod, l, mi, mf, counts, w1, w3, w2):
    dest_t, blk_exp, blk_valid, blk_first = _moe_plan(mi, counts)
    xs = _dispatch(dest_t, x, g, mod, l)
    y = _experts(blk_exp, blk_valid, blk_first, xs, w1, w3, w2, l)
    return dest_t, mf, y


def kernel(x_prompt, x_sample, cache_k_na, cache_v_na, cache_k_swa, cache_v_swa, state_hgrn, c, c_ctx, w_mod, b_mod, norm_mix_g, norm_ffn_g, w_in_even, w_out_even, na_rel_bias, swa_sink, w_in_odd, w_out_odd, conv_w, conv_b, conv_norm_g, conv_norm_b, hgrn_lb_raw, hgrn_norm_g, router_group_w, router_group_b, router_expert_w, router_expert_b, moe_w1, moe_w3, moe_w2, final_norm_g):
    x = jnp.concatenate([x_prompt.reshape(NT_CTX, D), x_sample.reshape(NT_LAT, D)], axis=0)
    cvec = jnp.zeros((16, D), F32).at[0].set(c_ctx).at[1:1 + DEC_BATCH].set(c)
    mod = _mod_table(cvec, w_mod, b_mod)

    lb_p = jax.nn.softmax(hgrn_lb_raw.astype(F32), axis=0)
    lower_bounds = jnp.cumsum(lb_p, axis=0) - lb_p[0:1]
    cos, sin = _rope_tables()
    cache_v_swa_ext = _swa_value_ext(cache_v_swa)

    k_na, v_na, k_swa, v_swa, s_hg = [], [], [], [], []
    moe = None
    for l in range(DEPTH):
        w_in = (w_in_even if l % 2 == 0 else w_in_odd)[l // 2].astype(BF16)
        if moe is None:
            p = _modmm(x, norm_mix_g[l], mod, l, w_in, tm=512)
        else:
            x, p = _comb_modmm(*moe[:2], x, mod, l - 1, moe[2], norm_mix_g[l], w_in)
        if l % 2 == 0:
            e = l // 2
            o = _ctx_attn(p, swa_sink[e])
            o = _na_attn(p, cache_k_na, cache_v_na, _na_bias_table(na_rel_bias[e]), e, o)
            o = _swa_attn(p, cache_k_swa, cache_v_swa_ext, swa_sink[e], cos, sin, e, o)
            pc = p[:NT_CTX]
            to_heads = lambda u, nh: u.reshape(BATCH, SEQ, nh, HEAD_DIM).transpose(0, 2, 1, 3)
            k_na.append(to_heads(pc[:, _KA:_KA + 512], NA_HEADS))
            v_na.append(to_heads(pc[:, _VA:_VA + 512], NA_HEADS))
            k_swa.append(to_heads(pc[:, _KB:_KB + 128], SWA_KV_HEADS))
            v_swa.append(to_heads(pc[:, _VB:_VB + 128], SWA_KV_HEADS))
            w_out = w_out_even[e]
        else:
            j = l // 2
            o = _conv_module(p, conv_w[j], conv_b[j], conv_norm_g[j], conv_norm_b[j])
            zero_state = jnp.zeros((BATCH, 2, HG_HEADS, 128, 128), F32)
            o, s_ctx = _hgrn(p, lower_bounds[j], hgrn_norm_g[j], zero_state, o, BATCH, SEQ, 0)
            s0_lat = jnp.swapaxes(state_hgrn[:, j], -1, -2)
            o, _ = _hgrn(p, lower_bounds[j], hgrn_norm_g[j], s0_lat, o, DEC_BATCH, DEC_SEQ, NT_CTX)
            s_hg.append(jnp.swapaxes(s_ctx, -1, -2))
            w_out = w_out_odd[j]
        x, mi, mf, counts = _mmres_router(o, x, mod, l, w_out.astype(BF16), norm_ffn_g[l],
                                          router_group_w[l], router_group_b[l],
                                          router_expert_w[l], router_expert_b[l])
        moe = _moe_experts(x, norm_ffn_g[l], mod, l, mi, mf, counts, moe_w1, moe_w3, moe_w2)

    x = _combine(*moe[:2], x, mod, DEPTH - 1, moe[2])
    y_prompt = _final_norm(x, final_norm_g, 0, NT_CTX).reshape(BATCH, SEQ, D)
    y_sample = _final_norm(x, final_norm_g, NT_CTX, NT_LAT).reshape(DEC_BATCH, DEC_SEQ, D)
    return (y_prompt, y_sample, jnp.stack(k_na, axis=1), jnp.stack(v_na, axis=1),
            jnp.stack(k_swa, axis=1), jnp.stack(v_swa, axis=1), jnp.stack(s_hg, axis=1))
```

```python
import functools

import numpy as np
import jax
import jax.numpy as jnp
from jax import lax
from jax.experimental import pallas as pl
from jax.experimental.pallas import tpu as pltpu

F32 = jnp.float32
BF16 = jnp.bfloat16

D = 1024
BATCH = 16
SEQ = 256
DEPTH = 4
DEC_BATCH = 8
DEC_SEQ = 4096
PAST_LEN = 512
GRID_W = 64
HEAD_DIM = 64
EPS = 1e-6
NA_HEADS = 8
NA_WIN_R = 8
NA_WIN_C = 16
SWA_HEADS = 8
SWA_KV_HEADS = 2
SWA_WINDOW = 128
ROPE_BASE = 10000.0
CONV_CH = 512
CONV_WIDTH = 31
HG_HEADS = 4
HG_DK = 128
HG_CHUNK = 64
N_GROUPS = 4
EXPERTS_PER_GROUP = 8
N_EXPERTS = 32
D_EXPERT = 512
N_EVEN = 2
N_ODD = 2
D_IN_EVEN = 2304
D_IN_ODD = 3584

NT_CTX = BATCH * SEQ
NT_LAT = DEC_BATCH * DEC_SEQ
NT = NT_CTX + NT_LAT
SEG = 4096
assert NT_CTX == SEG and DEC_SEQ == SEG
N_MOD = 1 + DEC_BATCH
MASK = -1e30

MOE_MB = 512
MOE_NB = (2 * NT) // MOE_MB + N_EXPERTS
MOE_SLOTS = MOE_NB * MOE_MB

VMEM_LIMIT = 56 * 1024 * 1024


def _cp(sem, vmem=VMEM_LIMIT):
    return pltpu.CompilerParams(dimension_semantics=sem, vmem_limit_bytes=vmem)


def _dot(a, b):
    return jnp.dot(a, b, preferred_element_type=F32)


def _dot_nt(a, b):
    return lax.dot_general(a, b, (((1,), (1,)), ((), ())), preferred_element_type=F32)


def _dot_tn(a, b):
    return lax.dot_general(a, b, (((0,), (0,)), ((), ())), preferred_element_type=F32)


def _split2(a):
    hi = a.astype(BF16)
    lo = (a - hi.astype(F32)).astype(BF16)
    return hi, lo


def _dot3(a, b_hi, b_lo):
    a_hi, a_lo = _split2(a)
    return _dot(a_hi, b_hi) + (_dot(a_hi, b_lo) + _dot(a_lo, b_hi))


def _sigmoid(x):
    return 1.0 / (1.0 + jnp.exp(-x))


def _modulate(x, g, shift, scale):
    ms = jnp.mean(x * x, axis=-1, keepdims=True)
    return (x * lax.rsqrt(ms + EPS) * g) * (1.0 + scale) + shift


def _mod_kernel(c_ref, w_ref, b_ref, o_ref):
    cv = c_ref[...]
    s = cv * _sigmoid(cv)
    w_hi, w_lo = _split2(w_ref[...])
    o_ref[...] = _dot3(s, w_hi, w_lo) + b_ref[...]


def _mod_table(cvec, w_mod, b_mod):
    tn = 1536
    out = pl.pallas_call(
        _mod_kernel,
        grid=(DEPTH, 6 * D // tn),
        in_specs=[
            pl.BlockSpec((16, D), lambda l, j: (0, 0)),
            pl.BlockSpec((None, D, tn), lambda l, j: (l, 0, j)),
            pl.BlockSpec((None, 1, tn), lambda l, j: (l, 0, j)),
        ],
        out_specs=pl.BlockSpec((None, 16, tn), lambda l, j: (l, 0, j)),
        out_shape=jax.ShapeDtypeStruct((DEPTH, 16, 6 * D), F32),
        compiler_params=_cp(("arbitrary", "arbitrary")),
        name="mod_table",
    )(cvec, w_mod, b_mod.reshape(DEPTH, 1, 6 * D))
    return out.reshape(DEPTH, 16, 1, 6 * D)


def _mod_spec(l, which, tm):
    return pl.BlockSpec((None, None, 1, D), lambda i, *_: (l, (i * tm) // SEG, 0, which))


def _modmm_kernel(x_ref, g_ref, sh_ref, sc_ref, w_ref, o_ref, *, nchunk):
    h = _modulate(x_ref[...], g_ref[...], sh_ref[...], sc_ref[...]).astype(BF16)
    n = o_ref.shape[1]
    for n0 in range(0, n, nchunk):
        o_ref[:, n0:n0 + nchunk] = _dot(h, w_ref[:, n0:n0 + nchunk])


def _modmm(x, g, mod, l, w_bf16, tm):
    n = w_bf16.shape[1]
    return pl.pallas_call(
        functools.partial(_modmm_kernel, nchunk=256),
        grid=(NT // tm,),
        in_specs=[
            pl.BlockSpec((tm, D), lambda i: (i, 0)),
            pl.BlockSpec((1, D), lambda i: (0, 0)),
            _mod_spec(l, 0, tm),
            _mod_spec(l, 1, tm),
            pl.BlockSpec((D, n), lambda i: (0, 0)),
        ],
        out_specs=pl.BlockSpec((tm, n), lambda i: (i, 0)),
        out_shape=jax.ShapeDtypeStruct((NT, n), F32),
        compiler_params=_cp(("arbitrary",)),
        name="modmm",
    )(x, g.reshape(1, D), mod, mod, w_bf16)


def _final_kernel(x_ref, g_ref, o_ref):
    x = x_ref[...]
    ms = jnp.mean(x * x, axis=-1, keepdims=True)
    o_ref[...] = x * lax.rsqrt(ms + EPS) * g_ref[...]


def _final_norm(x, g, row0, n_rows, tm=512):
    r0 = row0 // tm
    return pl.pallas_call(
        _final_kernel,
        grid=(n_rows // tm,),
        in_specs=[pl.BlockSpec((tm, D), lambda i: (r0 + i, 0)), pl.BlockSpec((1, D), lambda i: (0, 0))],
        out_specs=pl.BlockSpec((tm, D), lambda i: (i, 0)),
        out_shape=jax.ShapeDtypeStruct((n_rows, D), F32),
        compiler_params=_cp(("arbitrary",)),
        name="final_norm",
    )(x, g.reshape(1, D))


_QA, _KA, _VA, _QB, _KB, _VB = 0, 512, 1024, 1536, 2048, 2176


def _ctx_attn_kernel(sink_ref, p_ref, o_ref):
    scale = HEAD_DIM ** -0.5

    def head(qc, kc, vc, sink):
        q = p_ref[:, qc:qc + 64].astype(BF16)
        k = p_ref[:, kc:kc + 64].astype(BF16)
        v = p_ref[:, vc:vc + 64].astype(BF16)
        s = _dot_nt(q, k) * scale
        m = jnp.max(s, axis=-1, keepdims=True)
        if sink is not None:
            m = jnp.maximum(m, sink)
        p = jnp.exp(s - m)
        den = jnp.sum(p, axis=-1, keepdims=True)
        if sink is not None:
            den = den + jnp.exp(sink - m)
        return _dot(p.astype(BF16), v) / den

    for h in range(NA_HEADS):
        o_ref[:, 64 * h:64 * h + 64] = head(_QA + 64 * h, _KA + 64 * h, _VA + 64 * h, None)
    for j in range(SWA_HEADS):
        kv = j // (SWA_HEADS // SWA_KV_HEADS)
        o_ref[:, 512 + 64 * j:512 + 64 * j + 64] = head(_QB + 64 * j, _KB + 64 * kv, _VB + 64 * kv, sink_ref[j])


def _ctx_attn(p, sink):
    return pl.pallas_call(
        _ctx_attn_kernel,
        grid=(BATCH,),
        in_specs=[
            pl.BlockSpec(memory_space=pltpu.SMEM),
            pl.BlockSpec((SEQ, D_IN_EVEN), lambda b: (b, 0)),
        ],
        out_specs=pl.BlockSpec((SEQ, D), lambda b: (b, 0)),
        out_shape=jax.ShapeDtypeStruct((NT, D), F32),
        compiler_params=_cp(("arbitrary",)),
        name="ctx_attn",
    )(sink, p)


NA_QROWS = 8
NA_BAND = 16
NA_TQ = NA_QROWS * GRID_W
NA_TK = NA_BAND * GRID_W
NA_RB = (DEC_SEQ // GRID_W) // NA_QROWS


def _na_band_start(rb):
    rows = DEC_SEQ // GRID_W
    return int(np.clip(NA_QROWS * rb - NA_WIN_R // 2, 0, rows - NA_BAND))


def _na_row_index():
    rows = DEC_SEQ // GRID_W
    n_dr = 2 * NA_WIN_R - 1
    out = np.full((3, NA_QROWS, NA_BAND), n_dr, np.int32)
    for ci, rb in enumerate((0, 1, NA_RB - 1)):
        for ql in range(NA_QROWS):
            qrow = NA_QROWS * rb + ql
            r0 = int(np.clip(qrow - NA_WIN_R // 2, 0, rows - NA_WIN_R))
            for kl in range(NA_BAND):
                krow = _na_band_start(rb) + kl
                if r0 <= krow < r0 + NA_WIN_R:
                    out[ci, ql, kl] = krow - qrow + (NA_WIN_R - 1)
    return out


def _na_bias_table(rel_bias):
    h = rel_bias.shape[0]
    n_dr, n_dc = 2 * NA_WIN_R - 1, 2 * NA_WIN_C - 1
    cols = np.arange(GRID_W)
    c0 = np.clip(cols - NA_WIN_C // 2, 0, GRID_W - NA_WIN_C)
    vc = (cols[None, :] >= c0[:, None]) & (cols[None, :] < c0[:, None] + NA_WIN_C)
    dc = cols[None, :] - cols[:, None] + (NA_WIN_C - 1)
    sel = ((np.arange(n_dc)[:, None, None] == dc[None]) & vc[None]).astype(np.float32)
    bc = jnp.dot(rel_bias.reshape(h * n_dr, n_dc), jnp.asarray(sel.reshape(n_dc, -1)),
                 precision=lax.Precision.HIGHEST).reshape(h, n_dr, GRID_W, GRID_W)
    bc = jnp.where(jnp.asarray(vc), bc, MASK)
    bc = jnp.concatenate([bc, jnp.full((h, 1, GRID_W, GRID_W), MASK, F32)], axis=1)
    tab = jnp.take(bc, jnp.asarray(_na_row_index()), axis=1)
    return tab.transpose(0, 1, 2, 4, 3, 5).reshape(h // 2, 2, 3, NA_TQ, NA_TK)


def _na_kernel(q_ref, k_ref, v_ref, kc_ref, vc_ref, bias_ref, oin_ref, o_ref):
    del oin_ref
    rb = pl.program_id(2)
    scale = HEAD_DIM ** -0.5
    start = jnp.clip(NA_QROWS * rb - NA_WIN_R // 2, 0, DEC_SEQ // GRID_W - NA_BAND) * GRID_W
    start = pl.multiple_of(start, 256)
    case = jnp.where(rb == 0, 0, jnp.where(rb == NA_RB - 1, 2, 1))
    kb = k_ref[pl.ds(start, NA_TK), :].astype(BF16)
    vb = v_ref[pl.ds(start, NA_TK), :].astype(BF16)
    q = q_ref[...] * scale
    lane = lax.broadcasted_iota(jnp.int32, (1, 128), 1)
    hs = range(2)
    s_loc = [_dot_nt(jnp.where((lane >> 6) == hh, q, 0.0).astype(BF16), kb) + bias_ref[hh, case] for hh in hs]
    s_ctx = [_dot_nt(q[:, 64 * hh:64 * hh + 64].astype(BF16), kc_ref[hh].astype(BF16)) for hh in hs]
    m = [jnp.maximum(jnp.max(s_loc[hh], axis=-1, keepdims=True), jnp.max(s_ctx[hh], axis=-1, keepdims=True))
         for hh in hs]
    p_loc = [jnp.exp(s_loc[hh] - m[hh]) for hh in hs]
    p_ctx = [jnp.exp(s_ctx[hh] - m[hh]) for hh in hs]
    inv = [1.0 / (jnp.sum(p_loc[hh], axis=-1, keepdims=True) + jnp.sum(p_ctx[hh], axis=-1, keepdims=True))
           for hh in hs]
    o_loc = [_dot(p_loc[hh].astype(BF16), vb) * inv[hh] for hh in hs]
    o_ctx = [_dot(p_ctx[hh].astype(BF16), vc_ref[hh].astype(BF16)) * inv[hh] for hh in hs]
    o_ref[...] = jnp.where((lane >> 6) == 0, o_loc[0], o_loc[1]) + jnp.concatenate(o_ctx, axis=-1)


def _na_attn(p, cache_k, cache_v, bias_tab, e, o_full):
    n_hp = NA_HEADS // 2
    lat0 = NT_CTX // NA_TQ
    per_b = DEC_SEQ // NA_TQ
    return pl.pallas_call(
        _na_kernel,
        grid=(n_hp, DEC_BATCH, NA_RB),
        in_specs=[
            pl.BlockSpec((NA_TQ, 128), lambda hp, b, rb: (lat0 + b * per_b + rb, _QA // 128 + hp)),
            pl.BlockSpec((DEC_SEQ, 128), lambda hp, b, rb: (1 + b, _KA // 128 + hp)),
            pl.BlockSpec((DEC_SEQ, 128), lambda hp, b, rb: (1 + b, _VA // 128 + hp)),
            pl.BlockSpec((None, None, 2, PAST_LEN, HEAD_DIM), lambda hp, b, rb: (b, e, hp, 0, 0)),
            pl.BlockSpec((None, None, 2, PAST_LEN, HEAD_DIM), lambda hp, b, rb: (b, e, hp, 0, 0)),
            pl.BlockSpec((None, 2, 3, NA_TQ, NA_TK), lambda hp, b, rb: (hp, 0, 0, 0, 0)),
            pl.BlockSpec(memory_space=pl.ANY),
        ],
        out_specs=pl.BlockSpec((NA_TQ, 128), lambda hp, b, rb: (lat0 + b * per_b + rb, hp)),
        out_shape=jax.ShapeDtypeStruct((NT, D), F32),
        input_output_aliases={6: 0},
        compiler_params=_cp(("arbitrary", "arbitrary", "arbitrary")),
        name="na_attn",
    )(p, p, p, cache_k, cache_v, bias_tab, o_full)


SWA_TQ = 256
SWA_TK = SWA_TQ + 2 * SWA_WINDOW


def _rope_tables():
    pos = np.arange(DEC_SEQ)
    row = (pos // GRID_W).astype(np.float32)
    col = (pos % GRID_W).astype(np.float32)
    half = HEAD_DIM // 2
    inv = jnp.asarray(ROPE_BASE, F32) ** (-jnp.arange(0, half, 2, dtype=F32) / half)
    ar = jnp.asarray(row)[:, None] * inv[None, :]
    ac = jnp.asarray(col)[:, None] * inv[None, :]
    cos = jnp.concatenate([jnp.cos(ar), jnp.cos(ar), jnp.cos(ac), jnp.cos(ac)], axis=1)
    sin = jnp.concatenate([-jnp.sin(ar), jnp.sin(ar), -jnp.sin(ac), jnp.sin(ac)], axis=1)
    return jnp.tile(cos, (1, 2)), jnp.tile(sin, (1, 2))


def _rope128(t, cos, sin):
    lane = lax.broadcasted_iota(jnp.int32, (1, 128), 1)
    up = pltpu.roll(t, 112, 1)
    dn = pltpu.roll(t, 16, 1)
    sw = jnp.where((lane & 31) < 16, up, dn)
    return t * cos + sw * sin


def _swa_mask_table():
    out = []
    for qb in (0, 1, DEC_SEQ // SWA_TQ - 1):
        q0 = qb * SWA_TQ
        bs = int(np.clip(q0 - SWA_WINDOW, 0, DEC_SEQ - SWA_TK))
        d = (q0 + np.arange(SWA_TQ))[:, None] - (bs + np.arange(SWA_TK))[None, :]
        out.append(np.where(np.abs(d) <= SWA_WINDOW, 0.0, MASK))
    return np.stack(out).astype(np.float32)


def _swa_kernel(sink_ref, q_ref, k_ref, v_ref, kc_ref, vc_ref, cos_ref, sin_ref, mask_ref, oin_ref, o_ref,
                kr_ref):
    del oin_ref
    qb = pl.program_id(1)
    scale = HEAD_DIM ** -0.5
    g = SWA_HEADS // SWA_KV_HEADS

    @pl.when(qb == 0)
    def _():
        kr_ref[...] = _rope128(k_ref[...], cos_ref[...], sin_ref[...]).astype(BF16)

    q0 = pl.multiple_of(qb * SWA_TQ, SWA_TQ)
    bs = pl.multiple_of(jnp.clip(q0 - SWA_WINDOW, 0, DEC_SEQ - SWA_TK), 128)
    cq = cos_ref[pl.ds(q0, SWA_TQ), :]
    sq = sin_ref[pl.ds(q0, SWA_TQ), :]
    kb = kr_ref[pl.ds(bs, SWA_TK), :]
    vb = v_ref[pl.ds(bs, SWA_TK), :].astype(BF16)
    qr = [_rope128(q_ref[:, 128 * j:128 * j + 128], cq, sq) * scale for j in range(SWA_HEADS // 2)]
    case = jnp.where(qb == 0, 0, jnp.where(qb == pl.num_programs(1) - 1, 2, 1))
    wmask = mask_ref[case]
    rowc = lax.broadcasted_iota(jnp.int32, (g * SWA_TQ, 1), 0) >> 8
    lane_half = lax.broadcasted_iota(jnp.int32, (1, 128), 1) >> 6

    kvs = range(SWA_KV_HEADS)
    q4, sink = [], []
    for kvh in kvs:
        heads = []
        sk = jnp.zeros((g * SWA_TQ, 1), F32)
        for gi in range(g):
            hq = kvh * g + gi
            heads.append(qr[hq // 2][:, 64 * (hq % 2):64 * (hq % 2) + 64])
            sk = jnp.where(rowc == gi, sink_ref[hq], sk)
        q4.append(jnp.concatenate(heads, axis=0).astype(BF16))
        sink.append(sk)
    s_loc = [(_dot_nt(q4[kvh], kb[:, 64 * kvh:64 * kvh + 64]).reshape(g, SWA_TQ, SWA_TK)
              + wmask[None]).reshape(g * SWA_TQ, SWA_TK) for kvh in kvs]
    s_ctx = [_dot_nt(q4[kvh], kc_ref[kvh].astype(BF16)) for kvh in kvs]
    m = [jnp.maximum(jnp.max(jnp.maximum(s_loc[kvh], s_ctx[kvh]), axis=-1, keepdims=True), sink[kvh])
         for kvh in kvs]
    p_loc = [jnp.exp(s_loc[kvh] - m[kvh]).astype(BF16) for kvh in kvs]
    p_ctx = [jnp.exp(s_ctx[kvh] - m[kvh]).astype(BF16) for kvh in kvs]
    pv = [_dot(p_loc[kvh], jnp.where(lane_half == kvh, vb, jnp.ones_like(vb)))
          + _dot(p_ctx[kvh], vc_ref[kvh].astype(BF16)) for kvh in kvs]
    for kvh in kvs:
        lo, hi = 64 * kvh, 64 * (1 - kvh)
        den = pv[kvh][:, hi:hi + 1] + jnp.exp(sink[kvh] - m[kvh])
        o = pv[kvh][:, lo:lo + 64] / den
        for gi in range(g):
            hq = kvh * g + gi
            o_ref[:, 64 * hq:64 * hq + 64] = o[SWA_TQ * gi:SWA_TQ * (gi + 1), :]


def _swa_value_ext(cache_v):
    ones = jnp.ones_like(cache_v[..., 0, :, :])
    return jnp.stack([jnp.concatenate([cache_v[..., 0, :, :], ones], axis=-1),
                      jnp.concatenate([ones, cache_v[..., 1, :, :]], axis=-1)], axis=-3)


def _swa_attn(p, cache_k, cache_v_ext, sink, cos, sin, e, o_full):
    assert SWA_TQ == 256
    lat0 = NT_CTX // SWA_TQ
    per_b = DEC_SEQ // SWA_TQ
    return pl.pallas_call(
        _swa_kernel,
        grid=(DEC_BATCH, per_b),
        in_specs=[
            pl.BlockSpec(memory_space=pltpu.SMEM),
            pl.BlockSpec((SWA_TQ, 512), lambda b, qb: (lat0 + b * per_b + qb, _QB // 512)),
            pl.BlockSpec((DEC_SEQ, 128), lambda b, qb: (1 + b, _KB // 128)),
            pl.BlockSpec((DEC_SEQ, 128), lambda b, qb: (1 + b, _VB // 128)),
            pl.BlockSpec((None, None, 2, PAST_LEN, HEAD_DIM), lambda b, qb: (b, e, 0, 0, 0)),
            pl.BlockSpec((None, None, 2, PAST_LEN, 128), lambda b, qb: (b, e, 0, 0, 0)),
            pl.BlockSpec((DEC_SEQ, 128), lambda b, qb: (0, 0)),
            pl.BlockSpec((DEC_SEQ, 128), lambda b, qb: (0, 0)),
            pl.BlockSpec((3, SWA_TQ, SWA_TK), lambda b, qb: (0, 0, 0)),
            pl.BlockSpec(memory_space=pl.ANY),
        ],
        out_specs=pl.BlockSpec((SWA_TQ, 512), lambda b, qb: (lat0 + b * per_b + qb, 1)),
        out_shape=jax.ShapeDtypeStruct((NT, D), F32),
        scratch_shapes=[pltpu.VMEM((DEC_SEQ, 128), BF16)],
        input_output_aliases={9: 0},
        compiler_params=_cp(("arbitrary", "arbitrary")),
        name="swa_attn",
    )(sink, p, p, p, cache_k, cache_v_ext, cos, sin, jnp.asarray(_swa_mask_table()), o_full)


CONV_TT = 256
CONV_HALO = 16
CONV_RC = 32


def _conv_kernel(u_ref, ul_ref, ur_ref, w_ref, cb_ref, lg_ref, lb_ref, o_ref, xs_ref, wb_ref):
    i = pl.program_id(0)

    @pl.when(i == 0)
    def _():
        for t in range(CONV_WIDTH):
            wb_ref[t] = jnp.broadcast_to(w_ref[t:t + 1, :], (8, CONV_CH))

    n_ctx_tiles = NT_CTX // CONV_TT
    per_seq = DEC_SEQ // CONV_TT
    j = i - n_ctx_tiles
    is_ctx = i < n_ctx_tiles
    is_start = is_ctx | ((j % per_seq) == 0)
    is_end = is_ctx | ((j % per_seq) == per_seq - 1)

    def glu(u):
        return u[:, :CONV_CH] * _sigmoid(u[:, CONV_CH:])

    xs_ref[0, CONV_HALO:CONV_HALO + CONV_TT, :] = glu(u_ref[...])
    xs_ref[0, 0:CONV_HALO, :] = jnp.where(is_start, 0.0, glu(ul_ref[...]))
    xs_ref[0, CONV_HALO + CONV_TT:, :] = jnp.where(is_end, 0.0, glu(ur_ref[...]))
    n_rows = CONV_TT + 2 * CONV_HALO
    xs0 = xs_ref[0]
    for b in range(1, 8):
        xs_ref[b] = pltpu.roll(xs0, n_rows - b, 0)

    pad = CONV_WIDTH // 2
    for c in range(CONV_TT // CONV_RC):
        base = CONV_HALO + c * CONV_RC - pad
        acc = jnp.zeros((CONV_RC, CONV_CH), F32)
        for t in range(CONV_WIDTH):
            off = base + t
            acc = acc + (xs_ref[off % 8, off - off % 8:off - off % 8 + CONV_RC, :]
                         * jnp.tile(wb_ref[t], (CONV_RC // 8, 1)))
        y = acc + cb_ref[...]
        mu = jnp.mean(y, axis=-1, keepdims=True)
        yc = y - mu
        var = jnp.mean(yc * yc, axis=-1, keepdims=True)
        yn = yc * lax.rsqrt(var + EPS) * lg_ref[...] + lb_ref[...]
        o_ref[c * CONV_RC:(c + 1) * CONV_RC, :] = yn * _sigmoid(yn)


def _conv_module(p, conv_w, conv_b, ln_g, ln_b):
    nh = CONV_TT // CONV_HALO
    last = NT // CONV_HALO - 1
    return pl.pallas_call(
        _conv_kernel,
        grid=(NT // CONV_TT,),
        in_specs=[
            pl.BlockSpec((CONV_TT, 2 * CONV_CH), lambda i: (i, 0)),
            pl.BlockSpec((CONV_HALO, 2 * CONV_CH), lambda i: (jnp.maximum(i * nh - 1, 0), 0)),
            pl.BlockSpec((CONV_HALO, 2 * CONV_CH), lambda i: (jnp.minimum((i + 1) * nh, last), 0)),
            pl.BlockSpec((CONV_WIDTH, CONV_CH), lambda i: (0, 0)),
            pl.BlockSpec((1, CONV_CH), lambda i: (0, 0)),
            pl.BlockSpec((1, CONV_CH), lambda i: (0, 0)),
            pl.BlockSpec((1, CONV_CH), lambda i: (0, 0)),
        ],
        out_specs=pl.BlockSpec((CONV_TT, CONV_CH), lambda i: (i, 0)),
        out_shape=jax.ShapeDtypeStruct((NT, D), F32),
        scratch_shapes=[pltpu.VMEM((8, CONV_TT + 2 * CONV_HALO, CONV_CH), F32),
                        pltpu.VMEM((CONV_WIDTH, 8, CONV_CH), F32)],
        compiler_params=_cp(("arbitrary",)),
        name="conv_module",
    )(p, p, p, conv_w, conv_b.reshape(1, -1), ln_g.reshape(1, -1), ln_b.reshape(1, -1))


_U, _HQ, _HI, _HFF, _HFB, _HGT = 0, 1024, 1536, 2048, 2560, 3072
HG_LEVELS = 6
HG_W = HG_HEADS * HG_DK


def _hgrn_constants():
    c = HG_CHUNK
    mm = np.zeros((HG_LEVELS + 1, c, c), np.float32)
    for lv in range(HG_LEVELS):
        n = c >> lv
        half = n // 2
        for t in range(c):
            for s in range(c):
                if t // n == s // n and t % n >= half and s % n < half:
                    mm[lv, t, s] = 1.0
    mm[HG_LEVELS] = np.eye(c, dtype=np.float32)
    tri = np.tril(np.ones((c, c), np.float32))
    return np.stack([tri, tri.T]), np.stack([mm, mm[:, ::-1, ::-1]])


def _hgrn_level_decays(cum, forget, cum_ref, fwd):
    c = HG_CHUNK
    cum_ref[...] = cum
    out = []
    for n in (64, 32, 16):
        pieces = []
        for b in range(c // n):
            r = b * n + n // 2 - (1 if fwd else 0)
            ref = jnp.broadcast_to(cum_ref[r:r + 1, :], (n // 2, HG_W))
            lo = cum[b * n:b * n + n // 2]
            hi = cum[b * n + n // 2:(b + 1) * n]
            pieces += [ref - lo, hi - ref] if fwd else [lo - ref, ref - hi]
        out.append(jnp.exp(jnp.concatenate(pieces, axis=0)))
    rows = [b * 8 + 4 - (1 if fwd else 0) for b in range(c // 8)]
    ref8 = jnp.concatenate([jnp.broadcast_to(cum_ref[r:r + 1, :], (8, HG_W)) for r in rows], axis=0)
    out.append(jnp.exp(-jnp.abs(cum - ref8)))
    row = lax.broadcasted_iota(jnp.int32, (c, HG_W), 0)
    p = 2 - (1 if fwd else 0)
    ref4 = None
    for j in range(4):
        shift = (j - p) % c
        cand = cum if shift == 0 else pltpu.roll(cum, shift, 0)
        ref4 = cand if ref4 is None else jnp.where((row & 3) == j, cand, ref4)
    out.append(jnp.exp(-jnp.abs(cum - ref4)))
    out.append(jnp.where((row & 1) == (1 if fwd else 0), forget, 1.0))
    return out


def _hgrn_gates(q, f, lb, tri):
    qs = q * _sigmoid(q)
    forget = lb + (1.0 - lb) * _sigmoid(f)
    g = jnp.log(forget)
    k = (1.0 - lb) * _sigmoid(-f)
    g_hi = g.astype(BF16)
    r1 = g - g_hi.astype(F32)
    g_mid = r1.astype(BF16)
    g_lo = (r1 - g_mid.astype(F32)).astype(BF16)
    cum = _dot(tri, g_hi) + (_dot(tri, g_mid) + _dot(tri, g_lo))
    return qs, k, forget, cum


def _hgrn_operands(qs, k, forget, v, cum, cum_ref, fwd):
    c = HG_CHUNK
    decays = _hgrn_level_decays(cum, forget, cum_ref, fwd)
    end_row = c - 1 if fwd else 0
    ecum = jnp.exp(cum)
    eend = jnp.exp(-jnp.abs(cum - jnp.broadcast_to(cum_ref[end_row:end_row + 1, :], (c, HG_W))))
    qb, kb = qs.astype(BF16), k.astype(BF16)
    qk = [(qb, kb)]
    for el in decays:
        elb = el.astype(BF16)
        qk.append((qb * elb, kb * elb))
    return dict(qk=qk, vb=v.astype(BF16), qc=(qs * ecum).astype(BF16), kend=(k * eend).astype(BF16),
                dec=ecum[end_row:end_row + 1, :])


def _hgrn_scores(ops, masks):
    out = []
    for h in range(HG_HEADS):
        sl = slice(h * HG_DK, (h + 1) * HG_DK)
        prods = [_dot_nt(qb[:, sl], kb[:, sl]) for qb, kb in ops["qk"]]
        a = masks[HG_LEVELS] * prods[0]
        for lv in range(HG_LEVELS):
            a = a + masks[lv] * prods[lv + 1]
        out.append(a.astype(BF16))
    return out


def _hgrn_outputs(ops, scores, st_ref):
    outs = []
    for h in range(HG_HEADS):
        sl = slice(h * HG_DK, (h + 1) * HG_DK)
        s_t = st_ref[h]
        outs.append(_dot(scores[h], ops["vb"][:, sl]) + _dot_nt(ops["qc"][:, sl], s_t.astype(BF16)))
        st_ref[h] = s_t * ops["dec"][:, sl] + _dot_tn(ops["vb"][:, sl], ops["kend"][:, sl])
    return jnp.concatenate(outs, axis=1)


def _hgrn_kernel(qf_ref, if_ref, ff_ref, qb_ref, ib_ref, fb_ref, lb_ref, s0_ref, tri_ref, m_ref,
                 of_ref, ob_ref, sfin_ref, stf_ref, stb_ref, cumf_ref, cumb_ref, *, tile):
    c = HG_CHUNK
    nct = tile // c
    tt = pl.program_id(1)

    @pl.when(tt == 0)
    def _():
        stf_ref[...] = s0_ref[0]
        stb_ref[...] = s0_ref[1]

    lbf = lb_ref[0:1, :]
    lbb = lb_ref[1:2, :]

    masks_f = [m_ref[0, x] for x in range(HG_LEVELS + 1)]
    masks_b = [m_ref[1, x] for x in range(HG_LEVELS + 1)]

    def body(pi, carry):
        rows = []
        for u in range(2):
            ci = 2 * pi + u
            rows.append((pl.multiple_of(ci * c, c), pl.multiple_of((nct - 1 - ci) * c, c)))
        gates = [(_hgrn_gates(qf_ref[pl.ds(rf, c), :], ff_ref[pl.ds(rf, c), :], lbf, tri_ref[0]),
                  _hgrn_gates(qb_ref[pl.ds(rb, c), :], fb_ref[pl.ds(rb, c), :], lbb, tri_ref[1]))
                 for rf, rb in rows]
        ops = [(_hgrn_operands(*gates[u][0][:3], if_ref[pl.ds(rf, c), :], gates[u][0][3], cumf_ref.at[u], True),
                _hgrn_operands(*gates[u][1][:3], ib_ref[pl.ds(rb, c), :], gates[u][1][3], cumb_ref.at[u], False))
               for u, (rf, rb) in enumerate(rows)]
        scores = [(_hgrn_scores(ops[u][0], masks_f), _hgrn_scores(ops[u][1], masks_b)) for u in range(2)]
        for u, (rf, rb) in enumerate(rows):
            o_f = _hgrn_outputs(ops[u][0], scores[u][0], stf_ref)
            o_b = _hgrn_outputs(ops[u][1], scores[u][1], stb_ref)
            of_ref[pl.ds(rf, c), :] = o_f
            ob_ref[pl.ds(rb, c), :] = o_b
        return carry

    lax.fori_loop(0, nct // 2, body, 0)

    @pl.when(tt == pl.num_programs(1) - 1)
    def _():
        sfin_ref[0] = stf_ref[...]
        sfin_ref[1] = stb_ref[...]


def _hgrn_scan(p, lb, s0_t, n_batch, t_len, row0):
    tri, mm = _hgrn_constants()
    tri = jnp.asarray(tri, BF16)
    mm = jnp.asarray(mm, F32)
    tile = min(t_len, 512)
    ntt = t_len // tile
    r0 = row0 // tile

    def fwd(c0):
        return pl.BlockSpec((tile, HG_W), lambda b, t: (r0 + b * ntt + t, c0 // HG_W))

    def bwd(c0):
        return pl.BlockSpec((tile, HG_W), lambda b, t: (r0 + b * ntt + ntt - 1 - t, c0 // HG_W))

    st_spec = pl.BlockSpec((None, 2, HG_HEADS, 128, 128), lambda b, t: (b, 0, 0, 0, 0))
    n_rows = n_batch * t_len
    return pl.pallas_call(
        functools.partial(_hgrn_kernel, tile=tile),
        grid=(n_batch, ntt),
        in_specs=[
            fwd(_HQ), fwd(_HI), fwd(_HFF), bwd(_HQ), bwd(_HI), bwd(_HFB),
            pl.BlockSpec((2, HG_W), lambda b, t: (0, 0)),
            st_spec,
            pl.BlockSpec((2, HG_CHUNK, HG_CHUNK), lambda b, t: (0, 0, 0)),
            pl.BlockSpec((2, HG_LEVELS + 1, HG_CHUNK, HG_CHUNK), lambda b, t: (0, 0, 0, 0)),
        ],
        out_specs=[
            pl.BlockSpec((tile, HG_W), lambda b, t: (b * ntt + t, 0)),
            pl.BlockSpec((tile, HG_W), lambda b, t: (b * ntt + ntt - 1 - t, 0)),
            st_spec,
        ],
        out_shape=[jax.ShapeDtypeStruct((n_rows, HG_W), F32),
                   jax.ShapeDtypeStruct((n_rows, HG_W), F32),
                   jax.ShapeDtypeStruct((n_batch, 2, HG_HEADS, 128, 128), F32)],
        scratch_shapes=[pltpu.VMEM((HG_HEADS, 128, 128), F32), pltpu.VMEM((HG_HEADS, 128, 128), F32),
                        pltpu.VMEM((2, HG_CHUNK, HG_W), F32), pltpu.VMEM((2, HG_CHUNK, HG_W), F32)],
        compiler_params=_cp(("arbitrary", "arbitrary")),
        name="hgrn_scan",
    )(p, p, p, p, p, p, lb, s0_t, tri, mm)


def _hgrn_fin_kernel(of_ref, ob_ref, gt_ref, ng_ref, oin_ref, o_ref):
    del oin_ref
    od = of_ref[...] + ob_ref[...]
    gt = gt_ref[...]
    gate = gt * _sigmoid(gt)
    for h in range(HG_HEADS):
        sl = slice(h * HG_DK, (h + 1) * HG_DK)
        x = od[:, sl]
        ms = jnp.mean(x * x, axis=-1, keepdims=True)
        o_ref[:, sl] = x * lax.rsqrt(ms + EPS) * ng_ref[...] * gate[:, sl]


def _hgrn_finish(o_f, o_b, p, norm_g, o_full, row0, tm=512):
    n_rows = o_f.shape[0]
    r0 = row0 // tm
    return pl.pallas_call(
        _hgrn_fin_kernel,
        grid=(n_rows // tm,),
        in_specs=[
            pl.BlockSpec((tm, HG_W), lambda i: (i, 0)),
            pl.BlockSpec((tm, HG_W), lambda i: (i, 0)),
            pl.BlockSpec((tm, HG_W), lambda i: (r0 + i, _HGT // HG_W)),
            pl.BlockSpec((1, 128), lambda i: (0, 0)),
            pl.BlockSpec(memory_space=pl.ANY),
        ],
        out_specs=pl.BlockSpec((tm, HG_W), lambda i: (r0 + i, 1)),
        out_shape=jax.ShapeDtypeStruct((NT, D), F32),
        input_output_aliases={4: 0},
        compiler_params=_cp(("arbitrary",)),
        name="hgrn_finish",
    )(o_f, o_b, p, norm_g.reshape(1, 128), o_full)


def _hgrn(p, lb, norm_g, s0_t, o_full, n_batch, t_len, row0):
    o_f, o_b, s_fin = _hgrn_scan(p, lb, s0_t, n_batch, t_len, row0)
    return _hgrn_finish(o_f, o_b, p, norm_g, o_full, row0), s_fin


ROUTE_TM = 512


def _mmres_router_kernel(o_ref, x_ref, gate_ref, w_ref, g_ref, sh_ref, sc_ref, whi_ref, br_ref,
                         tri_ref, xo_ref, mi_ref, mf_ref, cnt_ref, carry_ref):
    i = pl.program_id(0)

    @pl.when(i == 0)
    def _():
        carry_ref[...] = jnp.zeros_like(carry_ref)

    xn = x_ref[...] + gate_ref[...] * _dot(o_ref[...].astype(BF16), w_ref[...])
    xo_ref[...] = xn
    h = _modulate(xn, g_ref[...], sh_ref[...], sc_ref[...])
    h_hi, h_lo = _split2(h)
    hw = _dot(h_hi, whi_ref[...])
    logits = hw[:, :128] + (hw[:, 128:] + _dot(h_lo, whi_ref[:, :128])) + br_ref[...]
    lane = lax.broadcasted_iota(jnp.int32, logits.shape, 1)
    neg = jnp.float32(-3e38)
    is_g = lane < N_GROUPS
    lg = jnp.where(is_g, logits, neg)
    mg = jnp.max(lg, axis=-1, keepdims=True)
    grp = jnp.min(jnp.where(lg == mg, lane, 128), axis=-1, keepdims=True)
    pg = 1.0 / jnp.sum(jnp.where(is_g, jnp.exp(lg - mg), 0.0), axis=-1, keepdims=True)
    ex = lane - N_GROUPS
    in_grp = (ex >= 0) & (ex < N_EXPERTS) & ((ex >> 3) == grp)
    le = jnp.where(in_grp, logits, neg)
    v1 = jnp.max(le, axis=-1, keepdims=True)
    i1 = jnp.min(jnp.where(le == v1, lane, 128), axis=-1, keepdims=True)
    le2 = jnp.where(lane == i1, neg, le)
    v2 = jnp.max(le2, axis=-1, keepdims=True)
    i2 = jnp.min(jnp.where(le2 == v2, lane, 128), axis=-1, keepdims=True)
    t = jnp.exp(v2 - v1)
    g1 = pg / (1.0 + t)
    g2 = pg * t / (1.0 + t)
    oh = jnp.where((lane == i1) | (lane == i2), 1.0, 0.0)
    prefix = _dot(tri_ref[...], oh.astype(BF16)) + carry_ref[...]
    r1 = jnp.sum(jnp.where(lane == i1, prefix, 0.0), axis=-1, keepdims=True).astype(jnp.int32)
    r2 = jnp.sum(jnp.where(lane == i2, prefix, 0.0), axis=-1, keepdims=True).astype(jnp.int32)
    carry_ref[...] = carry_ref[...] + jnp.sum(oh, axis=0, keepdims=True)
    cnt_ref[...] = carry_ref[...]
    mi = jnp.where(lane == 0, i1 - N_GROUPS, jnp.where(lane == 1, i2 - N_GROUPS,
                   jnp.where(lane == 2, r1, jnp.where(lane == 3, r2, 0))))
    mi_ref[...] = jnp.transpose(mi)[0:8, :]
    mf_ref[...] = jnp.where(lane == 0, g1, jnp.where(lane == 1, g2, 0.0))


def _router_weights(wg, bg, we, be):
    pad = 128 - N_GROUPS - N_EXPERTS
    wr = jnp.pad(jnp.concatenate([wg, we], axis=-1), ((0, 0), (0, 0), (0, pad)))
    br = jnp.pad(jnp.concatenate([bg, be], axis=-1), ((0, 0), (0, pad)))[:, None, :]
    w_hi = wr.astype(BF16)
    w_lo = (wr - w_hi.astype(F32)).astype(BF16)
    return jnp.concatenate([w_hi, w_lo], axis=-1), br


def _mmres_router(o, x, mod, l, w_out_bf16, g, w_hl, br):
    tm = ROUTE_TM
    tri = jnp.asarray(np.tril(np.ones((tm, tm), np.float32), -1), BF16)
    const = lambda i: (0, 0)
    return pl.pallas_call(
        _mmres_router_kernel,
        grid=(NT // tm,),
        in_specs=[
            pl.BlockSpec((tm, D), lambda i: (i, 0)),
            pl.BlockSpec((tm, D), lambda i: (i, 0)),
            _mod_spec(l, 2, tm),
            pl.BlockSpec((D, D), const),
            pl.BlockSpec((1, D), const),
            _mod_spec(l, 3, tm),
            _mod_spec(l, 4, tm),
            pl.BlockSpec((None, D, 256), lambda i: (l, 0, 0)),
            pl.BlockSpec((None, 1, 128), lambda i: (l, 0, 0)),
            pl.BlockSpec((tm, tm), const),
        ],
        out_specs=[
            pl.BlockSpec((tm, D), lambda i: (i, 0)),
            pl.BlockSpec((8, tm), lambda i: (0, i)),
            pl.BlockSpec((tm, 128), lambda i: (i, 0)),
            pl.BlockSpec((1, 128), const),
        ],
        out_shape=[
            jax.ShapeDtypeStruct((NT, D), F32),
            jax.ShapeDtypeStruct((8, NT), jnp.int32),
            jax.ShapeDtypeStruct((NT, 128), F32),
            jax.ShapeDtypeStruct((1, 128), F32),
        ],
        scratch_shapes=[pltpu.VMEM((1, 128), F32)],
        compiler_params=_cp(("arbitrary",)),
        name="mmres_router",
    )(o, x, mod, w_out_bf16, g.reshape(1, D), mod, mod, w_hl, br, tri)


DISP_TM = 512


def _row_copy(src, s_row, dst, d_row, sem):
    return pltpu.make_async_copy(src.at[pl.ds(s_row, 1)], dst.at[pl.ds(d_row, 1)], sem)


def _dispatch_kernel(dest_ref, x_ref, g_ref, sh_ref, sc_ref, xs_hbm, h_ref, sem):
    h_ref[...] = _modulate(x_ref[...], g_ref[...], sh_ref[...], sc_ref[...])

    def group(t8, start):
        base = pl.multiple_of(t8 * 8, 8)
        for s in range(8):
            for k in range(2):
                cp = _row_copy(h_ref, base + s, xs_hbm, dest_ref[k, base + s], sem)
                if start:
                    cp.start(priority=k)
                else:
                    cp.wait()

    lax.fori_loop(0, DISP_TM // 8, lambda t8, c: (group(t8, True), c)[1], 0)
    lax.fori_loop(0, DISP_TM // 8, lambda t8, c: (group(t8, False), c)[1], 0)


def _dispatch(dest_t, x, g, mod, l):
    return pl.pallas_call(
        _dispatch_kernel,
        grid=(NT // DISP_TM,),
        in_specs=[
            pl.BlockSpec((2, DISP_TM), lambda i: (0, i), memory_space=pltpu.SMEM),
            pl.BlockSpec((DISP_TM, D), lambda i: (i, 0)),
            pl.BlockSpec((1, D), lambda i: (0, 0)),
            _mod_spec(l, 3, DISP_TM),
            _mod_spec(l, 4, DISP_TM),
        ],
        out_specs=pl.BlockSpec(memory_space=pl.ANY),
        out_shape=jax.ShapeDtypeStruct((MOE_SLOTS, D), F32),
        scratch_shapes=[pltpu.VMEM((DISP_TM, D), F32), pltpu.SemaphoreType.DMA(())],
        compiler_params=_cp(("arbitrary",)),
        name="moe_dispatch",
    )(dest_t, x, g.reshape(1, D), mod, mod)


def _expert_kernel(bexp_ref, bval_ref, bfirst_ref, xs_ref, w1_ref, w3_ref, w2_ref, y_ref,
                   w1b_ref, w3b_ref, w2b_ref):
    del bexp_ref
    i = pl.program_id(0)
    nv = bval_ref[i]

    @pl.when(bfirst_ref[i] == 1)
    def _():
        w1b_ref[...] = w1_ref[...].astype(BF16)
        w3b_ref[...] = w3_ref[...].astype(BF16)
        w2b_ref[...] = w2_ref[...].astype(BF16)

    @pl.when(nv > 0)
    def _():
        row = lax.broadcasted_iota(jnp.int32, (MOE_MB, 1), 0)
        xb = jnp.where(row < nv, xs_ref[...], 0.0).astype(BF16)
        a = _dot(xb, w1b_ref[...])
        b = _dot(xb, w3b_ref[...])
        hid = (a * _sigmoid(a) * b).astype(BF16)
        y_ref[...] = _dot(hid, w2b_ref[...])

    @pl.when(nv <= 0)
    def _():
        y_ref[...] = jnp.zeros_like(y_ref)


def _experts(blk_exp, blk_valid, blk_first, xs, w1, w3, w2, l):
    grid_spec = pltpu.PrefetchScalarGridSpec(
        num_scalar_prefetch=3,
        grid=(MOE_NB,),
        in_specs=[
            pl.BlockSpec((MOE_MB, D), lambda i, be, bv, bf: (i, 0)),
            pl.BlockSpec((None, None, D, D_EXPERT), lambda i, be, bv, bf: (l, be[i], 0, 0)),
            pl.BlockSpec((None, None, D, D_EXPERT), lambda i, be, bv, bf: (l, be[i], 0, 0)),
            pl.BlockSpec((None, None, D_EXPERT, D), lambda i, be, bv, bf: (l, be[i], 0, 0)),
        ],
        out_specs=pl.BlockSpec((MOE_MB, D), lambda i, be, bv, bf: (i, 0)),
        scratch_shapes=[pltpu.VMEM((D, D_EXPERT), BF16), pltpu.VMEM((D, D_EXPERT), BF16),
                        pltpu.VMEM((D_EXPERT, D), BF16)],
    )
    return pl.pallas_call(
        _expert_kernel,
        grid_spec=grid_spec,
        out_shape=jax.ShapeDtypeStruct((MOE_SLOTS, D), F32),
        compiler_params=_cp(("arbitrary",)),
        name="moe_experts",
    )(blk_exp, blk_valid, blk_first, xs, w1, w3, w2)


COMB_TM = 256


def _combine_kernel(dest_ref, mf_ref, x_ref, gate_ref, y_hbm, o_ref, buf_ref, sem):
    def group(t8, start):
        base = pl.multiple_of(t8 * 8, 8)
        for s in range(8):
            for k in range(2):
                cp = _row_copy(y_hbm, dest_ref[k, base + s], buf_ref.at[k], base + s, sem)
                if start:
                    cp.start(priority=k)
                else:
                    cp.wait()

    lax.fori_loop(0, COMB_TM // 8, lambda t8, c: (group(t8, True), c)[1], 0)
    lax.fori_loop(0, COMB_TM // 8, lambda t8, c: (group(t8, False), c)[1], 0)
    mf = mf_ref[...]
    y = mf[:, 0:1] * buf_ref[0] + mf[:, 1:2] * buf_ref[1]
    o_ref[...] = x_ref[...] + gate_ref[...] * y


def _combine(dest_t, mf, x, mod, l, y):
    tm = COMB_TM
    return pl.pallas_call(
        _combine_kernel,
        grid=(NT // tm,),
        in_specs=[
            pl.BlockSpec((2, tm), lambda i: (0, i), memory_space=pltpu.SMEM),
            pl.BlockSpec((tm, 128), lambda i: (i, 0)),
            pl.BlockSpec((tm, D), lambda i: (i, 0)),
            _mod_spec(l, 5, tm),
            pl.BlockSpec(memory_space=pl.ANY),
        ],
        out_specs=pl.BlockSpec((tm, D), lambda i: (i, 0)),
        out_shape=jax.ShapeDtypeStruct((NT, D), F32),
        scratch_shapes=[pltpu.VMEM((2, tm, D), F32), pltpu.SemaphoreType.DMA(())],
        compiler_params=_cp(("arbitrary",)),
        name="moe_combine",
    )(dest_t, mf, x, mod, y)


def _comb_modmm_kernel(dcur_ref, dnxt_ref, mf_ref, x_ref, gate_ref, g_ref, sh_ref, sc_ref, w_ref, y_hbm,
                       xo_ref, p_ref, buf_ref, sem, *, nchunk):
    i = pl.program_id(0)
    n = pl.num_programs(0)
    slot = lax.rem(i, 2)
    tm = x_ref.shape[0]

    def rows(dest_ref, to_slot, lo, hi, start):
        for t in range(lo, hi):
            for k in range(2):
                cp = _row_copy(y_hbm, dest_ref[k, t], buf_ref.at[to_slot, k], t, sem.at[to_slot])
                if start:
                    cp.start(priority=k)
                else:
                    cp.wait()

    def rolled(dest_ref, to_slot, start):
        def body(t8, c):
            base = pl.multiple_of(t8 * 8, 8)
            for s in range(8):
                for k in range(2):
                    cp = _row_copy(y_hbm, dest_ref[k, base + s], buf_ref.at[to_slot, k], base + s,
                                   sem.at[to_slot])
                    if start:
                        cp.start(priority=k)
                    else:
                        cp.wait()
            return c
        lax.fori_loop(0, tm // 8, body, 0)

    @pl.when(i == 0)
    def _():
        rolled(dcur_ref, slot, True)

    rolled(dcur_ref, slot, False)
    mf = mf_ref[...]
    y = mf[:, 0:1] * buf_ref[slot, 0] + mf[:, 1:2] * buf_ref[slot, 1]
    xn = x_ref[...] + gate_ref[...] * y
    xo_ref[...] = xn
    h = _modulate(xn, g_ref[...], sh_ref[...], sc_ref[...]).astype(BF16)
    nout = p_ref.shape[1]
    n_ch = nout // nchunk
    per = -(-tm // n_ch)
    for c in range(n_ch):
        rows(dnxt_ref, 1 - slot, min(c * per, tm), min((c + 1) * per, tm), True)
        p_ref[:, c * nchunk:(c + 1) * nchunk] = _dot(h, w_ref[:, c * nchunk:(c + 1) * nchunk])

    @pl.when(i == n - 1)
    def _():
        rolled(dnxt_ref, 1 - slot, False)


def _comb_modmm(dest_t, mf, x, mod, l, y, g_next, w_bf16, tm=256):
    nout = w_bf16.shape[1]
    nt = NT // tm
    return pl.pallas_call(
        functools.partial(_comb_modmm_kernel, nchunk=256),
        grid=(nt,),
        in_specs=[
            pl.BlockSpec((2, tm), lambda i: (0, i), memory_space=pltpu.SMEM),
            pl.BlockSpec((2, tm), lambda i: (0, jnp.minimum(i + 1, nt - 1)), memory_space=pltpu.SMEM),
            pl.BlockSpec((tm, 128), lambda i: (i, 0)),
            pl.BlockSpec((tm, D), lambda i: (i, 0)),
            _mod_spec(l, 5, tm),
            pl.BlockSpec((1, D), lambda i: (0, 0)),
            _mod_spec(l + 1, 0, tm),
            _mod_spec(l + 1, 1, tm),
            pl.BlockSpec((D, nout), lambda i: (0, 0)),
            pl.BlockSpec(memory_space=pl.ANY),
        ],
        out_specs=[pl.BlockSpec((tm, D), lambda i: (i, 0)), pl.BlockSpec((tm, nout), lambda i: (i, 0))],
        out_shape=[jax.ShapeDtypeStruct((NT, D), F32), jax.ShapeDtypeStruct((NT, nout), F32)],
        scratch_shapes=[pltpu.VMEM((2, 2, tm, D), F32), pltpu.SemaphoreType.DMA((2,))],
        compiler_params=_cp(("arbitrary",)),
        name="moe_combine_modmm",
    )(dest_t, dest_t, mf, x, mod, g_next.reshape(1, D), mod, mod, w_bf16, y)


def _moe_plan(mi, counts):
    cnt = counts[0, N_GROUPS:N_GROUPS + N_EXPERTS].astype(jnp.int32)
    padded = (cnt + MOE_MB - 1) // MOE_MB * MOE_MB
    pad_ends = jnp.cumsum(padded)
    pad_starts = pad_ends - padded
    dest_t = pad_starts[mi[0:2]] + mi[2:4]
    blk0 = jnp.arange(MOE_NB, dtype=jnp.int32) * MOE_MB
    blk_exp = jnp.sum((pad_ends[None, :] <= blk0[:, None]).astype(jnp.int32), axis=1)
    blk_exp = jnp.minimum(blk_exp, N_EXPERTS - 1)
    blk_valid = jnp.clip(cnt[blk_exp] - (blk0 - pad_starts[blk_exp]), 0, MOE_MB)
    blk_valid = jnp.where(blk0 < pad_ends[-1], blk_valid, 0).astype(jnp.int32)
    blk_first = jnp.concatenate([jnp.ones((1,), jnp.int32),
                                 (blk_exp[1:] != blk_exp[:-1]).astype(jnp.int32)])
    return dest_t.astype(jnp.int32), blk_exp, blk_valid, blk_first


def _moe_experts(x, g, mod, l, mi, mf, counts, w1, w3, w2):
    dest_t, blk_exp, blk_valid, blk_first = _moe_plan(mi, counts)
    xs = _dispatch(dest_t, x, g, mod, l)
    y = _experts(blk_exp, blk_valid, blk_first, xs, w1, w3, w2, l)
    return dest_t, mf, y


def kernel(x_prompt, x_sample, cache_k_na, cache_v_na, cache_k_swa, cache_v_swa, state_hgrn, c, c_ctx, w_mod, b_mod, norm_mix_g, norm_ffn_g, w_in_even, w_out_even, na_rel_bias, swa_sink, w_in_odd, w_out_odd, conv_w, conv_b, conv_norm_g, conv_norm_b, hgrn_lb_raw, hgrn_norm_g, router_group_w, router_group_b, router_expert_w, router_expert_b, moe_w1, moe_w3, moe_w2, final_norm_g):
    x = jnp.concatenate([x_prompt.reshape(NT_CTX, D), x_sample.reshape(NT_LAT, D)], axis=0)
    cvec = jnp.zeros((16, D), F32).at[0].set(c_ctx).at[1:1 + DEC_BATCH].set(c)
    mod = _mod_table(cvec, w_mod, b_mod)

    lb_p = jax.nn.softmax(hgrn_lb_raw.astype(F32), axis=0)
    lower_bounds = jnp.cumsum(lb_p, axis=0) - lb_p[0:1]
    cos, sin = _rope_tables()
    cache_v_swa_ext = _swa_value_ext(cache_v_swa)
    router_w = _router_weights(router_group_w, router_group_b, router_expert_w, router_expert_b)

    k_na, v_na, k_swa, v_swa, s_hg = [], [], [], [], []
    moe = None
    for l in range(DEPTH):
        w_in = (w_in_even if l % 2 == 0 else w_in_odd)[l // 2].astype(BF16)
        if moe is None:
            p = _modmm(x, norm_mix_g[l], mod, l, w_in, tm=512)
        else:
            x, p = _comb_modmm(*moe[:2], x, mod, l - 1, moe[2], norm_mix_g[l], w_in)
        if l % 2 == 0:
            e = l // 2
            o = _ctx_attn(p, swa_sink[e])
            o = _na_attn(p, cache_k_na, cache_v_na, _na_bias_table(na_rel_bias[e]), e, o)
            o = _swa_attn(p, cache_k_swa, cache_v_swa_ext, swa_sink[e], cos, sin, e, o)
            pc = p[:NT_CTX]
            to_heads = lambda u, nh: u.reshape(BATCH, SEQ, nh, HEAD_DIM).transpose(0, 2, 1, 3)
            k_na.append(to_heads(pc[:, _KA:_KA + 512], NA_HEADS))
            v_na.append(to_heads(pc[:, _VA:_VA + 512], NA_HEADS))
            k_swa.append(to_heads(pc[:, _KB:_KB + 128], SWA_KV_HEADS))
            v_swa.append(to_heads(pc[:, _VB:_VB + 128], SWA_KV_HEADS))
            w_out = w_out_even[e]
        else:
            j = l // 2
            o = _conv_module(p, conv_w[j], conv_b[j], conv_norm_g[j], conv_norm_b[j])
            zero_state = jnp.zeros((BATCH, 2, HG_HEADS, 128, 128), F32)
            o, s_ctx = _hgrn(p, lower_bounds[j], hgrn_norm_g[j], zero_state, o, BATCH, SEQ, 0)
            s0_lat = jnp.swapaxes(state_hgrn[:, j], -1, -2)
            o, _ = _hgrn(p, lower_bounds[j], hgrn_norm_g[j], s0_lat, o, DEC_BATCH, DEC_SEQ, NT_CTX)
            s_hg.append(jnp.swapaxes(s_ctx, -1, -2))
            w_out = w_out_odd[j]
        x, mi, mf, counts = _mmres_router(o, x, mod, l, w_out.astype(BF16), norm_ffn_g[l], *router_w)
        moe = _moe_experts(x, norm_ffn_g[l], mod, l, mi, mf, counts, moe_w1, moe_w3, moe_w2)

    x = _combine(*moe[:2], x, mod, DEPTH - 1, moe[2])
    y_prompt = _final_norm(x, final_norm_g, 0, NT_CTX).reshape(BATCH, SEQ, D)
    y_sample = _final_norm(x, final_norm_g, NT_CTX, NT_LAT).reshape(DEC_BATCH, DEC_SEQ, D)
    return (y_prompt, y_sample, jnp.stack(k_na, axis=1), jnp.stack(v_na, axis=1),
            jnp.stack(k_swa, axis=1), jnp.stack(v_swa, axis=1), jnp.stack(s_hg, axis=1))
```

```python
import functools

import numpy as np
import jax
import jax.numpy as jnp
from jax import lax
from jax.experimental import pallas as pl
from jax.experimental.pallas import tpu as pltpu

F32 = jnp.float32
BF16 = jnp.bfloat16

D = 1024
BATCH = 16
SEQ = 256
DEPTH = 4
DEC_BATCH = 8
DEC_SEQ = 4096
PAST_LEN = 512
GRID_W = 64
HEAD_DIM = 64
EPS = 1e-6
NA_HEADS = 8
NA_WIN_R = 8
NA_WIN_C = 16
SWA_HEADS = 8
SWA_KV_HEADS = 2
SWA_WINDOW = 128
ROPE_BASE = 10000.0
CONV_CH = 512
CONV_WIDTH = 31
HG_HEADS = 4
HG_DK = 128
HG_CHUNK = 64
N_GROUPS = 4
EXPERTS_PER_GROUP = 8
N_EXPERTS = 32
D_EXPERT = 512
N_EVEN = 2
N_ODD = 2
D_IN_EVEN = 2304
D_IN_ODD = 3584

NT_CTX = BATCH * SEQ
NT_LAT = DEC_BATCH * DEC_SEQ
NT = NT_CTX + NT_LAT
SEG = 4096
assert NT_CTX == SEG and DEC_SEQ == SEG
N_MOD = 1 + DEC_BATCH
MASK = -1e30

MOE_MB = 512
MOE_NB = (2 * NT) // MOE_MB + N_EXPERTS
MOE_SLOTS = MOE_NB * MOE_MB

VMEM_LIMIT = 56 * 1024 * 1024


def _cp(sem, vmem=VMEM_LIMIT):
    return pltpu.CompilerParams(dimension_semantics=sem, vmem_limit_bytes=vmem)


def _dot(a, b):
    return jnp.dot(a, b, preferred_element_type=F32)


def _dot_nt(a, b):
    return lax.dot_general(a, b, (((1,), (1,)), ((), ())), preferred_element_type=F32)


def _dot_tn(a, b):
    return lax.dot_general(a, b, (((0,), (0,)), ((), ())), preferred_element_type=F32)


def _split2(a):
    hi = a.astype(BF16)
    lo = (a - hi.astype(F32)).astype(BF16)
    return hi, lo


def _dot3(a, b_hi, b_lo):
    a_hi, a_lo = _split2(a)
    return _dot(a_hi, b_hi) + (_dot(a_hi, b_lo) + _dot(a_lo, b_hi))


def _sigmoid(x):
    return 1.0 / (1.0 + jnp.exp(-x))


def _modulate(x, g, shift, scale):
    ms = jnp.mean(x * x, axis=-1, keepdims=True)
    return (x * lax.rsqrt(ms + EPS) * g) * (1.0 + scale) + shift


def _mod_kernel(c_ref, w_ref, b_ref, o_ref):
    cv = c_ref[...]
    s = cv * _sigmoid(cv)
    w_hi, w_lo = _split2(w_ref[...])
    o_ref[...] = _dot3(s, w_hi, w_lo) + b_ref[...]


def _mod_table(cvec, w_mod, b_mod):
    tn = 1536
    out = pl.pallas_call(
        _mod_kernel,
        grid=(DEPTH, 6 * D // tn),
        in_specs=[
            pl.BlockSpec((16, D), lambda l, j: (0, 0)),
            pl.BlockSpec((None, D, tn), lambda l, j: (l, 0, j)),
            pl.BlockSpec((None, 1, tn), lambda l, j: (l, 0, j)),
        ],
        out_specs=pl.BlockSpec((None, 16, tn), lambda l, j: (l, 0, j)),
        out_shape=jax.ShapeDtypeStruct((DEPTH, 16, 6 * D), F32),
        compiler_params=_cp(("arbitrary", "arbitrary")),
        name="mod_table",
    )(cvec, w_mod, b_mod.reshape(DEPTH, 1, 6 * D))
    return out.reshape(DEPTH, 16, 1, 6 * D)


def _mod_spec(l, which, tm):
    return pl.BlockSpec((None, None, 1, D), lambda i, *_: (l, (i * tm) // SEG, 0, which))


def _modmm_kernel(x_ref, g_ref, sh_ref, sc_ref, w_ref, o_ref, *, nchunk):
    h = _modulate(x_ref[...], g_ref[...], sh_ref[...], sc_ref[...]).astype(BF16)
    n = o_ref.shape[1]
    for n0 in range(0, n, nchunk):
        o_ref[:, n0:n0 + nchunk] = _dot(h, w_ref[:, n0:n0 + nchunk])


def _modmm(x, g, mod, l, w_bf16, tm):
    n = w_bf16.shape[1]
    return pl.pallas_call(
        functools.partial(_modmm_kernel, nchunk=256),
        grid=(NT // tm,),
        in_specs=[
            pl.BlockSpec((tm, D), lambda i: (i, 0)),
            pl.BlockSpec((1, D), lambda i: (0, 0)),
            _mod_spec(l, 0, tm),
            _mod_spec(l, 1, tm),
            pl.BlockSpec((D, n), lambda i: (0, 0)),
        ],
        out_specs=pl.BlockSpec((tm, n), lambda i: (i, 0)),
        out_shape=jax.ShapeDtypeStruct((NT, n), F32),
        compiler_params=_cp(("arbitrary",)),
        name="modmm",
    )(x, g.reshape(1, D), mod, mod, w_bf16)


def _final_kernel(x_ref, g_ref, o_ref):
    x = x_ref[...]
    ms = jnp.mean(x * x, axis=-1, keepdims=True)
    o_ref[...] = x * lax.rsqrt(ms + EPS) * g_ref[...]


def _final_norm(x, g, row0, n_rows, tm=512):
    r0 = row0 // tm
    return pl.pallas_call(
        _final_kernel,
        grid=(n_rows // tm,),
        in_specs=[pl.BlockSpec((tm, D), lambda i: (r0 + i, 0)), pl.BlockSpec((1, D), lambda i: (0, 0))],
        out_specs=pl.BlockSpec((tm, D), lambda i: (i, 0)),
        out_shape=jax.ShapeDtypeStruct((n_rows, D), F32),
        compiler_params=_cp(("arbitrary",)),
        name="final_norm",
    )(x, g.reshape(1, D))


_QA, _KA, _VA, _QB, _KB, _VB = 0, 512, 1024, 1536, 2048, 2176


def _ctx_attn_kernel(sink_ref, p_ref, o_ref):
    scale = HEAD_DIM ** -0.5

    def head(qc, kc, vc, sink):
        q = p_ref[:, qc:qc + 64].astype(BF16)
        k = p_ref[:, kc:kc + 64].astype(BF16)
        v = p_ref[:, vc:vc + 64].astype(BF16)
        s = _dot_nt(q, k) * scale
        m = jnp.max(s, axis=-1, keepdims=True)
        if sink is not None:
            m = jnp.maximum(m, sink)
        p = jnp.exp(s - m)
        den = jnp.sum(p, axis=-1, keepdims=True)
        if sink is not None:
            den = den + jnp.exp(sink - m)
        return _dot(p.astype(BF16), v) / den

    for h in range(NA_HEADS):
        o_ref[:, 64 * h:64 * h + 64] = head(_QA + 64 * h, _KA + 64 * h, _VA + 64 * h, None)
    for j in range(SWA_HEADS):
        kv = j // (SWA_HEADS // SWA_KV_HEADS)
        o_ref[:, 512 + 64 * j:512 + 64 * j + 64] = head(_QB + 64 * j, _KB + 64 * kv, _VB + 64 * kv, sink_ref[j])


def _ctx_attn(p, sink):
    return pl.pallas_call(
        _ctx_attn_kernel,
        grid=(BATCH,),
        in_specs=[
            pl.BlockSpec(memory_space=pltpu.SMEM),
            pl.BlockSpec((SEQ, D_IN_EVEN), lambda b: (b, 0)),
        ],
        out_specs=pl.BlockSpec((SEQ, D), lambda b: (b, 0)),
        out_shape=jax.ShapeDtypeStruct((NT, D), F32),
        compiler_params=_cp(("arbitrary",)),
        name="ctx_attn",
    )(sink, p)


NA_QROWS = 8
NA_BAND = 16
NA_TQ = NA_QROWS * GRID_W
NA_TK = NA_BAND * GRID_W
NA_RB = (DEC_SEQ // GRID_W) // NA_QROWS


def _na_band_start(rb):
    rows = DEC_SEQ // GRID_W
    return int(np.clip(NA_QROWS * rb - NA_WIN_R // 2, 0, rows - NA_BAND))


def _na_row_index():
    rows = DEC_SEQ // GRID_W
    n_dr = 2 * NA_WIN_R - 1
    out = np.full((3, NA_QROWS, NA_BAND), n_dr, np.int32)
    for ci, rb in enumerate((0, 1, NA_RB - 1)):
        for ql in range(NA_QROWS):
            qrow = NA_QROWS * rb + ql
            r0 = int(np.clip(qrow - NA_WIN_R // 2, 0, rows - NA_WIN_R))
            for kl in range(NA_BAND):
                krow = _na_band_start(rb) + kl
                if r0 <= krow < r0 + NA_WIN_R:
                    out[ci, ql, kl] = krow - qrow + (NA_WIN_R - 1)
    return out


def _na_bias_table(rel_bias):
    h = rel_bias.shape[0]
    n_dr, n_dc = 2 * NA_WIN_R - 1, 2 * NA_WIN_C - 1
    cols = np.arange(GRID_W)
    c0 = np.clip(cols - NA_WIN_C // 2, 0, GRID_W - NA_WIN_C)
    vc = (cols[None, :] >= c0[:, None]) & (cols[None, :] < c0[:, None] + NA_WIN_C)
    dc = cols[None, :] - cols[:, None] + (NA_WIN_C - 1)
    sel = ((np.arange(n_dc)[:, None, None] == dc[None]) & vc[None]).astype(np.float32)
    bc = jnp.dot(rel_bias.reshape(h * n_dr, n_dc), jnp.asarray(sel.reshape(n_dc, -1)),
                 precision=lax.Precision.HIGHEST).reshape(h, n_dr, GRID_W, GRID_W)
    bc = jnp.where(jnp.asarray(vc), bc, MASK)
    bc = jnp.concatenate([bc, jnp.full((h, 1, GRID_W, GRID_W), MASK, F32)], axis=1)
    tab = jnp.take(bc, jnp.asarray(_na_row_index()), axis=1)
    return tab.transpose(0, 1, 2, 4, 3, 5).reshape(h // 2, 2, 3, NA_TQ, NA_TK)


def _na_kernel(q_ref, k_ref, v_ref, kc_ref, vc_ref, bias_ref, oin_ref, o_ref):
    del oin_ref
    rb = pl.program_id(2)
    scale = HEAD_DIM ** -0.5
    start = jnp.clip(NA_QROWS * rb - NA_WIN_R // 2, 0, DEC_SEQ // GRID_W - NA_BAND) * GRID_W
    start = pl.multiple_of(start, 256)
    case = jnp.where(rb == 0, 0, jnp.where(rb == NA_RB - 1, 2, 1))
    kb = k_ref[pl.ds(start, NA_TK), :].astype(BF16)
    vb = v_ref[pl.ds(start, NA_TK), :].astype(BF16)
    q = q_ref[...] * scale
    lane = lax.broadcasted_iota(jnp.int32, (1, 128), 1)
    hs = range(2)
    s_loc = [_dot_nt(jnp.where((lane >> 6) == hh, q, 0.0).astype(BF16), kb) + bias_ref[hh, case] for hh in hs]
    s_ctx = [_dot_nt(q[:, 64 * hh:64 * hh + 64].astype(BF16), kc_ref[hh].astype(BF16)) for hh in hs]
    m = [jnp.maximum(jnp.max(s_loc[hh], axis=-1, keepdims=True), jnp.max(s_ctx[hh], axis=-1, keepdims=True))
         for hh in hs]
    p_loc = [jnp.exp(s_loc[hh] - m[hh]) for hh in hs]
    p_ctx = [jnp.exp(s_ctx[hh] - m[hh]) for hh in hs]
    inv = [1.0 / (jnp.sum(p_loc[hh], axis=-1, keepdims=True) + jnp.sum(p_ctx[hh], axis=-1, keepdims=True))
           for hh in hs]
    o_loc = [_dot(p_loc[hh].astype(BF16), vb) * inv[hh] for hh in hs]
    o_ctx = [_dot(p_ctx[hh].astype(BF16), vc_ref[hh].astype(BF16)) * inv[hh] for hh in hs]
    o_ref[...] = jnp.where((lane >> 6) == 0, o_loc[0], o_loc[1]) + jnp.concatenate(o_ctx, axis=-1)


def _na_attn(p, cache_k, cache_v, bias_tab, e, o_full):
    n_hp = NA_HEADS // 2
    lat0 = NT_CTX // NA_TQ
    per_b = DEC_SEQ // NA_TQ
    return pl.pallas_call(
        _na_kernel,
        grid=(n_hp, DEC_BATCH, NA_RB),
        in_specs=[
            pl.BlockSpec((NA_TQ, 128), lambda hp, b, rb: (lat0 + b * per_b + rb, _QA // 128 + hp)),
            pl.BlockSpec((DEC_SEQ, 128), lambda hp, b, rb: (1 + b, _KA // 128 + hp)),
            pl.BlockSpec((DEC_SEQ, 128), lambda hp, b, rb: (1 + b, _VA // 128 + hp)),
            pl.BlockSpec((None, None, 2, PAST_LEN, HEAD_DIM), lambda hp, b, rb: (b, e, hp, 0, 0)),
            pl.BlockSpec((None, None, 2, PAST_LEN, HEAD_DIM), lambda hp, b, rb: (b, e, hp, 0, 0)),
            pl.BlockSpec((None, 2, 3, NA_TQ, NA_TK), lambda hp, b, rb: (hp, 0, 0, 0, 0)),
            pl.BlockSpec(memory_space=pl.ANY),
        ],
        out_specs=pl.BlockSpec((NA_TQ, 128), lambda hp, b, rb: (lat0 + b * per_b + rb, hp)),
        out_shape=jax.ShapeDtypeStruct((NT, D), F32),
        input_output_aliases={6: 0},
        compiler_params=_cp(("arbitrary", "arbitrary", "arbitrary")),
        name="na_attn",
    )(p, p, p, cache_k, cache_v, bias_tab, o_full)


SWA_TQ = 256
SWA_TK = SWA_TQ + 2 * SWA_WINDOW


def _rope_tables():
    pos = np.arange(DEC_SEQ)
    row = (pos // GRID_W).astype(np.float32)
    col = (pos % GRID_W).astype(np.float32)
    half = HEAD_DIM // 2
    inv = jnp.asarray(ROPE_BASE, F32) ** (-jnp.arange(0, half, 2, dtype=F32) / half)
    ar = jnp.asarray(row)[:, None] * inv[None, :]
    ac = jnp.asarray(col)[:, None] * inv[None, :]
    cos = jnp.concatenate([jnp.cos(ar), jnp.cos(ar), jnp.cos(ac), jnp.cos(ac)], axis=1)
    sin = jnp.concatenate([-jnp.sin(ar), jnp.sin(ar), -jnp.sin(ac), jnp.sin(ac)], axis=1)
    return jnp.tile(cos, (1, 2)), jnp.tile(sin, (1, 2))


def _rope128(t, cos, sin):
    lane = lax.broadcasted_iota(jnp.int32, (1, 128), 1)
    up = pltpu.roll(t, 112, 1)
    dn = pltpu.roll(t, 16, 1)
    sw = jnp.where((lane & 31) < 16, up, dn)
    return t * cos + sw * sin


def _swa_mask_table():
    out = []
    for qb in (0, 1, DEC_SEQ // SWA_TQ - 1):
        q0 = qb * SWA_TQ
        bs = int(np.clip(q0 - SWA_WINDOW, 0, DEC_SEQ - SWA_TK))
        d = (q0 + np.arange(SWA_TQ))[:, None] - (bs + np.arange(SWA_TK))[None, :]
        out.append(np.where(np.abs(d) <= SWA_WINDOW, 0.0, MASK))
    return np.stack(out).astype(np.float32)


def _swa_kernel(sink_ref, q_ref, k_ref, v_ref, kc_ref, vc_ref, cos_ref, sin_ref, mask_ref, oin_ref, o_ref,
                kr_ref):
    del oin_ref
    qb = pl.program_id(1)
    scale = HEAD_DIM ** -0.5
    g = SWA_HEADS // SWA_KV_HEADS

    @pl.when(qb == 0)
    def _():
        kr_ref[...] = _rope128(k_ref[...], cos_ref[...], sin_ref[...]).astype(BF16)

    q0 = pl.multiple_of(qb * SWA_TQ, SWA_TQ)
    bs = pl.multiple_of(jnp.clip(q0 - SWA_WINDOW, 0, DEC_SEQ - SWA_TK), 128)
    cq = cos_ref[pl.ds(q0, SWA_TQ), :]
    sq = sin_ref[pl.ds(q0, SWA_TQ), :]
    kb = kr_ref[pl.ds(bs, SWA_TK), :]
    vb = v_ref[pl.ds(bs, SWA_TK), :].astype(BF16)
    qr = [_rope128(q_ref[:, 128 * j:128 * j + 128], cq, sq) * scale for j in range(SWA_HEADS // 2)]
    case = jnp.where(qb == 0, 0, jnp.where(qb == pl.num_programs(1) - 1, 2, 1))
    wmask = mask_ref[case]
    rowc = lax.broadcasted_iota(jnp.int32, (g * SWA_TQ, 1), 0) >> 8
    lane_half = lax.broadcasted_iota(jnp.int32, (1, 128), 1) >> 6

    kvs = range(SWA_KV_HEADS)
    q4, sink = [], []
    for kvh in kvs:
        heads = []
        sk = jnp.zeros((g * SWA_TQ, 1), F32)
        for gi in range(g):
            hq = kvh * g + gi
            heads.append(qr[hq // 2][:, 64 * (hq % 2):64 * (hq % 2) + 64])
            sk = jnp.where(rowc == gi, sink_ref[hq], sk)
        q4.append(jnp.concatenate(heads, axis=0).astype(BF16))
        sink.append(sk)
    s_loc = [(_dot_nt(q4[kvh], kb[:, 64 * kvh:64 * kvh + 64]).reshape(g, SWA_TQ, SWA_TK)
              + wmask[None]).reshape(g * SWA_TQ, SWA_TK) for kvh in kvs]
    s_ctx = [_dot_nt(q4[kvh], kc_ref[kvh].astype(BF16)) for kvh in kvs]
    m = [jnp.maximum(jnp.max(jnp.maximum(s_loc[kvh], s_ctx[kvh]), axis=-1, keepdims=True), sink[kvh])
         for kvh in kvs]
    p_loc = [jnp.exp(s_loc[kvh] - m[kvh]).astype(BF16) for kvh in kvs]
    p_ctx = [jnp.exp(s_ctx[kvh] - m[kvh]).astype(BF16) for kvh in kvs]
    pv = [_dot(p_loc[kvh], jnp.where(lane_half == kvh, vb, jnp.ones_like(vb)))
          + _dot(p_ctx[kvh], vc_ref[kvh].astype(BF16)) for kvh in kvs]
    for kvh in kvs:
        lo, hi = 64 * kvh, 64 * (1 - kvh)
        den = pv[kvh][:, hi:hi + 1] + jnp.exp(sink[kvh] - m[kvh])
        o = pv[kvh][:, lo:lo + 64] / den
        for gi in range(g):
            hq = kvh * g + gi
            o_ref[:, 64 * hq:64 * hq + 64] = o[SWA_TQ * gi:SWA_TQ * (gi + 1), :]


def _swa_value_ext(cache_v):
    ones = jnp.ones_like(cache_v[..., 0, :, :])
    return jnp.stack([jnp.concatenate([cache_v[..., 0, :, :], ones], axis=-1),
                      jnp.concatenate([ones, cache_v[..., 1, :, :]], axis=-1)], axis=-3)


def _swa_attn(p, cache_k, cache_v_ext, sink, cos, sin, e, o_full):
    assert SWA_TQ == 256
    lat0 = NT_CTX // SWA_TQ
    per_b = DEC_SEQ // SWA_TQ
    return pl.pallas_call(
        _swa_kernel,
        grid=(DEC_BATCH, per_b),
        in_specs=[
            pl.BlockSpec(memory_space=pltpu.SMEM),
            pl.BlockSpec((SWA_TQ, 512), lambda b, qb: (lat0 + b * per_b + qb, _QB // 512)),
            pl.BlockSpec((DEC_SEQ, 128), lambda b, qb: (1 + b, _KB // 128)),
            pl.BlockSpec((DEC_SEQ, 128), lambda b, qb: (1 + b, _VB // 128)),
            pl.BlockSpec((None, None, 2, PAST_LEN, HEAD_DIM), lambda b, qb: (b, e, 0, 0, 0)),
            pl.BlockSpec((None, None, 2, PAST_LEN, 128), lambda b, qb: (b, e, 0, 0, 0)),
            pl.BlockSpec((DEC_SEQ, 128), lambda b, qb: (0, 0)),
            pl.BlockSpec((DEC_SEQ, 128), lambda b, qb: (0, 0)),
            pl.BlockSpec((3, SWA_TQ, SWA_TK), lambda b, qb: (0, 0, 0)),
            pl.BlockSpec(memory_space=pl.ANY),
        ],
        out_specs=pl.BlockSpec((SWA_TQ, 512), lambda b, qb: (lat0 + b * per_b + qb, 1)),
        out_shape=jax.ShapeDtypeStruct((NT, D), F32),
        scratch_shapes=[pltpu.VMEM((DEC_SEQ, 128), BF16)],
        input_output_aliases={9: 0},
        compiler_params=_cp(("arbitrary", "arbitrary")),
        name="swa_attn",
    )(sink, p, p, p, cache_k, cache_v_ext, cos, sin, jnp.asarray(_swa_mask_table()), o_full)


CONV_TT = 256
CONV_HALO = 16
CONV_RC = 32


def _conv_kernel(u_ref, ul_ref, ur_ref, w_ref, cb_ref, lg_ref, lb_ref, o_ref, xs_ref, wb_ref):
    i = pl.program_id(0)

    @pl.when(i == 0)
    def _():
        for t in range(CONV_WIDTH):
            wb_ref[t] = jnp.broadcast_to(w_ref[t:t + 1, :], (8, CONV_CH))

    n_ctx_tiles = NT_CTX // CONV_TT
    per_seq = DEC_SEQ // CONV_TT
    j = i - n_ctx_tiles
    is_ctx = i < n_ctx_tiles
    is_start = is_ctx | ((j % per_seq) == 0)
    is_end = is_ctx | ((j % per_seq) == per_seq - 1)

    def glu(u):
        return u[:, :CONV_CH] * _sigmoid(u[:, CONV_CH:])

    xs_ref[0, CONV_HALO:CONV_HALO + CONV_TT, :] = glu(u_ref[...])
    xs_ref[0, 0:CONV_HALO, :] = jnp.where(is_start, 0.0, glu(ul_ref[...]))
    xs_ref[0, CONV_HALO + CONV_TT:, :] = jnp.where(is_end, 0.0, glu(ur_ref[...]))
    n_rows = CONV_TT + 2 * CONV_HALO
    xs0 = xs_ref[0]
    for b in range(1, 8):
        xs_ref[b] = pltpu.roll(xs0, n_rows - b, 0)

    pad = CONV_WIDTH // 2
    for c in range(CONV_TT // CONV_RC):
        base = CONV_HALO + c * CONV_RC - pad
        acc = jnp.zeros((CONV_RC, CONV_CH), F32)
        for t in range(CONV_WIDTH):
            off = base + t
            acc = acc + (xs_ref[off % 8, off - off % 8:off - off % 8 + CONV_RC, :]
                         * jnp.tile(wb_ref[t], (CONV_RC // 8, 1)))
        y = acc + cb_ref[...]
        mu = jnp.mean(y, axis=-1, keepdims=True)
        yc = y - mu
        var = jnp.mean(yc * yc, axis=-1, keepdims=True)
        yn = yc * lax.rsqrt(var + EPS) * lg_ref[...] + lb_ref[...]
        o_ref[c * CONV_RC:(c + 1) * CONV_RC, :] = yn * _sigmoid(yn)


def _conv_module(p, conv_w, conv_b, ln_g, ln_b):
    nh = CONV_TT // CONV_HALO
    last = NT // CONV_HALO - 1
    return pl.pallas_call(
        _conv_kernel,
        grid=(NT // CONV_TT,),
        in_specs=[
            pl.BlockSpec((CONV_TT, 2 * CONV_CH), lambda i: (i, 0)),
            pl.BlockSpec((CONV_HALO, 2 * CONV_CH), lambda i: (jnp.maximum(i * nh - 1, 0), 0)),
            pl.BlockSpec((CONV_HALO, 2 * CONV_CH), lambda i: (jnp.minimum((i + 1) * nh, last), 0)),
            pl.BlockSpec((CONV_WIDTH, CONV_CH), lambda i: (0, 0)),
            pl.BlockSpec((1, CONV_CH), lambda i: (0, 0)),
            pl.BlockSpec((1, CONV_CH), lambda i: (0, 0)),
            pl.BlockSpec((1, CONV_CH), lambda i: (0, 0)),
        ],
        out_specs=pl.BlockSpec((CONV_TT, CONV_CH), lambda i: (i, 0)),
        out_shape=jax.ShapeDtypeStruct((NT, D), F32),
        scratch_shapes=[pltpu.VMEM((8, CONV_TT + 2 * CONV_HALO, CONV_CH), F32),
                        pltpu.VMEM((CONV_WIDTH, 8, CONV_CH), F32)],
        compiler_params=_cp(("arbitrary",)),
        name="conv_module",
    )(p, p, p, conv_w, conv_b.reshape(1, -1), ln_g.reshape(1, -1), ln_b.reshape(1, -1))


_U, _HQ, _HI, _HFF, _HFB, _HGT = 0, 1024, 1536, 2048, 2560, 3072
HG_LEVELS = 6
HG_W = HG_HEADS * HG_DK


def _hgrn_constants():
    c = HG_CHUNK
    mm = np.zeros((HG_LEVELS + 1, c, c), np.float32)
    for lv in range(HG_LEVELS):
        n = c >> lv
        half = n // 2
        for t in range(c):
            for s in range(c):
                if t // n == s // n and t % n >= half and s % n < half:
                    mm[lv, t, s] = 1.0
    mm[HG_LEVELS] = np.eye(c, dtype=np.float32)
    tri = np.tril(np.ones((c, c), np.float32))
    return np.stack([tri, tri.T]), np.stack([mm, mm[:, ::-1, ::-1]])


def _hgrn_level_decays(cum, forget, cum_ref, fwd):
    c = HG_CHUNK
    cum_ref[...] = cum
    out = []
    for n in (64, 32, 16):
        pieces = []
        for b in range(c // n):
            r = b * n + n // 2 - (1 if fwd else 0)
            ref = jnp.broadcast_to(cum_ref[r:r + 1, :], (n // 2, HG_W))
            lo = cum[b * n:b * n + n // 2]
            hi = cum[b * n + n // 2:(b + 1) * n]
            pieces += [ref - lo, hi - ref] if fwd else [lo - ref, ref - hi]
        out.append(jnp.exp(jnp.concatenate(pieces, axis=0)))
    rows = [b * 8 + 4 - (1 if fwd else 0) for b in range(c // 8)]
    ref8 = jnp.concatenate([jnp.broadcast_to(cum_ref[r:r + 1, :], (8, HG_W)) for r in rows], axis=0)
    out.append(jnp.exp(-jnp.abs(cum - ref8)))
    row = lax.broadcasted_iota(jnp.int32, (c, HG_W), 0)
    p = 2 - (1 if fwd else 0)
    ref4 = None
    for j in range(4):
        shift = (j - p) % c
        cand = cum if shift == 0 else pltpu.roll(cum, shift, 0)
        ref4 = cand if ref4 is None else jnp.where((row & 3) == j, cand, ref4)
    out.append(jnp.exp(-jnp.abs(cum - ref4)))
    out.append(jnp.where((row & 1) == (1 if fwd else 0), forget, 1.0))
    return out


def _hgrn_gates(q, f, lb, tri):
    qs = q * _sigmoid(q)
    forget = lb + (1.0 - lb) * _sigmoid(f)
    g = jnp.log(forget)
    k = (1.0 - lb) * _sigmoid(-f)
    g_hi = g.astype(BF16)
    r1 = g - g_hi.astype(F32)
    g_mid = r1.astype(BF16)
    g_lo = (r1 - g_mid.astype(F32)).astype(BF16)
    cum = _dot(tri, g_hi) + (_dot(tri, g_mid) + _dot(tri, g_lo))
    return qs, k, forget, cum


def _hgrn_operands(qs, k, forget, v, cum, cum_ref, fwd):
    c = HG_CHUNK
    decays = _hgrn_level_decays(cum, forget, cum_ref, fwd)
    end_row = c - 1 if fwd else 0
    ecum = jnp.exp(cum)
    eend = jnp.exp(-jnp.abs(cum - jnp.broadcast_to(cum_ref[end_row:end_row + 1, :], (c, HG_W))))
    qb, kb = qs.astype(BF16), k.astype(BF16)
    qk = [(qb, kb)]
    for el in decays:
        elb = el.astype(BF16)
        qk.append((qb * elb, kb * elb))
    return dict(qk=qk, vb=v.astype(BF16), qc=(qs * ecum).astype(BF16), kend=(k * eend).astype(BF16),
                dec=ecum[end_row:end_row + 1, :])


def _hgrn_scores(ops, masks):
    out = []
    for h in range(HG_HEADS):
        sl = slice(h * HG_DK, (h + 1) * HG_DK)
        prods = [_dot_nt(qb[:, sl], kb[:, sl]) for qb, kb in ops["qk"]]
        a = masks[HG_LEVELS] * prods[0]
        for lv in range(HG_LEVELS):
            a = a + masks[lv] * prods[lv + 1]
        out.append(a.astype(BF16))
    return out


def _hgrn_outputs(ops, scores, st_ref):
    outs = []
    for h in range(HG_HEADS):
        sl = slice(h * HG_DK, (h + 1) * HG_DK)
        s_t = st_ref[h]
        outs.append(_dot(scores[h], ops["vb"][:, sl]) + _dot_nt(ops["qc"][:, sl], s_t.astype(BF16)))
        st_ref[h] = s_t * ops["dec"][:, sl] + _dot_tn(ops["vb"][:, sl], ops["kend"][:, sl])
    return jnp.concatenate(outs, axis=1)


def _hgrn_kernel(qf_ref, if_ref, ff_ref, qb_ref, ib_ref, fb_ref, lb_ref, s0_ref, tri_ref, m_ref,
                 of_ref, ob_ref, sfin_ref, stf_ref, stb_ref, cumf_ref, cumb_ref, *, tile):
    c = HG_CHUNK
    nct = tile // c
    tt = pl.program_id(1)

    @pl.when(tt == 0)
    def _():
        stf_ref[...] = s0_ref[0]
        stb_ref[...] = s0_ref[1]

    lbf = lb_ref[0:1, :]
    lbb = lb_ref[1:2, :]

    masks_f = [m_ref[0, x] for x in range(HG_LEVELS + 1)]
    masks_b = [m_ref[1, x] for x in range(HG_LEVELS + 1)]

    def body(pi, carry):
        rows = []
        for u in range(2):
            ci = 2 * pi + u
            rows.append((pl.multiple_of(ci * c, c), pl.multiple_of((nct - 1 - ci) * c, c)))
        gates = [(_hgrn_gates(qf_ref[pl.ds(rf, c), :], ff_ref[pl.ds(rf, c), :], lbf, tri_ref[0]),
                  _hgrn_gates(qb_ref[pl.ds(rb, c), :], fb_ref[pl.ds(rb, c), :], lbb, tri_ref[1]))
                 for rf, rb in rows]
        ops = [(_hgrn_operands(*gates[u][0][:3], if_ref[pl.ds(rf, c), :], gates[u][0][3], cumf_ref.at[u], True),
                _hgrn_operands(*gates[u][1][:3], ib_ref[pl.ds(rb, c), :], gates[u][1][3], cumb_ref.at[u], False))
               for u, (rf, rb) in enumerate(rows)]
        scores = [(_hgrn_scores(ops[u][0], masks_f), _hgrn_scores(ops[u][1], masks_b)) for u in range(2)]
        for u, (rf, rb) in enumerate(rows):
            o_f = _hgrn_outputs(ops[u][0], scores[u][0], stf_ref)
            o_b = _hgrn_outputs(ops[u][1], scores[u][1], stb_ref)
            of_ref[pl.ds(rf, c), :] = o_f
            ob_ref[pl.ds(rb, c), :] = o_b
        return carry

    lax.fori_loop(0, nct // 2, body, 0)

    @pl.when(tt == pl.num_programs(1) - 1)
    def _():
        sfin_ref[0] = stf_ref[...]
        sfin_ref[1] = stb_ref[...]


def _hgrn_scan(p, lb, s0_t, n_batch, t_len, row0):
    tri, mm = _hgrn_constants()
    tri = jnp.asarray(tri, BF16)
    mm = jnp.asarray(mm, F32)
    tile = min(t_len, 512)
    ntt = t_len // tile
    r0 = row0 // tile

    def fwd(c0):
        return pl.BlockSpec((tile, HG_W), lambda b, t: (r0 + b * ntt + t, c0 // HG_W))

    def bwd(c0):
        return pl.BlockSpec((tile, HG_W), lambda b, t: (r0 + b * ntt + ntt - 1 - t, c0 // HG_W))

    st_spec = pl.BlockSpec((None, 2, HG_HEADS, 128, 128), lambda b, t: (b, 0, 0, 0, 0))
    n_rows = n_batch * t_len
    return pl.pallas_call(
        functools.partial(_hgrn_kernel, tile=tile),
        grid=(n_batch, ntt),
        in_specs=[
            fwd(_HQ), fwd(_HI), fwd(_HFF), bwd(_HQ), bwd(_HI), bwd(_HFB),
            pl.BlockSpec((2, HG_W), lambda b, t: (0, 0)),
            st_spec,
            pl.BlockSpec((2, HG_CHUNK, HG_CHUNK), lambda b, t: (0, 0, 0)),
            pl.BlockSpec((2, HG_LEVELS + 1, HG_CHUNK, HG_CHUNK), lambda b, t: (0, 0, 0, 0)),
        ],
        out_specs=[
            pl.BlockSpec((tile, HG_W), lambda b, t: (b * ntt + t, 0)),
            pl.BlockSpec((tile, HG_W), lambda b, t: (b * ntt + ntt - 1 - t, 0)),
            st_spec,
        ],
        out_shape=[jax.ShapeDtypeStruct((n_rows, HG_W), F32),
                   jax.ShapeDtypeStruct((n_rows, HG_W), F32),
                   jax.ShapeDtypeStruct((n_batch, 2, HG_HEADS, 128, 128), F32)],
        scratch_shapes=[pltpu.VMEM((HG_HEADS, 128, 128), F32), pltpu.VMEM((HG_HEADS, 128, 128), F32),
                        pltpu.VMEM((2, HG_CHUNK, HG_W), F32), pltpu.VMEM((2, HG_CHUNK, HG_W), F32)],
        compiler_params=_cp(("arbitrary", "arbitrary")),
        name="hgrn_scan",
    )(p, p, p, p, p, p, lb, s0_t, tri, mm)


def _hgrn_fin_kernel(of_ref, ob_ref, gt_ref, ng_ref, oin_ref, o_ref):
    del oin_ref
    od = of_ref[...] + ob_ref[...]
    gt = gt_ref[...]
    gate = gt * _sigmoid(gt)
    for h in range(HG_HEADS):
        sl = slice(h * HG_DK, (h + 1) * HG_DK)
        x = od[:, sl]
        ms = jnp.mean(x * x, axis=-1, keepdims=True)
        o_ref[:, sl] = x * lax.rsqrt(ms + EPS) * ng_ref[...] * gate[:, sl]


def _hgrn_finish(o_f, o_b, p, norm_g, o_full, row0, tm=512):
    n_rows = o_f.shape[0]
    r0 = row0 // tm
    return pl.pallas_call(
        _hgrn_fin_kernel,
        grid=(n_rows // tm,),
        in_specs=[
            pl.BlockSpec((tm, HG_W), lambda i: (i, 0)),
            pl.BlockSpec((tm, HG_W), lambda i: (i, 0)),
            pl.BlockSpec((tm, HG_W), lambda i: (r0 + i, _HGT // HG_W)),
            pl.BlockSpec((1, 128), lambda i: (0, 0)),
            pl.BlockSpec(memory_space=pl.ANY),
        ],
        out_specs=pl.BlockSpec((tm, HG_W), lambda i: (r0 + i, 1)),
        out_shape=jax.ShapeDtypeStruct((NT, D), F32),
        input_output_aliases={4: 0},
        compiler_params=_cp(("arbitrary",)),
        name="hgrn_finish",
    )(o_f, o_b, p, norm_g.reshape(1, 128), o_full)


def _hgrn(p, lb, norm_g, s0_t, o_full, n_batch, t_len, row0):
    o_f, o_b, s_fin = _hgrn_scan(p, lb, s0_t, n_batch, t_len, row0)
    return _hgrn_finish(o_f, o_b, p, norm_g, o_full, row0), s_fin


ROUTE_TM = 512


def _mmres_router_kernel(o_ref, x_ref, gate_ref, w_ref, g_ref, sh_ref, sc_ref, whi_ref, br_ref,
                         tri_ref, xo_ref, mi_ref, mf_ref, cnt_ref, carry_ref):
    i = pl.program_id(0)

    @pl.when(i == 0)
    def _():
        carry_ref[...] = jnp.zeros_like(carry_ref)

    xn = x_ref[...] + gate_ref[...] * _dot(o_ref[...].astype(BF16), w_ref[...])
    xo_ref[...] = xn
    h = _modulate(xn, g_ref[...], sh_ref[...], sc_ref[...])
    h_hi, h_lo = _split2(h)
    hw = _dot(h_hi, whi_ref[...])
    logits = hw[:, :128] + (hw[:, 128:] + _dot(h_lo, whi_ref[:, :128])) + br_ref[...]
    lane = lax.broadcasted_iota(jnp.int32, logits.shape, 1)
    neg = jnp.float32(-3e38)
    is_g = lane < N_GROUPS
    lg = jnp.where(is_g, logits, neg)
    mg = jnp.max(lg, axis=-1, keepdims=True)
    grp = jnp.min(jnp.where(lg == mg, lane, 128), axis=-1, keepdims=True)
    pg = 1.0 / jnp.sum(jnp.where(is_g, jnp.exp(lg - mg), 0.0), axis=-1, keepdims=True)
    ex = lane - N_GROUPS
    in_grp = (ex >= 0) & (ex < N_EXPERTS) & ((ex >> 3) == grp)
    le = jnp.where(in_grp, logits, neg)
    v1 = jnp.max(le, axis=-1, keepdims=True)
    i1 = jnp.min(jnp.where(le == v1, lane, 128), axis=-1, keepdims=True)
    le2 = jnp.where(lane == i1, neg, le)
    v2 = jnp.max(le2, axis=-1, keepdims=True)
    i2 = jnp.min(jnp.where(le2 == v2, lane, 128), axis=-1, keepdims=True)
    t = jnp.exp(v2 - v1)
    g1 = pg / (1.0 + t)
    g2 = pg * t / (1.0 + t)
    oh = jnp.where((lane == i1) | (lane == i2), 1.0, 0.0)
    prefix = _dot(tri_ref[...], oh.astype(BF16)) + carry_ref[...]
    r1 = jnp.sum(jnp.where(lane == i1, prefix, 0.0), axis=-1, keepdims=True).astype(jnp.int32)
    r2 = jnp.sum(jnp.where(lane == i2, prefix, 0.0), axis=-1, keepdims=True).astype(jnp.int32)
    carry_ref[...] = carry_ref[...] + jnp.sum(oh, axis=0, keepdims=True)
    cnt_ref[...] = carry_ref[...]
    mi = jnp.where(lane == 0, i1 - N_GROUPS, jnp.where(lane == 1, i2 - N_GROUPS,
                   jnp.where(lane == 2, r1, jnp.where(lane == 3, r2, 0))))
    mi_ref[...] = jnp.transpose(mi)[0:8, :]
    mf_ref[...] = jnp.where(lane == 0, g1, jnp.where(lane == 1, g2, 0.0))


def _router_weights(wg, bg, we, be):
    pad = 128 - N_GROUPS - N_EXPERTS
    wr = jnp.pad(jnp.concatenate([wg, we], axis=-1), ((0, 0), (0, 0), (0, pad)))
    br = jnp.pad(jnp.concatenate([bg, be], axis=-1), ((0, 0), (0, pad)))[:, None, :]
    w_hi = wr.astype(BF16)
    w_lo = (wr - w_hi.astype(F32)).astype(BF16)
    return jnp.concatenate([w_hi, w_lo], axis=-1), br


def _mmres_router(o, x, mod, l, w_out_bf16, g, w_hl, br):
    tm = ROUTE_TM
    tri = jnp.asarray(np.tril(np.ones((tm, tm), np.float32), -1), BF16)
    const = lambda i: (0, 0)
    return pl.pallas_call(
        _mmres_router_kernel,
        grid=(NT // tm,),
        in_specs=[
            pl.BlockSpec((tm, D), lambda i: (i, 0)),
            pl.BlockSpec((tm, D), lambda i: (i, 0)),
            _mod_spec(l, 2, tm),
            pl.BlockSpec((D, D), const),
            pl.BlockSpec((1, D), const),
            _mod_spec(l, 3, tm),
            _mod_spec(l, 4, tm),
            pl.BlockSpec((None, D, 256), lambda i: (l, 0, 0)),
            pl.BlockSpec((None, 1, 128), lambda i: (l, 0, 0)),
            pl.BlockSpec((tm, tm), const),
        ],
        out_specs=[
            pl.BlockSpec((tm, D), lambda i: (i, 0)),
            pl.BlockSpec((8, tm), lambda i: (0, i)),
            pl.BlockSpec((tm, 128), lambda i: (i, 0)),
            pl.BlockSpec((1, 128), const),
        ],
        out_shape=[
            jax.ShapeDtypeStruct((NT, D), F32),
            jax.ShapeDtypeStruct((8, NT), jnp.int32),
            jax.ShapeDtypeStruct((NT, 128), F32),
            jax.ShapeDtypeStruct((1, 128), F32),
        ],
        scratch_shapes=[pltpu.VMEM((1, 128), F32)],
        compiler_params=_cp(("arbitrary",)),
        name="mmres_router",
    )(o, x, mod, w_out_bf16, g.reshape(1, D), mod, mod, w_hl, br, tri)


DISP_TM = 512


def _row_copy(src, s_row, dst, d_row, sem):
    return pltpu.make_async_copy(src.at[pl.ds(s_row, 1)], dst.at[pl.ds(d_row, 1)], sem)


def _dispatch_kernel(dest_ref, x_ref, g_ref, sh_ref, sc_ref, xs_hbm, h_ref, sem):
    h_ref[...] = _modulate(x_ref[...], g_ref[...], sh_ref[...], sc_ref[...])

    def group(t8, start):
        base = pl.multiple_of(t8 * 8, 8)
        for s in range(8):
            for k in range(2):
                cp = _row_copy(h_ref, base + s, xs_hbm, dest_ref[k, base + s], sem)
                if start:
                    cp.start(priority=k)
                else:
                    cp.wait()

    lax.fori_loop(0, DISP_TM // 8, lambda t8, c: (group(t8, True), c)[1], 0)
    lax.fori_loop(0, DISP_TM // 8, lambda t8, c: (group(t8, False), c)[1], 0)


def _dispatch(dest_t, x, g, mod, l):
    return pl.pallas_call(
        _dispatch_kernel,
        grid=(NT // DISP_TM,),
        in_specs=[
            pl.BlockSpec((2, DISP_TM), lambda i: (0, i), memory_space=pltpu.SMEM),
            pl.BlockSpec((DISP_TM, D), lambda i: (i, 0)),
            pl.BlockSpec((1, D), lambda i: (0, 0)),
            _mod_spec(l, 3, DISP_TM),
            _mod_spec(l, 4, DISP_TM),
        ],
        out_specs=pl.BlockSpec(memory_space=pl.ANY),
        out_shape=jax.ShapeDtypeStruct((MOE_SLOTS, D), F32),
        scratch_shapes=[pltpu.VMEM((DISP_TM, D), F32), pltpu.SemaphoreType.DMA(())],
        compiler_params=_cp(("arbitrary",)),
        name="moe_dispatch",
    )(dest_t, x, g.reshape(1, D), mod, mod)


def _expert_kernel(bexp_ref, bval_ref, bfirst_ref, xs_ref, w1_ref, w3_ref, w2_ref, y_ref,
                   w1b_ref, w3b_ref, w2b_ref):
    del bexp_ref
    i = pl.program_id(0)
    nv = bval_ref[i]

    @pl.when(bfirst_ref[i] == 1)
    def _():
        w1b_ref[...] = w1_ref[...].astype(BF16)
        w3b_ref[...] = w3_ref[...].astype(BF16)
        w2b_ref[...] = w2_ref[...].astype(BF16)

    @pl.when(nv > 0)
    def _():
        row = lax.broadcasted_iota(jnp.int32, (MOE_MB, 1), 0)
        xb = jnp.where(row < nv, xs_ref[...], 0.0).astype(BF16)
        a = _dot(xb, w1b_ref[...])
        b = _dot(xb, w3b_ref[...])
        hid = (a * _sigmoid(a) * b).astype(BF16)
        y_ref[...] = _dot(hid, w2b_ref[...])

    @pl.when(nv <= 0)
    def _():
        y_ref[...] = jnp.zeros_like(y_ref)


def _experts(blk_exp, blk_valid, blk_first, xs, w1, w3, w2, l):
    grid_spec = pltpu.PrefetchScalarGridSpec(
        num_scalar_prefetch=3,
        grid=(MOE_NB,),
        in_specs=[
            pl.BlockSpec((MOE_MB, D), lambda i, be, bv, bf: (i, 0)),
            pl.BlockSpec((None, None, D, D_EXPERT), lambda i, be, bv, bf: (l, be[i], 0, 0)),
            pl.BlockSpec((None, None, D, D_EXPERT), lambda i, be, bv, bf: (l, be[i], 0, 0)),
            pl.BlockSpec((None, None, D_EXPERT, D), lambda i, be, bv, bf: (l, be[i], 0, 0)),
        ],
        out_specs=pl.BlockSpec((MOE_MB, D), lambda i, be, bv, bf: (i, 0)),
        scratch_shapes=[pltpu.VMEM((D, D_EXPERT), BF16), pltpu.VMEM((D, D_EXPERT), BF16),
                        pltpu.VMEM((D_EXPERT, D), BF16)],
    )
    return pl.pallas_call(
        _expert_kernel,
        grid_spec=grid_spec,
        out_shape=jax.ShapeDtypeStruct((MOE_SLOTS, D), F32),
        compiler_params=_cp(("arbitrary",)),
        name="moe_experts",
    )(blk_exp, blk_valid, blk_first, xs, w1, w3, w2)


COMB_TM = 256


def _combine_kernel(dest_ref, mf_ref, x_ref, gate_ref, y_hbm, o_ref, buf_ref, sem):
    def group(t8, start):
        base = pl.multiple_of(t8 * 8, 8)
        for s in range(8):
            for k in range(2):
                cp = _row_copy(y_hbm, dest_ref[k, base + s], buf_ref.at[k], base + s, sem)
                if start:
                    cp.start(priority=k)
                else:
                    cp.wait()

    lax.fori_loop(0, COMB_TM // 8, lambda t8, c: (group(t8, True), c)[1], 0)
    lax.fori_loop(0, COMB_TM // 8, lambda t8, c: (group(t8, False), c)[1], 0)
    mf = mf_ref[...]
    y = mf[:, 0:1] * buf_ref[0] + mf[:, 1:2] * buf_ref[1]
    o_ref[...] = x_ref[...] + gate_ref[...] * y


def _combine(dest_t, mf, x, mod, l, y):
    tm = COMB_TM
    return pl.pallas_call(
        _combine_kernel,
        grid=(NT // tm,),
        in_specs=[
            pl.BlockSpec((2, tm), lambda i: (0, i), memory_space=pltpu.SMEM),
            pl.BlockSpec((tm, 128), lambda i: (i, 0)),
            pl.BlockSpec((tm, D), lambda i: (i, 0)),
            _mod_spec(l, 5, tm),
            pl.BlockSpec(memory_space=pl.ANY),
        ],
        out_specs=pl.BlockSpec((tm, D), lambda i: (i, 0)),
        out_shape=jax.ShapeDtypeStruct((NT, D), F32),
        scratch_shapes=[pltpu.VMEM((2, tm, D), F32), pltpu.SemaphoreType.DMA(())],
        compiler_params=_cp(("arbitrary",)),
        name="moe_combine",
    )(dest_t, mf, x, mod, y)


def _comb_modmm_kernel(dcur_ref, dnxt_ref, mf_ref, x_ref, gate_ref, g_ref, sh_ref, sc_ref, w_ref, y_hbm,
                       xo_ref, p_ref, buf_ref, sem, *, nchunk):
    i = pl.program_id(0)
    n = pl.num_programs(0)
    slot = lax.rem(i, 2)
    tm = x_ref.shape[0]

    def rows(dest_ref, to_slot, lo, hi, start):
        for t in range(lo, hi):
            for k in range(2):
                cp = _row_copy(y_hbm, dest_ref[k, t], buf_ref.at[to_slot, k], t, sem.at[to_slot])
                if start:
                    cp.start(priority=k)
                else:
                    cp.wait()

    def rolled(dest_ref, to_slot, start):
        def body(t8, c):
            base = pl.multiple_of(t8 * 8, 8)
            for s in range(8):
                for k in range(2):
                    cp = _row_copy(y_hbm, dest_ref[k, base + s], buf_ref.at[to_slot, k], base + s,
                                   sem.at[to_slot])
                    if start:
                        cp.start(priority=k)
                    else:
                        cp.wait()
            return c
        lax.fori_loop(0, tm // 8, body, 0)

    @pl.when(i == 0)
    def _():
        rolled(dcur_ref, slot, True)

    rolled(dcur_ref, slot, False)
    mf = mf_ref[...]
    y = mf[:, 0:1] * buf_ref[slot, 0] + mf[:, 1:2] * buf_ref[slot, 1]
    xn = x_ref[...] + gate_ref[...] * y
    xo_ref[...] = xn
    h = _modulate(xn, g_ref[...], sh_ref[...], sc_ref[...]).astype(BF16)
    nout = p_ref.shape[1]
    n_ch = nout // nchunk
    per = -(-tm // n_ch)
    for c in range(n_ch):
        rows(dnxt_ref, 1 - slot, min(c * per, tm), min((c + 1) * per, tm), True)
        p_ref[:, c * nchunk:(c + 1) * nchunk] = _dot(h, w_ref[:, c * nchunk:(c + 1) * nchunk])

    @pl.when(i == n - 1)
    def _():
        rolled(dnxt_ref, 1 - slot, False)


def _comb_modmm(dest_t, mf, x, mod, l, y, g_next, w_bf16, tm=256):
    nout = w_bf16.shape[1]
    nt = NT // tm
    return pl.pallas_call(
        functools.partial(_comb_modmm_kernel, nchunk=256),
        grid=(nt,),
        in_specs=[
            pl.BlockSpec((2, tm), lambda i: (0, i), memory_space=pltpu.SMEM),
            pl.BlockSpec((2, tm), lambda i: (0, jnp.minimum(i + 1, nt - 1)), memory_space=pltpu.SMEM),
            pl.BlockSpec((tm, 128), lambda i: (i, 0)),
            pl.BlockSpec((tm, D), lambda i: (i, 0)),
            _mod_spec(l, 5, tm),
            pl.BlockSpec((1, D), lambda i: (0, 0)),
            _mod_spec(l + 1, 0, tm),
            _mod_spec(l + 1, 1, tm),
            pl.BlockSpec((D, nout), lambda i: (0, 0)),
            pl.BlockSpec(memory_space=pl.ANY),
        ],
        out_specs=[pl.BlockSpec((tm, D), lambda i: (i, 0)), pl.BlockSpec((tm, nout), lambda i: (i, 0))],
        out_shape=[jax.ShapeDtypeStruct((NT, D), F32), jax.ShapeDtypeStruct((NT, nout), F32)],
        scratch_shapes=[pltpu.VMEM((2, 2, tm, D), F32), pltpu.SemaphoreType.DMA((2,))],
        compiler_params=_cp(("arbitrary",)),
        name="moe_combine_modmm",
    )(dest_t, dest_t, mf, x, mod, g_next.reshape(1, D), mod, mod, w_bf16, y)


def _moe_plan(mi, counts):
    cnt = counts[0, N_GROUPS:N_GROUPS + N_EXPERTS].astype(jnp.int32)
    padded = (cnt + MOE_MB - 1) // MOE_MB * MOE_MB
    pad_ends = jnp.cumsum(padded)
    pad_starts = pad_ends - padded
    hit = mi[0:2][None] == jnp.arange(N_EXPERTS, dtype=jnp.int32)[:, None, None]
    dest_t = jnp.sum(jnp.where(hit, pad_starts[:, None, None], 0), axis=0) + mi[2:4]
    blk0 = jnp.arange(MOE_NB, dtype=jnp.int32) * MOE_MB
    blk_exp = jnp.sum((pad_ends[None, :] <= blk0[:, None]).astype(jnp.int32), axis=1)
    blk_exp = jnp.minimum(blk_exp, N_EXPERTS - 1)
    blk_valid = jnp.clip(cnt[blk_exp] - (blk0 - pad_starts[blk_exp]), 0, MOE_MB)
    blk_valid = jnp.where(blk0 < pad_ends[-1], blk_valid, 0).astype(jnp.int32)
    blk_first = jnp.concatenate([jnp.ones((1,), jnp.int32),
                                 (blk_exp[1:] != blk_exp[:-1]).astype(jnp.int32)])
    return dest_t.astype(jnp.int32), blk_exp, blk_valid, blk_first


def _moe_experts(x, g, mod, l, mi, mf, counts, w1, w3, w2):
    dest_t, blk_exp, blk_valid, blk_first = _moe_plan(mi, counts)
    xs = _dispatch(dest_t, x, g, mod, l)
    y = _experts(blk_exp, blk_valid, blk_first, xs, w1, w3, w2, l)
    return dest_t, mf, y


def kernel(x_prompt, x_sample, cache_k_na, cache_v_na, cache_k_swa, cache_v_swa, state_hgrn, c, c_ctx, w_mod, b_mod, norm_mix_g, norm_ffn_g, w_in_even, w_out_even, na_rel_bias, swa_sink, w_in_odd, w_out_odd, conv_w, conv_b, conv_norm_g, conv_norm_b, hgrn_lb_raw, hgrn_norm_g, router_group_w, router_group_b, router_expert_w, router_expert_b, moe_w1, moe_w3, moe_w2, final_norm_g):
    x = jnp.concatenate([x_prompt.reshape(NT_CTX, D), x_sample.reshape(NT_LAT, D)], axis=0)
    cvec = jnp.zeros((16, D), F32).at[0].set(c_ctx).at[1:1 + DEC_BATCH].set(c)
    mod = _mod_table(cvec, w_mod, b_mod)

    lb_p = jax.nn.softmax(hgrn_lb_raw.astype(F32), axis=0)
    lower_bounds = jnp.cumsum(lb_p, axis=0) - lb_p[0:1]
    cos, sin = _rope_tables()
    cache_v_swa_ext = _swa_value_ext(cache_v_swa)
    router_w = _router_weights(router_group_w, router_group_b, router_expert_w, router_expert_b)

    k_na, v_na, k_swa, v_swa, s_hg = [], [], [], [], []
    moe = None
    for l in range(DEPTH):
        w_in = (w_in_even if l % 2 == 0 else w_in_odd)[l // 2].astype(BF16)
        if moe is None:
            p = _modmm(x, norm_mix_g[l], mod, l, w_in, tm=512)
        else:
            x, p = _comb_modmm(*moe[:2], x, mod, l - 1, moe[2], norm_mix_g[l], w_in)
        if l % 2 == 0:
            e = l // 2
            o = _ctx_attn(p, swa_sink[e])
            o = _na_attn(p, cache_k_na, cache_v_na, _na_bias_table(na_rel_bias[e]), e, o)
            o = _swa_attn(p, cache_k_swa, cache_v_swa_ext, swa_sink[e], cos, sin, e, o)
            pc = p[:NT_CTX]
            to_heads = lambda u, nh: u.reshape(BATCH, SEQ, nh, HEAD_DIM).transpose(0, 2, 1, 3)
            k_na.append(to_heads(pc[:, _KA:_KA + 512], NA_HEADS))
            v_na.append(to_heads(pc[:, _VA:_VA + 512], NA_HEADS))
            k_swa.append(to_heads(pc[:, _KB:_KB + 128], SWA_KV_HEADS))
            v_swa.append(to_heads(pc[:, _VB:_VB + 128], SWA_KV_HEADS))
            w_out = w_out_even[e]
        else:
            j = l // 2
            o = _conv_module(p, conv_w[j], conv_b[j], conv_norm_g[j], conv_norm_b[j])
            zero_state = jnp.zeros((BATCH, 2, HG_HEADS, 128, 128), F32)
            o, s_ctx = _hgrn(p, lower_bounds[j], hgrn_norm_g[j], zero_state, o, BATCH, SEQ, 0)
            s0_lat = jnp.swapaxes(state_hgrn[:, j], -1, -2)
            o, _ = _hgrn(p, lower_bounds[j], hgrn_norm_g[j], s0_lat, o, DEC_BATCH, DEC_SEQ, NT_CTX)
            s_hg.append(jnp.swapaxes(s_ctx, -1, -2))
            w_out = w_out_odd[j]
        x, mi, mf, counts = _mmres_router(o, x, mod, l, w_out.astype(BF16), norm_ffn_g[l], *router_w)
        moe = _moe_experts(x, norm_ffn_g[l], mod, l, mi, mf, counts, moe_w1, moe_w3, moe_w2)

    x = _combine(*moe[:2], x, mod, DEPTH - 1, moe[2])
    y_prompt = _final_norm(x, final_norm_g, 0, NT_CTX).reshape(BATCH, SEQ, D)
    y_sample = _final_norm(x, final_norm_g, NT_CTX, NT_LAT).reshape(DEC_BATCH, DEC_SEQ, D)
    return (y_prompt, y_sample, jnp.stack(k_na, axis=1), jnp.stack(v_na, axis=1),
            jnp.stack(k_swa, axis=1), jnp.stack(v_swa, axis=1), jnp.stack(s_hg, axis=1))
```
